```python
import math
import jax, jax.numpy as jnp
from jax import lax
import numpy as np

D_MODEL = 1024
BATCH = 8
SEQ = 4096
DEPTH = 2
DEC_BATCH = 32
DEC_SEQ = 8
PAST_LEN = 16384
PAGE_SIZE = 128

HEAD_DIM = 64
RW_HEADS = 4
RW_WIDTH = RW_HEADS * HEAD_DIM
RW_W_LORA = 64
RW_A_LORA = 64
RW_G_LORA = 128
RW_SHIFT_COLS = 3 * RW_WIDTH + RW_W_LORA + RW_A_LORA + RW_G_LORA
RW_GN_EPS = 64e-5
ML_HEADS = 4
ML_WIDTH = ML_HEADS * HEAD_DIM
ML_CONV = 4
ML_CHUNK = 64
ML_COLS = 3 * ML_WIDTH + 2 * ML_HEADS
NSA_HEADS = 8
NSA_KV_HEADS = 2
NSA_GROUP = NSA_HEADS // NSA_KV_HEADS
NSA_WIDTH = NSA_HEADS * HEAD_DIM
NSA_KV = NSA_KV_HEADS * HEAD_DIM
NSA_BLOCK = 64
NSA_TOPN = 8
NSA_WINDOW = 512
NSA_QBLOCK = 64
WIN_QBLOCK = 128
NSA_COLS = NSA_WIDTH + 6 * NSA_KV + 3 * NSA_HEADS
MIX_WIDTH = RW_WIDTH + ML_WIDTH + NSA_WIDTH
IN_COLS = RW_SHIFT_COLS + ML_COLS + NSA_COLS
MEM_LEN = 256
MEM_HEADS = 4
MEM_WIDTH = MEM_HEADS * HEAD_DIM
D_FF = 2816
NORM_EPS = 1e-6
BIG = 1e9

kernel_name = 'hybrid_rwkv7_mlstm_nsa_macaron_decode_step'


def _split(x, sizes):
    offs = [0]
    for s in sizes:
        offs.append(offs[-1] + s)
    return [x[..., offs[i]:offs[i + 1]] for i in range(len(sizes))]


def _rms_norm(x, g, eps=NORM_EPS):
    xf = x.astype(jnp.float32)
    y = xf * lax.rsqrt(jnp.mean(xf * xf, axis=-1, keepdims=True) + eps)
    return (y * g.astype(jnp.float32)).astype(x.dtype)


def _masked_softmax(s, mask):
    s = jnp.where(mask, s.astype(jnp.float32), -jnp.inf)
    m = jnp.max(s, axis=-1, keepdims=True)
    m = jnp.where(jnp.isfinite(m), m, 0.0)
    e = jnp.exp(s - m)
    return e / jnp.maximum(jnp.sum(e, axis=-1, keepdims=True), 1e-30)


def _swiglu(h, w_up, w_down):
    gate, up = jnp.split(h @ w_up, 2, axis=-1)
    return (jax.nn.silu(gate) * up) @ w_down


def _qblocks(z, qb):
    B, T = z.shape[:2]
    return jnp.moveaxis(z.reshape(B, T // qb, qb, *z.shape[2:]), 1, 0)


def _unblock(z):
    z = jnp.moveaxis(z, 0, 1)
    return z.reshape(z.shape[0], -1, *z.shape[3:])


def _rwkv7(p, shift_prev, S0, mu, w0, w2, a0, a2, g2, kk_s, ka, rk, gn_g, gn_b):
    B, T, _ = p.shape
    H, N = RW_HEADS, HEAD_DIM
    f32 = jnp.float32
    p_prev = jnp.concatenate([shift_prev.astype(p.dtype), p[:, :-1]], axis=1)
    xm = p + (p_prev - p) * mu
    r, k, v, wd, ad, gd = _split(xm, [RW_WIDTH, RW_WIDTH, RW_WIDTH, RW_W_LORA, RW_A_LORA, RW_G_LORA])
    w = -jax.nn.softplus(-(w0 + jnp.tanh(wd) @ w2)) - 0.5
    a = jax.nn.sigmoid(a0 + ad @ a2)
    g = jax.nn.sigmoid(gd) @ g2
    kk = (k * kk_s).reshape(B, T, H, N).astype(f32)
    kk = kk / jnp.maximum(jnp.sqrt(jnp.sum(kk * kk, axis=-1, keepdims=True)), 1e-12)
    k = k * (1.0 + (a - 1.0) * ka)
    hd = lambda z: z.reshape(B, T, H, N).astype(f32)
    rh, kh, vh, ah = hd(r), hd(k), hd(v), hd(a)
    decay = jnp.exp(-jnp.exp(hd(w)))
    tm = lambda z: jnp.moveaxis(z, 1, 0)

    def step(S, inp):
        r_t, w_t, k_t, v_t, kk_t, a_t = inp
        sa = jnp.einsum('bhij,bhj->bhi', S, -kk_t)
        S = (S * w_t[:, :, None, :] + sa[..., None] * (kk_t * a_t)[:, :, None, :]
             + v_t[..., None] * k_t[:, :, None, :])
        return S, jnp.einsum('bhij,bhj->bhi', S, r_t)

    S_T, ys = lax.scan(step, S0.astype(f32), (tm(rh), tm(decay), tm(kh), tm(vh), tm(kk), tm(ah)))
    y = jnp.moveaxis(ys, 0, 1)
    mean = jnp.mean(y, axis=-1, keepdims=True)
    var = jnp.mean(jnp.square(y - mean), axis=-1, keepdims=True)
    y = ((y - mean) * lax.rsqrt(var + RW_GN_EPS)).reshape(B, T, RW_WIDTH) * gn_g + gn_b
    bonus = (jnp.sum(rh * kh * rk, axis=-1, keepdims=True) * vh).reshape(B, T, RW_WIDTH)
    out = ((y + bonus) * g).astype(p.dtype)
    return out, p[:, -1:], S_T


def _mlstm_chunkwise(q, k, v, logi, logf, C0, n0, m0):
    B, H, T, dh = q.shape
    Lc = ML_CHUNK if T % ML_CHUNK == 0 else T
    nc = T // Lc

    def chunks(z):
        return jnp.moveaxis(z.reshape(B, H, nc, Lc, *z.shape[3:]), 2, 0)

    causal = jnp.tril(jnp.ones((Lc, Lc), dtype=bool))

    def step(carry, inp):
        C, n, m = carry
        qc, kc, vc, ic, fc = inp
        b = jnp.cumsum(fc, axis=-1)
        dmat = jnp.where(causal, b[..., :, None] - b[..., None, :] + ic[..., None, :], -jnp.inf)
        inter = b + m[..., None]
        m_t = jnp.maximum(inter, jnp.max(dmat, axis=-1))
        a = jnp.einsum('bhtd,bhsd->bhts', qc, kc) * jnp.exp(dmat - m_t[..., None])
        sc = jnp.exp(inter - m_t)
        num = sc[..., None] * jnp.einsum('bhij,bhtj->bhti', C, qc) + jnp.einsum('bhts,bhsi->bhti', a, vc)
        den = sc * jnp.einsum('bhj,bhtj->bht', n, qc) + jnp.sum(a, axis=-1)
        h = num / jnp.maximum(jnp.abs(den), jnp.exp(-m_t))[..., None]
        b_end = b[..., -1]
        gs = b_end[..., None] - b + ic
        m_new = jnp.maximum(b_end + m, jnp.max(gs, axis=-1))
        dec = jnp.exp(b_end + m - m_new)
        wg = jnp.exp(gs - m_new[..., None])
        C = dec[..., None, None] * C + jnp.einsum('bhs,bhsi,bhsj->bhij', wg, vc, kc)
        n = dec[..., None] * n + jnp.einsum('bhs,bhsj->bhj', wg, kc)
        return (C, n, m_new), h

    (C, n, m), hs = lax.scan(step, (C0, n0, m0), tuple(chunks(z) for z in (q, k, v, logi, logf)))
    h = jnp.moveaxis(hs, 0, 2).reshape(B, H, T, dh)
    return h, C, n, m


def _mlstm(p, conv_prev, C0, n0, m0, conv_w, conv_b, wq, wk, gate_b, norm_g, skip):
    B, T, _ = p.shape
    H, dh = ML_HEADS, HEAD_DIM
    f32 = jnp.float32
    qk_in, v, o_pre, i_pre, f_pre = _split(p, [ML_WIDTH, ML_WIDTH, ML_WIDTH, ML_HEADS, ML_HEADS])
    xpad = jnp.concatenate([conv_prev.astype(p.dtype), qk_in], axis=1)
    conv = conv_b + sum(conv_w[j] * xpad[:, j:j + T] for j in range(ML_CONV))
    ca = jax.nn.silu(conv)
    cah = ca.reshape(B, T, H, dh)
    q = jnp.einsum('bthd,hde->bhte', cah, wq).astype(f32)
    k = (jnp.einsum('bthd,hde->bhte', cah, wk) * dh ** -0.5).astype(f32)
    vh = jnp.moveaxis(v.reshape(B, T, H, dh), 1, 2).astype(f32)
    logi = jnp.moveaxis((i_pre + gate_b[0]).astype(f32), 1, 2)
    logf = jax.nn.log_sigmoid(jnp.moveaxis((f_pre + gate_b[1]).astype(f32), 1, 2))
    h, C, n, m = _mlstm_chunkwise(q, k, vh, logi, logf, C0.astype(f32), n0.astype(f32), m0.astype(f32))
    h = jnp.moveaxis(h, 1, 2)
    h = h * lax.rsqrt(jnp.mean(h * h, axis=-1, keepdims=True) + NORM_EPS)
    out = (h.reshape(B, T, ML_WIDTH) * norm_g + skip * ca) * jax.nn.sigmoid(o_pre)
    return out.astype(p.dtype), xpad[:, T:], C, n, m


def _nsa_compress(kv_full, wpos, wc, kc_g):
    B, L = kv_full.shape[:2]
    nc = L // NSA_BLOCK
    blk = kv_full[:, :nc * NSA_BLOCK].reshape(B, nc, NSA_BLOCK, *kv_full.shape[2:])
    c = jnp.einsum('bnlcgd,lcgd->bncgd', blk, wpos)
    c = jnp.einsum('bncgd,cgde->bncge', c, wc)
    return _rms_norm(c[:, :, 0], kc_g), c[:, :, 1]


def _nsa_cmp_attend(q, qpos, ck, cv):
    dh = q.shape[-1]
    nc = ck.shape[1]
    s = jnp.einsum('btgrd,bngd->btgrn', q, ck) * dh ** -0.5
    vis = (jnp.arange(nc) + 1) * NSA_BLOCK - 1 <= qpos[:, None]
    p = _masked_softmax(s, vis[None, :, None, None, :])
    o = jnp.einsum('btgrn,bngd->btgrd', p, cv.astype(jnp.float32))
    return o, jnp.sum(p, axis=3)


def _nsa_select(imp, qpos, n_blocks):
    nc = imp.shape[-1]
    imp = jnp.pad(imp, ((0, 0), (0, 0), (0, 0), (0, n_blocks - nc)))
    j = jnp.arange(n_blocks)[None, :]
    cur = (qpos // NSA_BLOCK)[:, None]
    forced = (j == 0) | (j == cur) | (j == cur - 1)
    score = jnp.where(forced[None, :, None, :], BIG,
                      jnp.where((j <= cur)[None, :, None, :], imp, -BIG))
    _, idx = lax.top_k(score, min(NSA_TOPN, n_blocks))
    return idx


def _nsa_sel_attend(q, qpos, idx, kvb):
    B, T, G, R, dh = q.shape
    kb, vb = kvb[..., 0, :], kvb[..., 1, :]
    s = jnp.einsum('btgrd,btgkld->btgrkl', q, kb) * dh ** -0.5
    kpos = idx[..., None] * NSA_BLOCK + jnp.arange(NSA_BLOCK)
    mask = kpos <= qpos[None, :, None, None, None]
    p = _masked_softmax(s.reshape(B, T, G, R, -1), mask.reshape(B, T, G, 1, -1))
    return jnp.einsum('btgrn,btgnd->btgrd', p, vb.reshape(B, T, G, -1, dh).astype(jnp.float32))


def _nsa_win_attend(q, qpos, kv, kpos):
    dh = q.shape[-1]
    s = jnp.einsum('btgrd,bsgd->btgrs', q, kv[:, :, 0]) * dh ** -0.5
    d = qpos[:, None] - kpos[None, :]
    mask = (kpos[None, :] >= 0) & (d >= 0) & (d < NSA_WINDOW)
    p = _masked_softmax(s, mask[None, :, None, None, :])
    return jnp.einsum('btgrs,bsgd->btgrd', p, kv[:, :, 1].astype(jnp.float32))


def _nsa(p, is_prompt, page_table, pool_cmp, pool_sel, win_prev, qk_g, wpos, wc, gate_b):
    B, T, _ = p.shape
    G, R, dh = NSA_KV_HEADS, NSA_GROUP, HEAD_DIM
    q, kc, vc, ks, vs, kw, vw, gates = _split(p, [NSA_WIDTH] + [NSA_KV] * 6 + [3 * NSA_HEADS])
    q = _rms_norm(q.reshape(B, T, G, R, dh), qk_g[0])
    sh = lambda z: z.reshape(B, T, G, dh)
    kv_c = jnp.stack([sh(kc), sh(vc)], axis=2)
    kv_s = jnp.stack([_rms_norm(sh(ks), qk_g[2]), sh(vs)], axis=2)
    kv_w = jnp.stack([_rms_norm(sh(kw), qk_g[3]), sh(vw)], axis=2)
    bidx = jnp.arange(B)[:, None, None, None]
    gidx = jnp.arange(G)[None, None, :, None]
    if is_prompt:
        pos0 = 0
        kv_c_full = kv_c
    else:
        pos0 = page_table.shape[1] * PAGE_SIZE
        past = pool_cmp[page_table].reshape(B, pos0, 2, G, dh)
        kv_c_full = jnp.concatenate([past.astype(kv_c.dtype), kv_c], axis=1)
    qpos = pos0 + jnp.arange(T)
    n_blocks = -(-(pos0 + T) // NSA_BLOCK)
    ck, cv = _nsa_compress(kv_c_full, wpos, wc, qk_g[1])
    o_c, imp = _nsa_cmp_attend(q, qpos, ck, cv)
    idx = _nsa_select(imp, qpos, n_blocks)
    if is_prompt:
        blocks = kv_s.reshape(B, n_blocks, NSA_BLOCK, 2, G, dh)

        def sel_block(args):
            qb, ib, pb = args
            return _nsa_sel_attend(qb, pb, ib, blocks[bidx, ib, :, :, gidx])

        o_s = _unblock(lax.map(sel_block, (_qblocks(q, NSA_QBLOCK), _qblocks(idx, NSA_QBLOCK),
                                           qpos.reshape(T // NSA_QBLOCK, NSA_QBLOCK))))
        kv_pad = jnp.pad(kv_w, ((0, 0), (NSA_WINDOW, 0), (0, 0), (0, 0), (0, 0)))
        span = NSA_WINDOW + WIN_QBLOCK

        def win_block(args):
            qb, i = args
            start = i * WIN_QBLOCK
            kvb = lax.dynamic_slice_in_dim(kv_pad, start, span, axis=1)
            kpos = start - NSA_WINDOW + jnp.arange(span)
            return _nsa_win_attend(qb, start + jnp.arange(WIN_QBLOCK), kvb, kpos)

        o_w = _unblock(lax.map(win_block, (_qblocks(q, WIN_QBLOCK), jnp.arange(T // WIN_QBLOCK))))
        new_win = kv_w[:, T - min(NSA_WINDOW, T):]
    else:
        bpp = PAGE_SIZE // NSA_BLOCK
        n_past = pos0 // NSA_BLOCK
        n_new = n_blocks - n_past
        pool_blk = pool_sel.reshape(-1, NSA_BLOCK, 2, G, dh)
        new_blk = jnp.pad(kv_s, ((0, 0), (0, n_new * NSA_BLOCK - T), (0, 0), (0, 0), (0, 0)))
        new_blk = new_blk.reshape(B, n_new, NSA_BLOCK, 2, G, dh)
        jp = jnp.minimum(idx, n_past - 1)
        phys = page_table[bidx, jp // bpp] * bpp + jp % bpp
        g_pool = pool_blk[phys, :, :, gidx].astype(kv_s.dtype)
        g_new = new_blk[bidx, jnp.clip(idx - n_past, 0, n_new - 1), :, :, gidx]
        kvb = jnp.where((idx < n_past)[..., None, None, None], g_pool, g_new)
        o_s = _nsa_sel_attend(q, qpos, idx, kvb)
        wb = win_prev.shape[1]
        kv_all = jnp.concatenate([win_prev.astype(kv_w.dtype), kv_w], axis=1)
        o_w = _nsa_win_attend(q, qpos, kv_all, pos0 - wb + jnp.arange(wb + T))
        new_win = kv_all[:, wb + T - min(NSA_WINDOW, wb + T):]
    gt = jax.nn.sigmoid((gates + gate_b).astype(jnp.float32)).reshape(B, T, G, R, 3)
    o = gt[..., 0:1] * o_c + gt[..., 1:2] * o_s + gt[..., 2:3] * o_w
    return o.reshape(B, T, NSA_WIDTH).astype(p.dtype), kv_c, kv_s, new_win


def _mem_kv(mem, src_g, w_kv, k_g):
    B, M, _ = mem.shape
    kv = (_rms_norm(mem, src_g) @ w_kv).reshape(B, M, 2, MEM_HEADS, HEAD_DIM)
    return jnp.stack([_rms_norm(kv[:, :, 0], k_g), kv[:, :, 1]], axis=2)


def _mem_attend(h, kv, w_q, q_g, w_o):
    B, T, _ = h.shape
    q = _rms_norm((h @ w_q).reshape(B, T, MEM_HEADS, HEAD_DIM), q_g)
    s = jnp.einsum('bthd,bshd->bths', q, kv[:, :, 0].astype(q.dtype)).astype(jnp.float32) * HEAD_DIM ** -0.5
    pr = jax.nn.softmax(s, axis=-1)
    o = jnp.einsum('bths,bshd->bthd', pr, kv[:, :, 1].astype(jnp.float32))
    return o.reshape(B, T, MEM_WIDTH).astype(h.dtype) @ w_o


def _layer(x, W, st, is_prompt, page_table, mem):
    B, T, _ = x.shape
    dt = x.dtype
    g = W['norm_g']
    x = x + 0.5 * _swiglu(_rms_norm(x, g[0]), W['ffa_up'], W['ffa_down'])
    p = _rms_norm(x, g[1]) @ W['w_in']
    p_rw, p_ml, p_nsa = _split(p, [RW_SHIFT_COLS, ML_COLS, NSA_COLS])
    if is_prompt:
        z = lambda *shape: jnp.zeros(shape, jnp.float32)
        st = {'rw_shift': z(B, 1, RW_SHIFT_COLS), 'rw_S': z(B, RW_HEADS, HEAD_DIM, HEAD_DIM),
              'ml_conv': z(B, ML_CONV - 1, ML_WIDTH), 'ml_C': z(B, ML_HEADS, HEAD_DIM, HEAD_DIM),
              'ml_n': z(B, ML_HEADS, HEAD_DIM), 'ml_m': z(B, ML_HEADS),
              'nsa_cmp': None, 'nsa_sel': None, 'nsa_win': None,
              'mem_kv': _mem_kv(mem, g[3], W['mem_w_kv'], W['mem_qk_g'][1])}
    o_rw, rw_shift, rw_S = _rwkv7(p_rw, st['rw_shift'], st['rw_S'], W['rw_mu'], W['rw_w0'], W['rw_w2'],
                                  W['rw_a0'], W['rw_a2'], W['rw_g2'], W['rw_kk'], W['rw_ka'], W['rw_rk'],
                                  W['rw_gn_g'], W['rw_gn_b'])
    o_ml, ml_conv, ml_C, ml_n, ml_m = _mlstm(p_ml, st['ml_conv'], st['ml_C'], st['ml_n'], st['ml_m'],
                                             W['ml_conv_w'], W['ml_conv_b'], W['ml_wq'], W['ml_wk'],
                                             W['ml_gate_b'], W['ml_norm_g'], W['ml_skip'])
    o_nsa, kv_c, kv_s, win = _nsa(p_nsa, is_prompt, page_table, st['nsa_cmp'], st['nsa_sel'], st['nsa_win'],
                                  W['nsa_qk_g'], W['nsa_wpos'], W['nsa_wc'], W['nsa_gate_b'])
    x = x + jnp.concatenate([o_rw, o_ml, o_nsa], axis=-1) @ W['w_out']
    x = x + _mem_attend(_rms_norm(x, g[2]), st['mem_kv'], W['mem_w_q'], W['mem_qk_g'][0], W['mem_w_o'])
    x = x + 0.5 * _swiglu(_rms_norm(x, g[4]), W['ffb_up'], W['ffb_down'])
    new = {'nsa_cmp': kv_c.astype(dt), 'nsa_sel': kv_s.astype(dt), 'nsa_win': win.astype(dt),
           'rw_shift': rw_shift.astype(dt), 'rw_S': rw_S.astype(dt), 'ml_conv': ml_conv.astype(dt),
           'ml_C': ml_C.astype(dt), 'ml_n': ml_n.astype(dt), 'ml_m': ml_m.astype(dt)}
    if is_prompt:
        new['mem_kv'] = st['mem_kv'].astype(dt)
    return x, new


def setup_inputs(seed: int = 0) -> dict:
    key = jax.random.key(seed)
    keys = iter(jax.random.split(key, 80))

    def nrm(shape, scale=1.0):
        return scale * jax.random.normal(next(keys), shape, jnp.float32)

    def gain(shape):
        return 1.0 + 0.02 * nrm(shape)

    L, G, dh = DEPTH, NSA_KV_HEADS, HEAD_DIM
    n_pages = PAST_LEN // PAGE_SIZE
    n_used = DEC_BATCH * n_pages
    n_phys = n_used + (n_used + 3) // 4
    wb = min(NSA_WINDOW, PAST_LEN)
    page_table = jax.random.permutation(next(keys), n_phys)[:n_used].reshape(DEC_BATCH, n_pages).astype(jnp.int32)
    return {
        'x_prompt': nrm((BATCH, SEQ, D_MODEL)),
        'x_sample': nrm((DEC_BATCH, DEC_SEQ, D_MODEL)),
        'cache_nsa_cmp': nrm((L, n_phys, PAGE_SIZE, 2, G, dh)),
        'cache_nsa_sel': nrm((L, n_phys, PAGE_SIZE, 2, G, dh)),
        'cache_nsa_win': nrm((L, DEC_BATCH, wb, 2, G, dh)),
        'cache_mem_kv': nrm((L, DEC_BATCH, MEM_LEN, 2, MEM_HEADS, dh)),
        'state_rwkv_shift': nrm((L, DEC_BATCH, 1, RW_SHIFT_COLS)),
        'state_rwkv_S': nrm((L, DEC_BATCH, RW_HEADS, dh, dh), 0.5),
        'state_mlstm_conv': nrm((L, DEC_BATCH, ML_CONV - 1, ML_WIDTH)),
        'state_mlstm_C': nrm((L, DEC_BATCH, ML_HEADS, dh, dh), 0.5),
        'state_mlstm_n': nrm((L, DEC_BATCH, ML_HEADS, dh), 0.5),
        'state_mlstm_m': nrm((L, DEC_BATCH, ML_HEADS)),
        'page_table': page_table,
        'mem_prompt': nrm((BATCH, MEM_LEN, D_MODEL)),
        'norm_g': gain((L, 5, D_MODEL)),
        'ffa_up': nrm((L, D_MODEL, 2 * D_FF), D_MODEL ** -0.5),
        'ffa_down': nrm((L, D_FF, D_MODEL), D_FF ** -0.5),
        'ffb_up': nrm((L, D_MODEL, 2 * D_FF), D_MODEL ** -0.5),
        'ffb_down': nrm((L, D_FF, D_MODEL), D_FF ** -0.5),
        'w_in': nrm((L, D_MODEL, IN_COLS), D_MODEL ** -0.5),
        'w_out': nrm((L, MIX_WIDTH, D_MODEL), MIX_WIDTH ** -0.5),
        'rw_mu': jax.random.uniform(next(keys), (L, RW_SHIFT_COLS), jnp.float32),
        'rw_w0': jax.random.uniform(next(keys), (L, RW_WIDTH), jnp.float32, -6.0, -1.0),
        'rw_w2': nrm((L, RW_W_LORA, RW_WIDTH), 0.5 * RW_W_LORA ** -0.5),
        'rw_a0': nrm((L, RW_WIDTH), 0.1),
        'rw_a2': nrm((L, RW_A_LORA, RW_WIDTH), 0.5 * RW_A_LORA ** -0.5),
        'rw_g2': nrm((L, RW_G_LORA, RW_WIDTH), RW_G_LORA ** -0.5),
        'rw_kk': 1.0 + nrm((L, RW_WIDTH), 0.1),
        'rw_ka': 1.0 + nrm((L, RW_WIDTH), 0.1),
        'rw_rk': nrm((L, RW_HEADS, dh), 0.1),
        'rw_gn_g': gain((L, RW_WIDTH)),
        'rw_gn_b': nrm((L, RW_WIDTH), 0.02),
        'ml_conv_w': nrm((L, ML_CONV, ML_WIDTH), 0.5),
        'ml_conv_b': nrm((L, ML_WIDTH), 0.02),
        'ml_wq': nrm((L, ML_HEADS, dh, dh), dh ** -0.5),
        'ml_wk': nrm((L, ML_HEADS, dh, dh), dh ** -0.5),
        'ml_gate_b': jnp.stack([nrm((L, ML_HEADS), 0.1),
                                jnp.linspace(3.0, 6.0, ML_HEADS)[None, :] + nrm((L, ML_HEADS), 0.1)], axis=1),
        'ml_norm_g': gain((L, ML_WIDTH)),
        'ml_skip': 1.0 + nrm((L, ML_WIDTH), 0.1),
        'nsa_qk_g': gain((L, 4, dh)),
        'nsa_wpos': NSA_BLOCK ** -0.5 * (1.0 + nrm((L, NSA_BLOCK, 2, G, dh), 0.3)),
        'nsa_wc': nrm((L, 2, G, dh, dh), dh ** -0.5),
        'nsa_gate_b': nrm((L, 3 * NSA_HEADS), 0.1),
        'mem_w_q': nrm((L, D_MODEL, MEM_WIDTH), D_MODEL ** -0.5),
        'mem_w_kv': nrm((L, D_MODEL, 2 * MEM_WIDTH), D_MODEL ** -0.5),
        'mem_qk_g': gain((L, 2, dh)),
        'mem_w_o': nrm((L, MEM_WIDTH, D_MODEL), MEM_WIDTH ** -0.5),
    }


def reference(x_prompt, x_sample, cache_nsa_cmp, cache_nsa_sel, cache_nsa_win, cache_mem_kv,
              state_rwkv_shift, state_rwkv_S, state_mlstm_conv, state_mlstm_C, state_mlstm_n, state_mlstm_m,
              page_table, mem_prompt, norm_g, ffa_up, ffa_down, ffb_up, ffb_down, w_in, w_out,
              rw_mu, rw_w0, rw_w2, rw_a0, rw_a2, rw_g2, rw_kk, rw_ka, rw_rk, rw_gn_g, rw_gn_b,
              ml_conv_w, ml_conv_b, ml_wq, ml_wk, ml_gate_b, ml_norm_g, ml_skip,
              nsa_qk_g, nsa_wpos, nsa_wc, nsa_gate_b, mem_w_q, mem_w_kv, mem_qk_g, mem_w_o):
    y_p, y_s = x_prompt, x_sample
    new_p, new_s = [], []
    for l in range(DEPTH):
        W = {'norm_g': norm_g[l], 'ffa_up': ffa_up[l], 'ffa_down': ffa_down[l], 'ffb_up': ffb_up[l],
             'ffb_down': ffb_down[l], 'w_in': w_in[l], 'w_out': w_out[l], 'rw_mu': rw_mu[l], 'rw_w0': rw_w0[l],
             'rw_w2': rw_w2[l], 'rw_a0': rw_a0[l], 'rw_a2': rw_a2[l], 'rw_g2': rw_g2[l], 'rw_kk': rw_kk[l],
             'rw_ka': rw_ka[l], 'rw_rk': rw_rk[l], 'rw_gn_g': rw_gn_g[l], 'rw_gn_b': rw_gn_b[l],
             'ml_conv_w': ml_conv_w[l], 'ml_conv_b': ml_conv_b[l], 'ml_wq': ml_wq[l], 'ml_wk': ml_wk[l],
             'ml_gate_b': ml_gate_b[l], 'ml_norm_g': ml_norm_g[l], 'ml_skip': ml_skip[l],
             'nsa_qk_g': nsa_qk_g[l], 'nsa_wpos': nsa_wpos[l], 'nsa_wc': nsa_wc[l], 'nsa_gate_b': nsa_gate_b[l],
             'mem_w_q': mem_w_q[l], 'mem_w_kv': mem_w_kv[l], 'mem_qk_g': mem_qk_g[l], 'mem_w_o': mem_w_o[l]}
        st = {'nsa_cmp': cache_nsa_cmp[l], 'nsa_sel': cache_nsa_sel[l], 'nsa_win': cache_nsa_win[l],
              'mem_kv': cache_mem_kv[l], 'rw_shift': state_rwkv_shift[l], 'rw_S': state_rwkv_S[l],
              'ml_conv': state_mlstm_conv[l], 'ml_C': state_mlstm_C[l], 'ml_n': state_mlstm_n[l],
              'ml_m': state_mlstm_m[l]}
        y_p, sp = _layer(y_p, W, None, True, None, mem_prompt)
        y_s, ss = _layer(y_s, W, st, False, page_table, None)
        new_p.append(sp)
        new_s.append(ss)
    P = lambda name: jnp.stack([d[name] for d in new_p])
    S = lambda name: jnp.stack([d[name] for d in new_s])
    return (y_p, y_s,
            P('nsa_cmp'), S('nsa_cmp'), P('nsa_sel'), S('nsa_sel'), P('nsa_win'), S('nsa_win'),
            P('mem_kv'),
            P('rw_shift'), S('rw_shift'), P('rw_S'), S('rw_S'),
            P('ml_conv'), S('ml_conv'), P('ml_C'), S('ml_C'), P('ml_n'), S('ml_n'), P('ml_m'), S('ml_m'))
```

```python
import functools

import jax
import jax.numpy as jnp
from jax import lax
from jax.experimental import pallas as pl
from jax.experimental.pallas import tpu as pltpu

F32 = jnp.float32
BF16 = jnp.bfloat16

HEAD_DIM = 64
RW_HEADS = 4
RW_WIDTH = RW_HEADS * HEAD_DIM
RW_W_LORA = 64
RW_A_LORA = 64
RW_G_LORA = 128
RW_COLS = 3 * RW_WIDTH + RW_W_LORA + RW_A_LORA + RW_G_LORA
RW_GN_EPS = 64e-5
ML_HEADS = 4
ML_WIDTH = ML_HEADS * HEAD_DIM
ML_CONV = 4
ML_CHUNK = 64
ML_COLS = 3 * ML_WIDTH + 2 * ML_HEADS
NSA_HEADS = 8
NSA_KV_HEADS = 2
NSA_GROUP = NSA_HEADS // NSA_KV_HEADS
NSA_WIDTH = NSA_HEADS * HEAD_DIM
NSA_KV = NSA_KV_HEADS * HEAD_DIM
NSA_BLOCK = 64
BLOCK_SHIFT = 6
NSA_TOPN = 8
NSA_WINDOW = 512
NSA_COLS = NSA_WIDTH + 6 * NSA_KV + 3 * NSA_HEADS
MEM_HEADS = 4
MEM_WIDTH = MEM_HEADS * HEAD_DIM
PAGE_SIZE = 128
NORM_EPS = 1e-6
BIG = 1e9
LANES = 128
VMEM_LIMIT = 56 * 1024 * 1024

P_RW = 0
P_ML = P_RW + RW_COLS
P_Q = P_ML + 3 * ML_WIDTH
P_KVC = P_Q + NSA_WIDTH
P_KVS = P_KVC + 2 * NSA_KV
P_KVW = P_KVS + 2 * NSA_KV
P_GATES = P_KVW + 2 * NSA_KV
P_TOTAL = P_GATES + LANES


def _dot(a, b):
    return jnp.dot(a.astype(BF16), b.astype(BF16), preferred_element_type=F32)


def _dot_nt(a, b):
    return lax.dot_general(a.astype(BF16), b.astype(BF16), (((1,), (1,)), ((), ())),
                           preferred_element_type=F32)


def _dot_tn(a, b):
    return lax.dot_general(a.astype(BF16), b.astype(BF16), (((0,), (0,)), ((), ())),
                           preferred_element_type=F32)


def _split2(x):
    hi = x.astype(BF16)
    lo = (x - hi.astype(F32)).astype(BF16)
    return hi, lo


def _split3(x):
    hi = x.astype(BF16)
    r = x - hi.astype(F32)
    mid = r.astype(BF16)
    lo = (r - mid.astype(F32)).astype(BF16)
    return hi, mid, lo


def _dot_sel(sel, x):
    hi, mid, lo = _split3(x)
    s = sel.astype(BF16)
    d = lambda t: jnp.dot(s, t, preferred_element_type=F32)
    return d(hi) + d(mid) + d(lo)


def _dot_x_sel(x, sel):
    hi, mid, lo = _split3(x)
    s = sel.astype(BF16)
    d = lambda t: jnp.dot(t, s, preferred_element_type=F32)
    return d(hi) + d(mid) + d(lo)


def _dot_hi(a, b):
    ah, al = _split2(a)
    bh, bl = _split2(b)
    d = lambda u, v: jnp.dot(u, v, preferred_element_type=F32)
    return d(ah, bh) + d(al, bh) + d(ah, bl)


def _dot_nt_hi(a, b):
    ah, al = _split2(a)
    bh, bl = _split2(b)
    d = lambda u, v: lax.dot_general(u, v, (((1,), (1,)), ((), ())), preferred_element_type=F32)
    return d(ah, bh) + d(al, bh) + d(ah, bl)


def _seg_mean_sq(x, seg_avg):
    hi, lo = _split2(x * x)
    d = lambda t: jnp.dot(t, seg_avg, preferred_element_type=F32)
    return d(hi) + d(lo)


def _rms_rows(x, g):
    return x * lax.rsqrt(jnp.mean(x * x, axis=-1, keepdims=True) + NORM_EPS) * g


def _sigmoid(x):
    return 1.0 / (1.0 + jnp.exp(-x))


def _softplus(x):
    return jnp.maximum(x, 0.0) + jnp.log(1.0 + jnp.exp(-jnp.abs(x)))


def _row_tile(n, target):
    t = min(n, target)
    while n % t:
        t //= 2
    return t


def _const_spec(shape):
    nd = len(shape)
    return pl.BlockSpec(shape, lambda *_: (0,) * nd, pipeline_mode=pl.Buffered(1))


def _seg_avg_matrix(width, scale):
    i = jnp.arange(width) // HEAD_DIM
    return (jnp.where(i[:, None] == i[None, :], scale, 0.0)).astype(BF16)


def _ffn_body(x_ref, g_ref, wg_ref, wu_ref, wd_ref, o_ref, *, f_chunk):
    x = x_ref[...]
    h = _rms_rows(x, g_ref[...]).astype(BF16)
    d_ff = wg_ref.shape[1]
    acc = jnp.zeros_like(x)
    for c in range(d_ff // f_chunk):
        sl = slice(c * f_chunk, (c + 1) * f_chunk)
        gate = jnp.dot(h, wg_ref[:, sl], preferred_element_type=F32)
        up = jnp.dot(h, wu_ref[:, sl], preferred_element_type=F32)
        act = (gate * _sigmoid(gate) * up).astype(BF16)
        acc = acc + jnp.dot(act, wd_ref[sl, :], preferred_element_type=F32)
    o_ref[...] = x + 0.5 * acc


def _ffn(x, g, w_up, w_down):
    n, d = x.shape
    d_ff = w_down.shape[0]
    tm = _row_tile(n, 512)
    f_chunk = d_ff // 2 if (d_ff // 2) % LANES == 0 else d_ff
    wg = w_up[:, :d_ff].astype(BF16)
    wu = w_up[:, d_ff:].astype(BF16)
    wd = w_down.astype(BF16)
    row = pl.BlockSpec((tm, d), lambda i: (i, 0))
    return pl.pallas_call(
        functools.partial(_ffn_body, f_chunk=f_chunk),
        grid=(n // tm,),
        in_specs=[row, _const_spec((1, d)), _const_spec((d, d_ff)), _const_spec((d, d_ff)),
                  _const_spec((d_ff, d))],
        out_specs=row,
        out_shape=jax.ShapeDtypeStruct((n, d), F32),
        compiler_params=pltpu.CompilerParams(dimension_semantics=("parallel",),
                                             vmem_limit_bytes=VMEM_LIMIT),
        name="ffn",
    )(x, g.reshape(1, d), wg, wu, wd)


def _in_proj_body(x_ref, g_ref, w_ref, gq_ref, gks_ref, gkw_ref, avg_ref,
                  rw_ref, ml_ref, q_ref, kvc_ref, kvs_ref, kvw_ref, gates_ref):
    h = _rms_rows(x_ref[...], g_ref[...]).astype(BF16)
    p = jnp.dot(h, w_ref[...], preferred_element_type=F32)
    rw_ref[...] = p[:, P_RW:P_ML]
    ml_ref[...] = p[:, P_ML:P_Q]
    gates_ref[...] = p[:, P_GATES:P_TOTAL]
    kvc_ref[...] = p[:, P_KVC:P_KVS]
    avg = avg_ref[...]

    def head_norm(z, g):
        w = z.shape[1]
        return z * lax.rsqrt(_seg_mean_sq(z, avg[:w, :w]) + NORM_EPS) * g

    q_ref[...] = head_norm(p[:, P_Q:P_KVC], gq_ref[...])
    kvs_ref[:, :NSA_KV] = head_norm(p[:, P_KVS:P_KVS + NSA_KV], gks_ref[...])
    kvs_ref[:, NSA_KV:] = p[:, P_KVS + NSA_KV:P_KVW]
    kvw_ref[:, :NSA_KV] = head_norm(p[:, P_KVW:P_KVW + NSA_KV], gkw_ref[...])
    kvw_ref[:, NSA_KV:] = p[:, P_KVW + NSA_KV:P_GATES]


def _permute_w_in(w_in):
    d = w_in.shape[0]
    o_ml = RW_COLS
    o_nsa = RW_COLS + ML_COLS
    ml_gates = w_in[:, o_ml + 3 * ML_WIDTH:o_nsa]
    nsa_gates = w_in[:, o_nsa + NSA_WIDTH + 6 * NSA_KV:]
    pad = jnp.zeros((d, LANES - 2 * ML_HEADS - 3 * NSA_HEADS), w_in.dtype)
    return jnp.concatenate([w_in[:, :o_ml + 3 * ML_WIDTH], w_in[:, o_nsa:o_nsa + NSA_WIDTH + 6 * NSA_KV],
                            ml_gates, nsa_gates, pad], axis=1).astype(BF16)


def _in_proj(x, g, w_perm, qk_g):
    n, d = x.shape
    tm = _row_tile(n, 512)
    row = lambda w: pl.BlockSpec((tm, w), lambda i: (i, 0))
    widths = (RW_COLS, 3 * ML_WIDTH, NSA_WIDTH, 2 * NSA_KV, 2 * NSA_KV, 2 * NSA_KV, LANES)
    gq = jnp.tile(qk_g[0], NSA_HEADS).reshape(1, NSA_WIDTH)
    gks = jnp.tile(qk_g[2], NSA_KV_HEADS).reshape(1, NSA_KV)
    gkw = jnp.tile(qk_g[3], NSA_KV_HEADS).reshape(1, NSA_KV)
    avg = _seg_avg_matrix(NSA_WIDTH, 1.0 / HEAD_DIM)
    return pl.pallas_call(
        _in_proj_body,
        grid=(n // tm,),
        in_specs=[row(d), _const_spec((1, d)), _const_spec((d, P_TOTAL)), _const_spec((1, NSA_WIDTH)),
                  _const_spec((1, NSA_KV)), _const_spec((1, NSA_KV)), _const_spec((NSA_WIDTH, NSA_WIDTH))],
        out_specs=[row(w) for w in widths],
        out_shape=[jax.ShapeDtypeStruct((n, w), F32) for w in widths],
        compiler_params=pltpu.CompilerParams(dimension_semantics=("parallel",),
                                             vmem_limit_bytes=VMEM_LIMIT),
        name="in_proj",
    )(x, g.reshape(1, d), w_perm, gq, gks, gkw, avg)


def _rwkv_body(p_ref, shift_ref, s0_ref, mu_ref, w0_ref, lw_ref, a0_ref, g2_ref, kks_ref, ka_ref,
               rk_ref, gng_ref, gnb_ref, tri_ref, seg_ref, o_ref, st_ref, carry_ref, s_ref, *, chunk):
    C = chunk
    N = HEAD_DIM

    @pl.when(pl.program_id(1) == 0)
    def _():
        carry_ref[...] = shift_ref[...]
        s_ref[...] = s0_ref[...]

    p = p_ref[...]
    rows = lax.broadcasted_iota(jnp.int32, (C, 1), 0)
    prev = jnp.where(rows == 0, carry_ref[...], pltpu.roll(p, 1, axis=0))
    carry_ref[...] = p[C - 1:C, :]
    xm = p + (prev - p) * mu_ref[...]
    r = xm[:, 0:RW_WIDTH]
    k = xm[:, RW_WIDTH:2 * RW_WIDTH]
    v = xm[:, 2 * RW_WIDTH:3 * RW_WIDTH]
    lin = xm[:, 3 * RW_WIDTH:3 * RW_WIDTH + RW_W_LORA + RW_A_LORA]
    lane = lax.broadcasted_iota(jnp.int32, lin.shape, 1)
    lora = _dot_hi(jnp.where(lane < RW_W_LORA, jnp.tanh(lin), lin), lw_ref[...])
    w = -_softplus(-(w0_ref[...] + lora[:, :RW_WIDTH])) - 0.5
    a = _sigmoid(a0_ref[...] + lora[:, RW_WIDTH:])
    g = _dot(_sigmoid(xm[:, 3 * RW_WIDTH + RW_W_LORA + RW_A_LORA:]), g2_ref[...])
    kk = k * kks_ref[...]
    hi, lo = _split2(kk * kk)
    seg = seg_ref[...]
    ss = jnp.dot(hi, seg, preferred_element_type=F32) + jnp.dot(lo, seg, preferred_element_type=F32)
    kk = kk / jnp.maximum(jnp.sqrt(ss), 1e-12)
    k = k * (1.0 + (a - 1.0) * ka_ref[...])
    logdec = -jnp.exp(w)
    tri = tri_ref[...]
    G = _dot_sel(tri, logdec)
    g_end = G[C - 1:C, :]
    e_g = jnp.exp(G)
    e_gi = jnp.exp(-G)
    kkd = kk * jnp.exp(G - logdec)
    rd = r * e_g
    b = kk * a
    bi = b * e_gi
    ki = k * e_gi
    e_end = jnp.exp(g_end - G)
    bi2 = b * e_end
    ki2 = k * e_end
    dec_end = jnp.exp(g_end)

    ti = lax.broadcasted_iota(jnp.int32, (C, C), 0)
    si = lax.broadcasted_iota(jnp.int32, (C, C), 1)
    strict = si < ti
    incl = si <= ti
    eye = (si == ti).astype(F32)
    rkk = r * k * rk_ref[...]

    for h in range(RW_HEADS):
        sl = slice(h * N, (h + 1) * N)
        lhs = jnp.concatenate([kkd[:, sl], rd[:, sl]], axis=0)
        rhs = jnp.concatenate([bi[:, sl], ki[:, sl]], axis=0)
        vh = v[:, sl]
        s0 = s_ref[h]
        m4 = _dot_nt(lhs, rhs)
        ks = _dot_nt(lhs, s0)
        nb = jnp.where(strict, -m4[:C, :C], 0.0)
        lk = jnp.where(strict, m4[:C, C:], 0.0)
        mb = jnp.where(incl, m4[C:, :C], 0.0)
        mk = jnp.where(incl, m4[C:, C:], 0.0)
        t_inv = eye + nb
        pw = nb
        span = 2
        while span < C:
            pw = _dot(pw, pw)
            t_inv = t_inv + _dot(t_inv, pw)
            span *= 2
        u = _dot(t_inv, -(ks[:C] + _dot(lk, vh)))
        uv = jnp.concatenate([u, vh], axis=0)
        y = ks[C:] + _dot(jnp.concatenate([mb, mk], axis=1), uv)
        s_new = s0 * dec_end[:, sl] + _dot_tn(uv, jnp.concatenate([bi2[:, sl], ki2[:, sl]], axis=0))
        s_ref[h] = s_new
        mean = jnp.mean(y, axis=-1, keepdims=True)
        var = jnp.mean(jnp.square(y - mean), axis=-1, keepdims=True)
        yn = (y - mean) * lax.rsqrt(var + RW_GN_EPS) * gng_ref[:, sl] + gnb_ref[:, sl]
        bonus = jnp.sum(rkk[:, sl], axis=-1, keepdims=True) * vh
        o_ref[:, sl] = (yn + bonus) * g[:, sl]

    @pl.when(pl.program_id(1) == pl.num_programs(1) - 1)
    def _():
        st_ref[...] = s_ref[...]


def _rwkv(p_rw, shift_prev, s0, W):
    B, T, _ = p_rw.shape
    C = ML_CHUNK if T % ML_CHUNK == 0 else T
    row = lambda v: v.reshape(1, -1).astype(F32)
    z = jnp.zeros((RW_W_LORA, RW_WIDTH), F32)
    lw = jnp.concatenate([jnp.concatenate([W['rw_w2'], z], axis=1),
                          jnp.concatenate([z, W['rw_a2']], axis=1)], axis=0)
    tri = (jnp.arange(C)[:, None] >= jnp.arange(C)[None, :]).astype(BF16)
    seg = _seg_avg_matrix(RW_WIDTH, 1.0)
    consts = [row(W['rw_mu']), row(W['rw_w0']), lw, row(W['rw_a0']), W['rw_g2'].astype(BF16), row(W['rw_kk']),
              row(W['rw_ka']), row(W['rw_rk']), row(W['rw_gn_g']), row(W['rw_gn_b']), tri, seg]
    out, s_t = pl.pallas_call(
        functools.partial(_rwkv_body, chunk=C),
        grid=(B, T // C),
        in_specs=[pl.BlockSpec((None, C, RW_COLS), lambda b, c: (b, c, 0)),
                  pl.BlockSpec((None, 1, RW_COLS), lambda b, c: (b, 0, 0)),
                  pl.BlockSpec((None, RW_HEADS, HEAD_DIM, HEAD_DIM), lambda b, c: (b, 0, 0, 0))]
                 + [_const_spec(c.shape) for c in consts],
        out_specs=[pl.BlockSpec((None, C, RW_WIDTH), lambda b, c: (b, c, 0)),
                   pl.BlockSpec((None, RW_HEADS, HEAD_DIM, HEAD_DIM), lambda b, c: (b, 0, 0, 0))],
        out_shape=[jax.ShapeDtypeStruct((B, T, RW_WIDTH), F32),
                   jax.ShapeDtypeStruct((B, RW_HEADS, HEAD_DIM, HEAD_DIM), F32)],
        scratch_shapes=[pltpu.VMEM((1, RW_COLS), F32), pltpu.VMEM((RW_HEADS, HEAD_DIM, HEAD_DIM), F32)],
        compiler_params=pltpu.CompilerParams(dimension_semantics=("parallel", "arbitrary"),
                                             vmem_limit_bytes=VMEM_LIMIT),
        name="rwkv7",
    )(p_rw, shift_prev, s0, *consts)
    return out, s_t


def _mlstm_body(p_ref, gc_ref, gt_ref, convp_ref, c0_ref, n0_ref, m0_ref, cw_ref, cb_ref, wq_ref, wk_ref,
                bias_r_ref, bias_c_ref, ng_ref, skip_ref, tril_ref, triu_ref,
                o_ref, ct_ref, nt_ref, mt_ref, ext_ref, c_ref, n_ref, m_ref, *, chunk):
    L = chunk
    N = HEAD_DIM
    H = ML_HEADS

    @pl.when(pl.program_id(1) == 0)
    def _():
        ext_ref[0:8, :] = convp_ref[...]
        c_ref[...] = c0_ref[...]
        n_ref[...] = n0_ref[...]
        m_ref[...] = m0_ref[...]

    p = p_ref[...]
    x = p[:, 0:ML_WIDTH]
    v = p[:, ML_WIDTH:2 * ML_WIDTH]
    o_pre = p[:, 2 * ML_WIDTH:3 * ML_WIDTH]
    ext_ref[8:8 + L, :] = x
    cw = cw_ref[...]
    conv = (cb_ref[...] + cw[3:4, :] * x + cw[2:3, :] * ext_ref[7:7 + L, :]
            + cw[1:2, :] * ext_ref[6:6 + L, :] + cw[0:1, :] * ext_ref[5:5 + L, :])
    tail = ext_ref[L:L + 8, :]
    ext_ref[0:8, :] = tail
    ca = conv * _sigmoid(conv)
    q = _dot(ca, wq_ref[...])
    k = _dot(ca, wk_ref[...]) * (N ** -0.5)

    gcb = gc_ref[...] + bias_r_ref[...]
    lf_c = -_softplus(-gcb)
    bcum_c = _dot_sel(tril_ref[...], lf_c)
    gtb = gt_ref[...] + bias_c_ref[...][:, 0:1]
    lf_r = -_softplus(-gtb)
    bcum_r = _dot_x_sel(lf_r, triu_ref[...])

    ti = lax.broadcasted_iota(jnp.int32, (L, L), 0)
    si = lax.broadcasted_iota(jnp.int32, (L, L), 1)
    causal = si <= ti

    for h in range(H):
        sl = slice(h * N, (h + 1) * N)
        qh, kh, vh = q[:, sl], k[:, sl], v[:, sl]
        b_c = bcum_c[:, H + h:H + h + 1]
        i_c = gcb[:, h:h + 1]
        b_r = bcum_r[H + h:H + h + 1, :]
        i_r = gtb[h:h + 1, :]
        m_prev = m_ref[h:h + 1, 0:1]
        dmat = jnp.where(causal, b_c - b_r + i_r, -jnp.inf)
        inter = b_c + m_prev
        m_t = jnp.maximum(inter, jnp.max(dmat, axis=-1, keepdims=True))
        a = _dot_nt(qh, kh) * jnp.exp(dmat - m_t)
        sc = jnp.exp(inter - m_t)
        cs = c_ref[h]
        ns = n_ref[h:h + 1, :]
        num = sc * _dot_nt(qh, cs) + _dot(a, vh)
        den = sc * jnp.sum(qh * ns, axis=-1, keepdims=True) + jnp.sum(a, axis=-1, keepdims=True)
        hh = num / jnp.maximum(jnp.abs(den), jnp.exp(-m_t))
        b_end = b_c[L - 1:L, :]
        gs_r = b_end - b_r + i_r
        m_new = jnp.maximum(b_end + m_prev, jnp.max(gs_r, axis=-1, keepdims=True))
        dec = jnp.exp(b_end + m_prev - m_new)
        wg = jnp.exp(b_end - b_c + i_c - m_new)
        c_ref[h] = dec * cs + _dot_tn(wg * vh, kh)
        n_ref[h:h + 1, :] = dec * ns + jnp.sum(wg * kh, axis=0, keepdims=True)
        m_ref[h:h + 1, :] = jnp.broadcast_to(m_new, (1, LANES))
        hn = hh * lax.rsqrt(jnp.mean(hh * hh, axis=-1, keepdims=True) + NORM_EPS)
        o_ref[:, sl] = (hn * ng_ref[:, sl] + skip_ref[:, sl] * ca[:, sl]) * _sigmoid(o_pre[:, sl])

    @pl.when(pl.program_id(1) == pl.num_programs(1) - 1)
    def _():
        ct_ref[...] = c_ref[...]
        nt_ref[...] = n_ref[...]
        mt_ref[...] = m_ref[...]


def _block_diag(w):
    H, N, _ = w.shape
    eye = jnp.eye(H, dtype=w.dtype)
    return (eye[:, None, :, None] * w[:, :, None, :]).reshape(H * N, H * N)


def _mlstm(p_ml, gates, conv_prev, c0, n0, m0, W):
    B, T, _ = p_ml.shape
    H, N = ML_HEADS, HEAD_DIM
    L = ML_CHUNK if T % ML_CHUNK == 0 else T
    nc = T // L
    gt = jnp.swapaxes(gates[:, :, :2 * H].reshape(B, nc, L, 2 * H), 2, 3)
    convp = jnp.concatenate([jnp.zeros((B, 8 - (ML_CONV - 1), ML_WIDTH), F32), conv_prev], axis=1)
    m0b = jnp.broadcast_to(jnp.pad(m0, ((0, 0), (0, 8 - H)))[:, :, None], (B, 8, LANES))
    bias = W['ml_gate_b'].reshape(2 * H)
    bias_r = jnp.pad(bias, (0, LANES - 2 * H)).reshape(1, LANES)
    bias_c = jnp.broadcast_to(bias[:, None], (2 * H, LANES))
    tril = (jnp.arange(L)[:, None] >= jnp.arange(L)[None, :]).astype(BF16)
    row = lambda z: z.reshape(1, -1).astype(F32)
    consts = [W['ml_conv_w'], row(W['ml_conv_b']), _block_diag(W['ml_wq']).astype(BF16),
              _block_diag(W['ml_wk']).astype(BF16), bias_r, bias_c, row(W['ml_norm_g']), row(W['ml_skip']),
              tril, tril.T]
    per_b = lambda *shape: pl.BlockSpec((None,) + shape, lambda b, c: (b,) + (0,) * len(shape))
    out, c_t, n_t, m_t = pl.pallas_call(
        functools.partial(_mlstm_body, chunk=L),
        grid=(B, nc),
        in_specs=[pl.BlockSpec((None, L, 3 * ML_WIDTH), lambda b, c: (b, c, 0)),
                  pl.BlockSpec((None, L, LANES), lambda b, c: (b, c, 0)),
                  pl.BlockSpec((None, None, 2 * H, L), lambda b, c: (b, c, 0, 0)),
                  per_b(8, ML_WIDTH), per_b(H, N, N), per_b(H, N), per_b(8, LANES)]
                 + [_const_spec(c.shape) for c in consts],
        out_specs=[pl.BlockSpec((None, L, ML_WIDTH), lambda b, c: (b, c, 0)),
                   per_b(H, N, N), per_b(H, N), per_b(8, LANES)],
        out_shape=[jax.ShapeDtypeStruct((B, T, ML_WIDTH), F32), jax.ShapeDtypeStruct((B, H, N, N), F32),
                   jax.ShapeDtypeStruct((B, H, N), F32), jax.ShapeDtypeStruct((B, 8, LANES), F32)],
        scratch_shapes=[pltpu.VMEM((L + 8, ML_WIDTH), F32), pltpu.VMEM((H, N, N), F32),
                        pltpu.VMEM((H, N), F32), pltpu.VMEM((8, LANES), F32)],
        compiler_params=pltpu.CompilerParams(dimension_semantics=("parallel", "arbitrary"),
                                             vmem_limit_bytes=VMEM_LIMIT),
        name="mlstm",
    )(p_ml, gates, gt, convp, c0, n0, m0b, *consts)
    return out, c_t, n_t, m_t[:, :H, 0]


def _compress_tail(c, wc_ref, gk_ref, avg_ref, o_ref):
    c = _dot_hi(c, wc_ref[...])
    ck = c[:, :NSA_KV]
    ck = ck * lax.rsqrt(_seg_mean_sq(ck, avg_ref[...]) + NORM_EPS) * gk_ref[...]
    o_ref[:, :NSA_KV] = ck
    o_ref[:, NSA_KV:] = c[:, NSA_KV:]


def _weighted_block_sums(x, wpos):
    nb = x.shape[0] // NSA_BLOCK
    return jnp.sum(x.reshape(nb, NSA_BLOCK, x.shape[1]) * wpos[None], axis=1)


def _compress_dense_body(x_ref, wpos_ref, wc_ref, gk_ref, avg_ref, o_ref):
    _compress_tail(_weighted_block_sums(x_ref[...], wpos_ref[...]), wc_ref, gk_ref, avg_ref, o_ref)


def _compress_paged_body(pt_ref, *refs, pages):
    page_refs = refs[:pages]
    wpos_ref, wc_ref, gk_ref, avg_ref, o_ref = refs[pages:]
    wpos = wpos_ref[...]
    c = jnp.concatenate([_weighted_block_sums(r[...], wpos) for r in page_refs], axis=0)
    _compress_tail(c, wc_ref, gk_ref, avg_ref, o_ref)


def _compress_consts(W):
    wpos = W['nsa_wpos'].reshape(NSA_BLOCK, 2 * NSA_KV)
    wc = _block_diag(W['nsa_wc'].reshape(2 * NSA_KV_HEADS, HEAD_DIM, HEAD_DIM))
    gk = jnp.tile(W['nsa_qk_g'][1], NSA_KV_HEADS).reshape(1, NSA_KV)
    return [wpos, wc, gk, _seg_avg_matrix(NSA_KV, 1.0 / HEAD_DIM)]


def _nsa_compress_dense(kv_c, W):
    B, T, w = kv_c.shape
    nc = T // NSA_BLOCK
    nb = nc if nc <= 16 else 16
    consts = _compress_consts(W)
    return pl.pallas_call(
        _compress_dense_body,
        grid=(B, nc // nb),
        in_specs=[pl.BlockSpec((None, nb * NSA_BLOCK, w), lambda b, i: (b, i, 0))]
                 + [_const_spec(c.shape) for c in consts],
        out_specs=pl.BlockSpec((None, nb, w), lambda b, i: (b, i, 0)),
        out_shape=jax.ShapeDtypeStruct((B, nc, w), F32),
        compiler_params=pltpu.CompilerParams(dimension_semantics=("parallel", "parallel")),
        name="nsa_compress_dense",
    )(kv_c, *consts)


def _nsa_compress_paged(pool, page_table, W):
    B, n_pages = page_table.shape
    w = pool.shape[-1]
    pages = min(16, n_pages)
    bpp = PAGE_SIZE // NSA_BLOCK
    consts = _compress_consts(W)

    def page_spec(k):
        return pl.BlockSpec((None, PAGE_SIZE, w), lambda b, i, pt: (pt[b * n_pages + i * pages + k], 0, 0))

    const = lambda c: pl.BlockSpec(c.shape, lambda b, i, pt: (0,) * c.ndim)
    return pl.pallas_call(
        functools.partial(_compress_paged_body, pages=pages),
        grid_spec=pltpu.PrefetchScalarGridSpec(
            num_scalar_prefetch=1,
            grid=(B, n_pages // pages),
            in_specs=[page_spec(k) for k in range(pages)] + [const(c) for c in consts],
            out_specs=pl.BlockSpec((None, pages * bpp, w), lambda b, i, pt: (b, i, 0)),
        ),
        out_shape=jax.ShapeDtypeStruct((B, n_pages * bpp, w), F32),
        compiler_params=pltpu.CompilerParams(dimension_semantics=("parallel", "parallel"),
                                             vmem_limit_bytes=VMEM_LIMIT),
        name="nsa_compress_paged",
    )(page_table.reshape(-1), *([pool] * pages), *consts)


def _stack_heads(qg):
    return jnp.concatenate([qg[:, r * HEAD_DIM:(r + 1) * HEAD_DIM] for r in range(NSA_GROUP)],
                           axis=0) * (HEAD_DIM ** -0.5)


def _tile_rows(x, n):
    return jnp.concatenate([x] * n, axis=0)


def _masked_softmax(s, mask):
    s = jnp.where(mask, s, -jnp.inf)
    m = jnp.max(s, axis=-1, keepdims=True)
    m = jnp.where(m == -jnp.inf, 0.0, m)
    e = jnp.exp(s - m)
    return e / jnp.maximum(jnp.sum(e, axis=-1, keepdims=True), 1e-30)


def _cmp_attend(qs, ck, cv, qpos_s, t):
    nc = ck.shape[0]
    s = _dot_nt_hi(qs, ck)
    blk = lax.broadcasted_iota(jnp.int32, (1, nc), 1)
    p = _masked_softmax(s, (blk + 1) * NSA_BLOCK - 1 <= qpos_s)
    imp = p[0:t]
    for r in range(1, NSA_GROUP):
        imp = imp + p[r * t:(r + 1) * t]
    return _dot(p, cv), imp


def _select_blocks(imp, qpos, n_blocks, k):
    t, w = imp.shape
    jf = lax.broadcasted_iota(jnp.int32, (t, w), 1)
    cur = jnp.right_shift(qpos, BLOCK_SHIFT)
    forced = (jf == 0) | (jf == cur) | (jf == cur - 1)
    score = jnp.where(forced, BIG, jnp.where(jf <= cur, imp, -BIG))
    score = jnp.where(jf < n_blocks, score, -jnp.inf)
    lane = jf.astype(F32)
    sel = jnp.zeros((t, w), F32)
    idx = []
    for _ in range(k):
        m = jnp.max(score, axis=-1, keepdims=True)
        i = jnp.min(jnp.where(score == m, lane, float(w)), axis=-1, keepdims=True)
        hit = lane == i
        sel = jnp.where(hit, 1.0, sel)
        score = jnp.where(hit, -jnp.inf, score)
        idx.append(i)
    return idx, sel


def _online_step(carry, s, ok, v):
    m, l, acc = carry
    s = jnp.where(ok, s, -jnp.inf)
    m_new = jnp.maximum(m, jnp.max(s, axis=-1, keepdims=True))
    m_safe = jnp.where(m_new == -jnp.inf, 0.0, m_new)
    p = jnp.exp(s - m_safe)
    alpha = jnp.exp(m - m_safe)
    return m_new, alpha * l + jnp.sum(p, axis=-1, keepdims=True), alpha * acc + _dot(p, v)


def _online_init(rows):
    return (jnp.full((rows, 1), -jnp.inf, F32), jnp.zeros((rows, 1), F32), jnp.zeros((rows, HEAD_DIM), F32))


def _online_finish(carry):
    _, l, acc = carry
    return acc / jnp.maximum(l, 1e-30)


def _gate_mix(gt, g, r, o_c, o_s, o_w):
    c0 = 2 * ML_HEADS + (g * NSA_GROUP + r) * 3
    return gt[:, c0:c0 + 1] * o_c + gt[:, c0 + 1:c0 + 2] * o_s + gt[:, c0 + 2:c0 + 3] * o_w


def _nsa_prompt_body(q_ref, gates_ref, gb_ref, cmp_ref, kvs_ref, kvw_ref, o_ref, *, tq, tk, n_blocks):
    R, N = NSA_GROUP, HEAD_DIM
    i = pl.program_id(1)
    t0 = i * tq
    qpos = t0 + lax.broadcasted_iota(jnp.int32, (tq, 1), 0)
    qpos_s = _tile_rows(qpos, R)
    gt = _sigmoid(gates_ref[...] + gb_ref[...])
    q = q_ref[...]
    cmp = cmp_ref[...]
    nc = cmp.shape[0]
    for g in range(NSA_KV_HEADS):
        ko = g * N
        vo = NSA_KV + g * N
        qs = _stack_heads(q[:, g * R * N:(g + 1) * R * N])
        o_c, imp = _cmp_attend(qs, cmp[:, ko:ko + N], cmp[:, vo:vo + N], qpos_s, tq)
        _, sel = _select_blocks(imp, qpos, n_blocks, min(NSA_TOPN, n_blocks))
        sel = sel.astype(BF16)

        def sel_step(kt, carry):
            k0 = pl.multiple_of(kt * tk, tk)
            kv = kvs_ref[pl.ds(k0, tk), :]
            s = _dot_nt(qs, kv[:, ko:ko + N])
            kpos = k0 + lax.broadcasted_iota(jnp.int32, (1, tk), 1)
            expand = (lax.broadcasted_iota(jnp.int32, (nc, tk), 0)
                      == jnp.right_shift(k0 + lax.broadcasted_iota(jnp.int32, (nc, tk), 1),
                                         BLOCK_SHIFT)).astype(BF16)
            chosen = jnp.dot(sel, expand, preferred_element_type=F32) > 0.5
            ok = _tile_rows(chosen & (kpos <= qpos), R)
            return _online_step(carry, s, ok, kv[:, vo:vo + N])

        n_kt = (t0 + tq + tk - 1) // tk
        o_s = _online_finish(lax.fori_loop(0, n_kt, sel_step, _online_init(R * tq)))

        def win_step(kt, carry):
            k0 = pl.multiple_of(kt * tq, tq)
            kv = kvw_ref[pl.ds(k0, tq), :]
            s = _dot_nt(qs, kv[:, ko:ko + N])
            d = qpos - (k0 + lax.broadcasted_iota(jnp.int32, (1, tq), 1))
            ok = _tile_rows((d >= 0) & (d < NSA_WINDOW), R)
            return _online_step(carry, s, ok, kv[:, vo:vo + N])

        lo = jnp.maximum(i - NSA_WINDOW // tq, 0)
        o_w = _online_finish(lax.fori_loop(lo, i + 1, win_step, _online_init(R * tq)))

        for r in range(R):
            rows = slice(r * tq, (r + 1) * tq)
            c0 = (g * R + r) * N
            o_ref[:, c0:c0 + N] = _gate_mix(gt, g, r, o_c[rows], o_s[rows], o_w[rows])


def _gate_bias_row(gate_b):
    return jnp.pad(gate_b, (2 * ML_HEADS, LANES - 2 * ML_HEADS - 3 * NSA_HEADS)).reshape(1, LANES)


def _nsa_prompt(q, gates, cmp_kv, kv_s, kv_w, W):
    B, T, _ = q.shape
    tq = min(128, T)
    tk = min(512, T)
    per_b = lambda a: pl.BlockSpec((None,) + a.shape[1:], lambda b, i: (b,) + (0,) * (a.ndim - 1))
    tile = lambda w: pl.BlockSpec((None, tq, w), lambda b, i: (b, i, 0))
    gb = _gate_bias_row(W['nsa_gate_b'])
    return pl.pallas_call(
        functools.partial(_nsa_prompt_body, tq=tq, tk=tk, n_blocks=T // NSA_BLOCK),
        grid=(B, T // tq),
        in_specs=[tile(NSA_WIDTH), tile(LANES), pl.BlockSpec((1, LANES), lambda b, i: (0, 0)),
                  per_b(cmp_kv), per_b(kv_s), per_b(kv_w)],
        out_specs=tile(NSA_WIDTH),
        out_shape=jax.ShapeDtypeStruct((B, T, NSA_WIDTH), F32),
        compiler_params=pltpu.CompilerParams(dimension_semantics=("parallel", "arbitrary"),
                                             vmem_limit_bytes=VMEM_LIMIT),
        name="nsa_prompt",
    )(q, gates, gb, cmp_kv, kv_s, kv_w)


def _nsa_sample_cmp_body(q_ref, cmp_ref, kvw_ref, winp_ref, oc_ref, ow_ref, idx_ref, *, pos0, n_blocks):
    R, N = NSA_GROUP, HEAD_DIM
    t = q_ref.shape[0]
    wb = winp_ref.shape[0]
    qpos = pos0 + lax.broadcasted_iota(jnp.int32, (t, 1), 0)
    qpos_s = _tile_rows(qpos, R)
    q = q_ref[...]
    cmp = cmp_ref[...]
    nc = cmp.shape[0]
    w_sel = -(-n_blocks // LANES) * LANES
    for g in range(NSA_KV_HEADS):
        ko = g * N
        vo = NSA_KV + g * N
        qs = _stack_heads(q[:, g * R * N:(g + 1) * R * N])
        o_c, imp = _cmp_attend(qs, cmp[:, ko:ko + N], cmp[:, vo:vo + N], qpos_s, t)
        if w_sel > nc:
            imp = jnp.concatenate([imp, jnp.zeros((t, w_sel - nc), F32)], axis=1)
        idx, _ = _select_blocks(imp, qpos, n_blocks, min(NSA_TOPN, n_blocks))
        lane = lax.broadcasted_iota(jnp.int32, (t, LANES), 1)
        tile = jnp.zeros((t, LANES), F32)
        for kk, col in enumerate(idx):
            tile = jnp.where(lane == kk, col, tile)
        idx_ref[g] = tile.astype(jnp.int32)

        carry = _online_init(R * t)
        prev = winp_ref[...]
        d = qpos_s - (pos0 - wb + lax.broadcasted_iota(jnp.int32, (1, wb), 1))
        carry = _online_step(carry, _dot_nt(qs, prev[:, ko:ko + N]), (d >= 0) & (d < NSA_WINDOW),
                             prev[:, vo:vo + N])
        new = kvw_ref[...]
        d = qpos_s - (pos0 + lax.broadcasted_iota(jnp.int32, (1, t), 1))
        carry = _online_step(carry, _dot_nt(qs, new[:, ko:ko + N]), (d >= 0) & (d < NSA_WINDOW),
                             new[:, vo:vo + N])
        o_w = _online_finish(carry)
        for r in range(R):
            rows = slice(r * t, (r + 1) * t)
            c0 = (g * R + r) * N
            oc_ref[:, c0:c0 + N] = o_c[rows]
            ow_ref[:, c0:c0 + N] = o_w[rows]


def _nsa_sample_cmp(q, cmp_kv, kv_w, win_prev, pos0, n_blocks):
    B, T, _ = q.shape
    per_b = lambda a: pl.BlockSpec((None,) + a.shape[1:], lambda b: (b,) + (0,) * (a.ndim - 1))
    out_b = lambda *s: pl.BlockSpec((None,) + s, lambda b: (b,) + (0,) * len(s))
    return pl.pallas_call(
        functools.partial(_nsa_sample_cmp_body, pos0=pos0, n_blocks=n_blocks),
        grid=(B,),
        in_specs=[per_b(q), per_b(cmp_kv), per_b(kv_w), per_b(win_prev)],
        out_specs=[out_b(T, NSA_WIDTH), out_b(T, NSA_WIDTH), out_b(NSA_KV_HEADS, T, LANES)],
        out_shape=[jax.ShapeDtypeStruct((B, T, NSA_WIDTH), F32), jax.ShapeDtypeStruct((B, T, NSA_WIDTH), F32),
                   jax.ShapeDtypeStruct((B, NSA_KV_HEADS, T, LANES), jnp.int32)],
        compiler_params=pltpu.CompilerParams(dimension_semantics=("parallel",)),
        name="nsa_sample_cmp",
    )(q, cmp_kv, kv_w, win_prev)


def _nsa_sample_sel_body(idx_ref, pt_ref, *refs, n_past, pos0, topk):
    R, N, G = NSA_GROUP, HEAD_DIM, NSA_KV_HEADS
    blk_refs = refs[:G * topk]
    q_ref, new_ref, oc_ref, ow_ref, gates_ref, gb_ref, o_ref = refs[G * topk:]
    b = pl.program_id(0)
    t = pl.program_id(1)
    n_t = pl.num_programs(1)
    qpos = pos0 + t
    gt = _sigmoid(gates_ref[pl.ds(t, 1), :] + gb_ref[...])
    q = q_ref[pl.ds(t, 1), :]
    o_c = oc_ref[pl.ds(t, 1), :]
    o_w = ow_ref[pl.ds(t, 1), :]
    new = new_ref[...]
    heads = []
    for g in range(G):
        ko = g * N
        vo = NSA_KV + g * N
        qs = _stack_heads(q[:, g * R * N:(g + 1) * R * N])
        scores, vals = [], []
        for k in range(topk):
            j = idx_ref[((b * n_t + t) * G + g) * topk + k]
            blk = jnp.where(j >= n_past, new, blk_refs[g * topk + k][...])
            kpos = j * NSA_BLOCK + lax.broadcasted_iota(jnp.int32, (1, NSA_BLOCK), 1)
            scores.append(jnp.where(kpos <= qpos, _dot_nt(qs, blk[:, ko:ko + N]), -jnp.inf))
            vals.append(blk[:, vo:vo + N])
        m = scores[0].max(axis=-1, keepdims=True)
        for s in scores[1:]:
            m = jnp.maximum(m, s.max(axis=-1, keepdims=True))
        m = jnp.where(m == -jnp.inf, 0.0, m)
        l = jnp.zeros((R, 1), F32)
        acc = jnp.zeros((R, N), F32)
        for s, v in zip(scores, vals):
            e = jnp.exp(s - m)
            l = l + jnp.sum(e, axis=-1, keepdims=True)
            acc = acc + _dot(e, v)
        o_s = acc / jnp.maximum(l, 1e-30)
        for r in range(R):
            c0 = (g * R + r) * N
            heads.append(_gate_mix(gt, g, r, o_c[:, c0:c0 + N], o_s[r:r + 1], o_w[:, c0:c0 + N]))
    o_ref[pl.ds(t, 1), :] = jnp.concatenate(heads, axis=1)


def _nsa_sample_sel(q, idx, page_table, pool_sel, kv_s_new, o_c, o_w, gates, W, pos0):
    B, T, _ = q.shape
    G = NSA_KV_HEADS
    topk = idx.shape[-1]
    n_pages = page_table.shape[1]
    bpp = PAGE_SIZE // NSA_BLOCK
    n_past = pos0 // NSA_BLOCK
    w = pool_sel.shape[-1]
    pool_blk = pool_sel.reshape(-1, NSA_BLOCK, w)
    new_blk = jnp.pad(kv_s_new, ((0, 0), (0, NSA_BLOCK - T), (0, 0)))

    def blk_spec(g, k):
        def index(b, t, idx_ref, pt_ref):
            j = jnp.minimum(idx_ref[((b * T + t) * G + g) * topk + k], n_past - 1)
            return (pt_ref[b * n_pages + j // bpp] * bpp + j % bpp, 0, 0)
        return pl.BlockSpec((None, NSA_BLOCK, w), index)

    per_b = lambda a: pl.BlockSpec((None,) + a.shape[1:], lambda b, t, i, p: (b,) + (0,) * (a.ndim - 1))
    gb = _gate_bias_row(W['nsa_gate_b'])
    return pl.pallas_call(
        functools.partial(_nsa_sample_sel_body, n_past=n_past, pos0=pos0, topk=topk),
        grid_spec=pltpu.PrefetchScalarGridSpec(
            num_scalar_prefetch=2,
            grid=(B, T),
            in_specs=[blk_spec(g, k) for g in range(G) for k in range(topk)]
                     + [per_b(q), per_b(new_blk), per_b(o_c), per_b(o_w), per_b(gates),
                        pl.BlockSpec((1, LANES), lambda b, t, i, p: (0, 0))],
            out_specs=pl.BlockSpec((None, T, NSA_WIDTH), lambda b, t, i, p: (b, 0, 0)),
        ),
        out_shape=jax.ShapeDtypeStruct((B, T, NSA_WIDTH), F32),
        compiler_params=pltpu.CompilerParams(dimension_semantics=("parallel", "arbitrary")),
        name="nsa_sample_sel",
    )(idx.reshape(-1), page_table.reshape(-1), *([pool_blk] * (G * topk)), q, new_blk, o_c, o_w, gates, gb)


def _nsa_prompt_mixer(q, gates, kv_c, kv_s, kv_w, W):
    return _nsa_prompt(q, gates, _nsa_compress_dense(kv_c, W), kv_s, kv_w, W)


def _nsa_sample_mixer(q, gates, kv_s, kv_w, page_table, pool_cmp, pool_sel, win_prev, W):
    B, T, _ = q.shape
    w = 2 * NSA_KV
    pos0 = page_table.shape[1] * PAGE_SIZE
    n_blocks = -(-(pos0 + T) // NSA_BLOCK)
    cmp_kv = _nsa_compress_paged(pool_cmp.reshape(-1, PAGE_SIZE, w), page_table, W)
    o_c, o_w, idx = _nsa_sample_cmp(q, cmp_kv, kv_w, win_prev.reshape(B, -1, w), pos0, n_blocks)
    idx = jnp.swapaxes(idx[..., :min(NSA_TOPN, n_blocks)], 1, 2)
    return _nsa_sample_sel(q, idx, page_table, pool_sel.reshape(-1, PAGE_SIZE, w), kv_s, o_c, o_w, gates, W, pos0)


def _mem_kv_body(x_ref, g_ref, w_ref, gk_ref, avg_ref, o_ref):
    h = _rms_rows(x_ref[...], g_ref[...]).astype(BF16)
    kv = jnp.dot(h, w_ref[...], preferred_element_type=F32)
    k = kv[:, :MEM_WIDTH]
    o_ref[:, :MEM_WIDTH] = k * lax.rsqrt(_seg_mean_sq(k, avg_ref[...]) + NORM_EPS) * gk_ref[...]
    o_ref[:, MEM_WIDTH:] = kv[:, MEM_WIDTH:]


def _mem_kv(mem, g, w_kv, k_g):
    n, d = mem.shape
    tm = _row_tile(n, 512)
    consts = [g.reshape(1, d), w_kv.astype(BF16), jnp.tile(k_g, MEM_HEADS).reshape(1, MEM_WIDTH),
              _seg_avg_matrix(MEM_WIDTH, 1.0 / HEAD_DIM)]
    return pl.pallas_call(
        _mem_kv_body,
        grid=(n // tm,),
        in_specs=[pl.BlockSpec((tm, d), lambda i: (i, 0))] + [_const_spec(c.shape) for c in consts],
        out_specs=pl.BlockSpec((tm, 2 * MEM_WIDTH), lambda i: (i, 0)),
        out_shape=jax.ShapeDtypeStruct((n, 2 * MEM_WIDTH), F32),
        compiler_params=pltpu.CompilerParams(dimension_semantics=("parallel",)),
        name="mem_kv",
    )(mem, *consts)


def _out_mem_body(x_ref, orw_ref, oml_ref, onsa_ref, kv_ref, w1_ref, w2_ref, w3_ref, g_ref, wq_ref, gq_ref,
                  avg_ref, wo_ref, o_ref):
    N = HEAD_DIM
    x = (x_ref[...] + _dot(orw_ref[...], w1_ref[...]) + _dot(oml_ref[...], w2_ref[...])
         + _dot(onsa_ref[...], w3_ref[...]))
    h = _rms_rows(x, g_ref[...]).astype(BF16)
    q = jnp.dot(h, wq_ref[...], preferred_element_type=F32)
    q = q * lax.rsqrt(_seg_mean_sq(q, avg_ref[...]) + NORM_EPS) * gq_ref[...] * (N ** -0.5)
    kv = kv_ref[...]
    heads = []
    for hd in range(MEM_HEADS):
        s = _dot_nt(q[:, hd * N:(hd + 1) * N], kv[:, hd * N:(hd + 1) * N])
        e = jnp.exp(s - jnp.max(s, axis=-1, keepdims=True))
        p = e / jnp.sum(e, axis=-1, keepdims=True)
        heads.append(_dot(p, kv[:, MEM_WIDTH + hd * N:MEM_WIDTH + (hd + 1) * N]))
    o_ref[...] = x + _dot(jnp.concatenate(heads, axis=1), wo_ref[...])


def _out_mem(x, o_rw, o_ml, o_nsa, kv, w_out, g, w_q, q_g, w_o):
    B, T, d = x.shape
    tm = _row_tile(T, 512)
    w1 = w_out[:RW_WIDTH].astype(BF16)
    w2 = w_out[RW_WIDTH:RW_WIDTH + ML_WIDTH].astype(BF16)
    w3 = w_out[RW_WIDTH + ML_WIDTH:].astype(BF16)
    consts = [w1, w2, w3, g.reshape(1, d), w_q.astype(BF16), jnp.tile(q_g, MEM_HEADS).reshape(1, MEM_WIDTH),
              _seg_avg_matrix(MEM_WIDTH, 1.0 / HEAD_DIM), w_o.astype(BF16)]
    tile = lambda w: pl.BlockSpec((None, tm, w), lambda b, i: (b, i, 0))
    return pl.pallas_call(
        _out_mem_body,
        grid=(B, T // tm),
        in_specs=[tile(d), tile(RW_WIDTH), tile(ML_WIDTH), tile(NSA_WIDTH),
                  pl.BlockSpec((None,) + kv.shape[1:], lambda b, i: (b, 0, 0))]
                 + [_const_spec(c.shape) for c in consts],
        out_specs=tile(d),
        out_shape=jax.ShapeDtypeStruct((B, T, d), F32),
        compiler_params=pltpu.CompilerParams(dimension_semantics=("parallel", "parallel"),
                                             vmem_limit_bytes=VMEM_LIMIT),
        name="out_mem",
    )(x, o_rw, o_ml, o_nsa, kv, *consts)


def _layer(x, W, st, page_table, mem):
    B, T, d = x.shape
    is_prompt = st is None
    g = W['norm_g']
    x1 = _ffn(x.reshape(B * T, d), g[0], W['ffa_up'], W['ffa_down'])
    p_rw, p_ml, q, kv_c, kv_s, kv_w, gates = [
        z.reshape(B, T, -1) for z in _in_proj(x1, g[1], W['w_in_perm'], W['nsa_qk_g'])]
    if is_prompt:
        zeros = lambda *s: jnp.zeros(s, F32)
        st = {'rw_shift': zeros(B, 1, RW_COLS), 'rw_S': zeros(B, RW_HEADS, HEAD_DIM, HEAD_DIM),
              'ml_conv': zeros(B, ML_CONV - 1, ML_WIDTH), 'ml_C': zeros(B, ML_HEADS, HEAD_DIM, HEAD_DIM),
              'ml_n': zeros(B, ML_HEADS, HEAD_DIM), 'ml_m': zeros(B, ML_HEADS)}
        m = mem.shape[1]
        mem_kv = _mem_kv(mem.reshape(B * m, d), g[3], W['mem_w_kv'], W['mem_qk_g'][1]).reshape(B, m, 2 * MEM_WIDTH)
    else:
        mem_kv = st['mem_kv'].reshape(B, -1, 2 * MEM_WIDTH)
    o_rw, rw_S = _rwkv(p_rw, st['rw_shift'], st['rw_S'], W)
    o_ml, ml_C, ml_n, ml_m = _mlstm(p_ml, gates, st['ml_conv'], st['ml_C'], st['ml_n'], st['ml_m'], W)
    if is_prompt:
        o_nsa = _nsa_prompt_mixer(q, gates, kv_c, kv_s, kv_w, W)
        win = kv_w[:, T - min(NSA_WINDOW, T):]
    else:
        o_nsa = _nsa_sample_mixer(q, gates, kv_s, kv_w, page_table, st['nsa_cmp'], st['nsa_sel'], st['nsa_win'], W)
        wb = st['nsa_win'].shape[1]
        kv_all = jnp.concatenate([st['nsa_win'].reshape(B, wb, 2 * NSA_KV), kv_w], axis=1)
        win = kv_all[:, wb + T - min(NSA_WINDOW, wb + T):]
    x2 = _out_mem(x1.reshape(B, T, d), o_rw, o_ml, o_nsa, mem_kv, W['w_out'], g[2], W['mem_w_q'],
                  W['mem_qk_g'][0], W['mem_w_o'])
    x3 = _ffn(x2.reshape(B * T, d), g[4], W['ffb_up'], W['ffb_down']).reshape(B, T, d)
    kv5 = lambda z: z.reshape(z.shape[0], z.shape[1], 2, NSA_KV_HEADS, HEAD_DIM)
    conv_all = jnp.concatenate([st['ml_conv'], p_ml[:, :, :ML_WIDTH]], axis=1) if T < ML_CONV - 1 else p_ml[:, :, :ML_WIDTH]
    new = {'nsa_cmp': kv5(kv_c), 'nsa_sel': kv5(kv_s), 'nsa_win': kv5(win),
           'rw_shift': p_rw[:, T - 1:], 'rw_S': rw_S, 'ml_conv': conv_all[:, conv_all.shape[1] - (ML_CONV - 1):],
           'ml_C': ml_C, 'ml_n': ml_n, 'ml_m': ml_m}
    if is_prompt:
        new['mem_kv'] = mem_kv.reshape(B, -1, 2, MEM_HEADS, HEAD_DIM)
    return x3, new


def kernel(x_prompt, x_sample, cache_nsa_cmp, cache_nsa_sel, cache_nsa_win, cache_mem_kv, state_rwkv_shift, state_rwkv_S, state_mlstm_conv, state_mlstm_C, state_mlstm_n, state_mlstm_m, page_table, mem_prompt, norm_g, ffa_up, ffa_down, ffb_up, ffb_down, w_in, w_out, rw_mu, rw_w0, rw_w2, rw_a0, rw_a2, rw_g2, rw_kk, rw_ka, rw_rk, rw_gn_g, rw_gn_b, ml_conv_w, ml_conv_b, ml_wq, ml_wk, ml_gate_b, ml_norm_g, ml_skip, nsa_qk_g, nsa_wpos, nsa_wc, nsa_gate_b, mem_w_q, mem_w_kv, mem_qk_g, mem_w_o):
    params = dict(norm_g=norm_g, ffa_up=ffa_up, ffa_down=ffa_down, ffb_up=ffb_up, ffb_down=ffb_down, w_out=w_out,
                  rw_mu=rw_mu, rw_w0=rw_w0, rw_w2=rw_w2, rw_a0=rw_a0, rw_a2=rw_a2, rw_g2=rw_g2, rw_kk=rw_kk,
                  rw_ka=rw_ka, rw_rk=rw_rk, rw_gn_g=rw_gn_g, rw_gn_b=rw_gn_b, ml_conv_w=ml_conv_w,
                  ml_conv_b=ml_conv_b, ml_wq=ml_wq, ml_wk=ml_wk, ml_gate_b=ml_gate_b, ml_norm_g=ml_norm_g,
                  ml_skip=ml_skip, nsa_qk_g=nsa_qk_g, nsa_wpos=nsa_wpos, nsa_wc=nsa_wc, nsa_gate_b=nsa_gate_b,
                  mem_w_q=mem_w_q, mem_w_kv=mem_w_kv, mem_qk_g=mem_qk_g, mem_w_o=mem_w_o)
    depth = norm_g.shape[0]
    y_p, y_s = x_prompt, x_sample
    new_p, new_s = [], []
    for l in range(depth):
        W = {name: v[l] for name, v in params.items()}
        W['w_in_perm'] = _permute_w_in(w_in[l])
        st = {'nsa_cmp': cache_nsa_cmp[l], 'nsa_sel': cache_nsa_sel[l], 'nsa_win': cache_nsa_win[l],
              'mem_kv': cache_mem_kv[l], 'rw_shift': state_rwkv_shift[l], 'rw_S': state_rwkv_S[l],
              'ml_conv': state_mlstm_conv[l], 'ml_C': state_mlstm_C[l], 'ml_n': state_mlstm_n[l],
              'ml_m': state_mlstm_m[l]}
        y_p, sp = _layer(y_p, W, None, None, mem_prompt)
        y_s, ss = _layer(y_s, W, st, page_table, None)
        new_p.append(sp)
        new_s.append(ss)
    P = lambda name: jnp.stack([d[name] for d in new_p])
    S = lambda name: jnp.stack([d[name] for d in new_s])
    return (y_p, y_s,
            P('nsa_cmp'), S('nsa_cmp'), P('nsa_sel'), S('nsa_sel'), P('nsa_win'), S('nsa_win'),
            P('mem_kv'),
            P('rw_shift'), S('rw_shift'), P('rw_S'), S('rw_S'),
            P('ml_conv'), S('ml_conv'), P('ml_C'), S('ml_C'), P('ml_n'), S('ml_n'), P('ml_m'), S('ml_m'))
```

```python
import functools

import jax
import jax.numpy as jnp
from jax import lax
from jax.experimental import pallas as pl
from jax.experimental.pallas import tpu as pltpu

F32 = jnp.float32
BF16 = jnp.bfloat16

HEAD_DIM = 64
RW_HEADS = 4
RW_WIDTH = RW_HEADS * HEAD_DIM
RW_W_LORA = 64
RW_A_LORA = 64
RW_G_LORA = 128
RW_COLS = 3 * RW_WIDTH + RW_W_LORA + RW_A_LORA + RW_G_LORA
RW_GN_EPS = 64e-5
ML_HEADS = 4
ML_WIDTH = ML_HEADS * HEAD_DIM
ML_CONV = 4
ML_CHUNK = 64
ML_COLS = 3 * ML_WIDTH + 2 * ML_HEADS
NSA_HEADS = 8
NSA_KV_HEADS = 2
NSA_GROUP = NSA_HEADS // NSA_KV_HEADS
NSA_WIDTH = NSA_HEADS * HEAD_DIM
NSA_KV = NSA_KV_HEADS * HEAD_DIM
NSA_BLOCK = 64
BLOCK_SHIFT = 6
NSA_TOPN = 8
NSA_WINDOW = 512
NSA_COLS = NSA_WIDTH + 6 * NSA_KV + 3 * NSA_HEADS
MEM_HEADS = 4
MEM_WIDTH = MEM_HEADS * HEAD_DIM
PAGE_SIZE = 128
NORM_EPS = 1e-6
BIG = 1e9
NEG = -1e30
LANES = 128
VMEM_LIMIT = 56 * 1024 * 1024

P_RW = 0
P_ML = P_RW + RW_COLS
P_Q = P_ML + 3 * ML_WIDTH
P_KVC = P_Q + NSA_WIDTH
P_KVS = P_KVC + 2 * NSA_KV
P_KVW = P_KVS + 2 * NSA_KV
P_GATES = P_KVW + 2 * NSA_KV
P_TOTAL = P_GATES + LANES


def _dot(a, b):
    return jnp.dot(a.astype(BF16), b.astype(BF16), preferred_element_type=F32)


def _dot_nt(a, b):
    return lax.dot_general(a.astype(BF16), b.astype(BF16), (((1,), (1,)), ((), ())),
                           preferred_element_type=F32)


def _dot_tn(a, b):
    return lax.dot_general(a.astype(BF16), b.astype(BF16), (((0,), (0,)), ((), ())),
                           preferred_element_type=F32)


def _split2(x):
    hi = x.astype(BF16)
    lo = (x - hi.astype(F32)).astype(BF16)
    return hi, lo


def _split3(x):
    hi = x.astype(BF16)
    r = x - hi.astype(F32)
    mid = r.astype(BF16)
    lo = (r - mid.astype(F32)).astype(BF16)
    return hi, mid, lo


def _dot_sel(sel, x):
    hi, mid, lo = _split3(x)
    s = sel.astype(BF16)
    d = lambda t: jnp.dot(s, t, preferred_element_type=F32)
    return d(hi) + d(mid) + d(lo)


def _dot_x_sel(x, sel):
    hi, mid, lo = _split3(x)
    s = sel.astype(BF16)
    d = lambda t: jnp.dot(t, s, preferred_element_type=F32)
    return d(hi) + d(mid) + d(lo)


def _dot_hi(a, b):
    ah, al = _split2(a)
    bh, bl = _split2(b)
    d = lambda u, v: jnp.dot(u, v, preferred_element_type=F32)
    return d(ah, bh) + d(al, bh) + d(ah, bl)


def _dot_nt_hi(a, b):
    ah, al = _split2(a)
    bh, bl = _split2(b)
    d = lambda u, v: lax.dot_general(u, v, (((1,), (1,)), ((), ())), preferred_element_type=F32)
    return d(ah, bh) + d(al, bh) + d(ah, bl)


def _seg_mean_sq(x, seg_avg):
    hi, lo = _split2(x * x)
    d = lambda t: jnp.dot(t, seg_avg, preferred_element_type=F32)
    return d(hi) + d(lo)


def _seg_mean_sq_rows(x, seg_avg):
    hi, lo = _split2(x * x)
    d = lambda t: jnp.dot(seg_avg, t, preferred_element_type=F32)
    return d(hi) + d(lo)


def _rms_rows(x, g):
    return x * lax.rsqrt(jnp.mean(x * x, axis=-1, keepdims=True) + NORM_EPS) * g


def _sigmoid(x):
    return 1.0 / (1.0 + jnp.exp(-x))


def _softplus(x):
    return jnp.maximum(x, 0.0) + jnp.log(1.0 + jnp.exp(-jnp.abs(x)))


def _row_tile(n, target):
    t = min(n, target)
    while n % t:
        t //= 2
    return t


def _const_spec(shape):
    nd = len(shape)
    return pl.BlockSpec(shape, lambda *_: (0,) * nd, pipeline_mode=pl.Buffered(1))


def _seg_avg_matrix(width, scale):
    i = jnp.arange(width) // HEAD_DIM
    return (jnp.where(i[:, None] == i[None, :], scale, 0.0)).astype(BF16)


def _channel_major(x):
    b, t = x.shape[:2]
    return jnp.transpose(x, (0, 2, 3, 4, 1)).reshape(b, -1, t)


def _token_major(x, c, g):
    b, w, t = x.shape
    return jnp.transpose(x.reshape(b, c, g, w // (c * g), t), (0, 4, 1, 2, 3))


def _ffn_body(x_ref, g_ref, wg_ref, wu_ref, wd_ref, o_ref, *, f_chunk):
    x = x_ref[...]
    h = _rms_rows(x, g_ref[...]).astype(BF16)
    d_ff = wg_ref.shape[1]
    acc = jnp.zeros_like(x)
    for c in range(d_ff // f_chunk):
        sl = slice(c * f_chunk, (c + 1) * f_chunk)
        gate = jnp.dot(h, wg_ref[:, sl], preferred_element_type=F32)
        up = jnp.dot(h, wu_ref[:, sl], preferred_element_type=F32)
        act = (gate * _sigmoid(gate) * up).astype(BF16)
        acc = acc + jnp.dot(act, wd_ref[sl, :], preferred_element_type=F32)
    o_ref[...] = x + 0.5 * acc


def _ffn(x, g, w_up, w_down):
    n, d = x.shape
    d_ff = w_down.shape[0]
    tm = _row_tile(n, 512)
    f_chunk = d_ff // 2 if (d_ff // 2) % LANES == 0 else d_ff
    wg = w_up[:, :d_ff].astype(BF16)
    wu = w_up[:, d_ff:].astype(BF16)
    wd = w_down.astype(BF16)
    row = pl.BlockSpec((tm, d), lambda i: (i, 0))
    return pl.pallas_call(
        functools.partial(_ffn_body, f_chunk=f_chunk),
        grid=(n // tm,),
        in_specs=[row, _const_spec((1, d)), _const_spec((d, d_ff)), _const_spec((d, d_ff)),
                  _const_spec((d_ff, d))],
        out_specs=row,
        out_shape=jax.ShapeDtypeStruct((n, d), F32),
        compiler_params=pltpu.CompilerParams(dimension_semantics=("parallel",),
                                             vmem_limit_bytes=VMEM_LIMIT),
        name="ffn",
    )(x, g.reshape(1, d), wg, wu, wd)


def _in_proj_body(x_ref, g_ref, w_ref, gq_ref, gks_ref, gkw_ref, avg_ref, rw_ref, ml_ref, gates_ref, q_ref,
                  *kv_refs, channel_major):
    h = _rms_rows(x_ref[...], g_ref[...]).astype(BF16)
    p = jnp.dot(h, w_ref[...], preferred_element_type=F32)
    rw_ref[...] = p[:, P_RW:P_ML]
    ml_ref[...] = p[:, P_ML:P_Q]
    gates_ref[...] = p[:, P_GATES:P_TOTAL]
    avg = avg_ref[...]

    def head_norm(z, g):
        w = z.shape[1]
        return z * lax.rsqrt(_seg_mean_sq(z, avg[:w, :w]) + NORM_EPS) * g

    q_ref[...] = head_norm(p[:, P_Q:P_KVC], gq_ref[...])
    kvc = p[:, P_KVC:P_KVS]
    ks = head_norm(p[:, P_KVS:P_KVS + NSA_KV], gks_ref[...])
    vs = p[:, P_KVS + NSA_KV:P_KVW]
    kw = head_norm(p[:, P_KVW:P_KVW + NSA_KV], gkw_ref[...])
    vw = p[:, P_KVW + NSA_KV:P_GATES]
    if channel_major:
        ks_ref, kw_ref, kvc_t_ref, kvs_t_ref, kvw_t_ref = kv_refs
        ks_ref[...] = ks.astype(BF16)
        kw_ref[...] = kw.astype(BF16)
        kvc_t_ref[...] = kvc.T
        kvs_t_ref[:NSA_KV, :] = ks.T
        kvs_t_ref[NSA_KV:, :] = vs.T
        kvw_t_ref[:NSA_KV, :] = kw.T
        kvw_t_ref[NSA_KV:, :] = vw.T
    else:
        kvc_ref, kvs_ref, kvw_ref = kv_refs
        kvc_ref[...] = kvc
        kvs_ref[:, :NSA_KV] = ks
        kvs_ref[:, NSA_KV:] = vs
        kvw_ref[:, :NSA_KV] = kw
        kvw_ref[:, NSA_KV:] = vw


def _permute_w_in(w_in):
    d = w_in.shape[0]
    o_ml = RW_COLS
    o_nsa = RW_COLS + ML_COLS
    ml_gates = w_in[:, o_ml + 3 * ML_WIDTH:o_nsa]
    nsa_gates = w_in[:, o_nsa + NSA_WIDTH + 6 * NSA_KV:]
    pad = jnp.zeros((d, LANES - 2 * ML_HEADS - 3 * NSA_HEADS), w_in.dtype)
    return jnp.concatenate([w_in[:, :o_ml + 3 * ML_WIDTH], w_in[:, o_nsa:o_nsa + NSA_WIDTH + 6 * NSA_KV],
                            ml_gates, nsa_gates, pad], axis=1).astype(BF16)


def _in_proj(x, g, w_perm, qk_g, channel_major):
    B, T, d = x.shape
    n = B * T
    tm = _row_tile(T if channel_major else n, 512)
    tpb = T // tm
    row = lambda w: pl.BlockSpec((tm, w), lambda i: (i, 0))
    widths = [RW_COLS, 3 * ML_WIDTH, LANES, NSA_WIDTH]
    out_specs = [row(w) for w in widths]
    out_shape = [jax.ShapeDtypeStruct((n, w), F32) for w in widths]
    w_kv = 2 * NSA_KV
    if channel_major:
        out_specs += [row(NSA_KV)] * 2 + [pl.BlockSpec((None, w_kv, tm), lambda i: (i // tpb, 0, i % tpb))] * 3
        out_shape += [jax.ShapeDtypeStruct((n, NSA_KV), BF16)] * 2 + [jax.ShapeDtypeStruct((B, w_kv, T), F32)] * 3
    else:
        out_specs += [row(w_kv)] * 3
        out_shape += [jax.ShapeDtypeStruct((n, w_kv), F32)] * 3
    gq = jnp.tile(qk_g[0], NSA_HEADS).reshape(1, NSA_WIDTH)
    gks = jnp.tile(qk_g[2], NSA_KV_HEADS).reshape(1, NSA_KV)
    gkw = jnp.tile(qk_g[3], NSA_KV_HEADS).reshape(1, NSA_KV)
    avg = _seg_avg_matrix(NSA_WIDTH, 1.0 / HEAD_DIM)
    outs = pl.pallas_call(
        functools.partial(_in_proj_body, channel_major=channel_major),
        grid=(n // tm,),
        in_specs=[row(d), _const_spec((1, d)), _const_spec((d, P_TOTAL)), _const_spec((1, NSA_WIDTH)),
                  _const_spec((1, NSA_KV)), _const_spec((1, NSA_KV)), _const_spec((NSA_WIDTH, NSA_WIDTH))],
        out_specs=out_specs,
        out_shape=out_shape,
        compiler_params=pltpu.CompilerParams(dimension_semantics=("parallel",),
                                             vmem_limit_bytes=VMEM_LIMIT),
        name="in_proj",
    )(x.reshape(n, d), g.reshape(1, d), w_perm, gq, gks, gkw, avg)
    return [o.reshape(B, T, -1) if o.shape[0] == n else o for o in outs]


def _rwkv_body(p_ref, shift_ref, s0_ref, mu_ref, w0_ref, lw_ref, a0_ref, g2_ref, kks_ref, ka_ref,
               rk_ref, gng_ref, gnb_ref, tri_ref, seg_ref, o_ref, st_ref, carry_ref, s_ref, *, chunk):
    C = chunk
    N = HEAD_DIM

    @pl.when(pl.program_id(1) == 0)
    def _():
        carry_ref[...] = shift_ref[...]
        s_ref[...] = s0_ref[...]

    p = p_ref[...]
    rows = lax.broadcasted_iota(jnp.int32, (C, 1), 0)
    prev = jnp.where(rows == 0, carry_ref[...], pltpu.roll(p, 1, axis=0))
    carry_ref[...] = p[C - 1:C, :]
    xm = p + (prev - p) * mu_ref[...]
    r = xm[:, 0:RW_WIDTH]
    k = xm[:, RW_WIDTH:2 * RW_WIDTH]
    v = xm[:, 2 * RW_WIDTH:3 * RW_WIDTH]
    lin = xm[:, 3 * RW_WIDTH:3 * RW_WIDTH + RW_W_LORA + RW_A_LORA]
    lane = lax.broadcasted_iota(jnp.int32, lin.shape, 1)
    lora = _dot_hi(jnp.where(lane < RW_W_LORA, jnp.tanh(lin), lin), lw_ref[...])
    w = -_softplus(-(w0_ref[...] + lora[:, :RW_WIDTH])) - 0.5
    a = _sigmoid(a0_ref[...] + lora[:, RW_WIDTH:])
    g = _dot(_sigmoid(xm[:, 3 * RW_WIDTH + RW_W_LORA + RW_A_LORA:]), g2_ref[...])
    kk = k * kks_ref[...]
    hi, lo = _split2(kk * kk)
    seg = seg_ref[...]
    ss = jnp.dot(hi, seg, preferred_element_type=F32) + jnp.dot(lo, seg, preferred_element_type=F32)
    kk = kk / jnp.maximum(jnp.sqrt(ss), 1e-12)
    k = k * (1.0 + (a - 1.0) * ka_ref[...])
    logdec = -jnp.exp(w)
    tri = tri_ref[...]
    G = _dot_sel(tri, logdec)
    g_end = G[C - 1:C, :]
    e_g = jnp.exp(G)
    e_gi = jnp.exp(-G)
    kkd = kk * jnp.exp(G - logdec)
    rd = r * e_g
    b = kk * a
    bi = b * e_gi
    ki = k * e_gi
    e_end = jnp.exp(g_end - G)
    bi2 = b * e_end
    ki2 = k * e_end
    dec_end = jnp.exp(g_end)

    ti = lax.broadcasted_iota(jnp.int32, (C, C), 0)
    si = lax.broadcasted_iota(jnp.int32, (C, C), 1)
    strict = si < ti
    incl = si <= ti
    eye = (si == ti).astype(F32)
    rkk = r * k * rk_ref[...]

    for h in range(RW_HEADS):
        sl = slice(h * N, (h + 1) * N)
        lhs = jnp.concatenate([kkd[:, sl], rd[:, sl]], axis=0)
        rhs = jnp.concatenate([bi[:, sl], ki[:, sl]], axis=0)
        vh = v[:, sl]
        s0 = s_ref[h]
        m4 = _dot_nt(lhs, rhs)
        ks = _dot_nt(lhs, s0)
        nb = jnp.where(strict, -m4[:C, :C], 0.0)
        lk = jnp.where(strict, m4[:C, C:], 0.0)
        mb = jnp.where(incl, m4[C:, :C], 0.0)
        mk = jnp.where(incl, m4[C:, C:], 0.0)
        t_inv = eye + nb
        pw = nb
        span = 2
        while span < C:
            pw = _dot(pw, pw)
            t_inv = t_inv + _dot(t_inv, pw)
            span *= 2
        u = _dot(t_inv, -(ks[:C] + _dot(lk, vh)))
        uv = jnp.concatenate([u, vh], axis=0)
        y = ks[C:] + _dot(jnp.concatenate([mb, mk], axis=1), uv)
        s_new = s0 * dec_end[:, sl] + _dot_tn(uv, jnp.concatenate([bi2[:, sl], ki2[:, sl]], axis=0))
        s_ref[h] = s_new
        mean = jnp.mean(y, axis=-1, keepdims=True)
        var = jnp.mean(jnp.square(y - mean), axis=-1, keepdims=True)
        yn = (y - mean) * lax.rsqrt(var + RW_GN_EPS) * gng_ref[:, sl] + gnb_ref[:, sl]
        bonus = jnp.sum(rkk[:, sl], axis=-1, keepdims=True) * vh
        o_ref[:, sl] = (yn + bonus) * g[:, sl]

    @pl.when(pl.program_id(1) == pl.num_programs(1) - 1)
    def _():
        st_ref[...] = s_ref[...]


def _rwkv(p_rw, shift_prev, s0, W):
    B, T, _ = p_rw.shape
    C = ML_CHUNK if T % ML_CHUNK == 0 else T
    row = lambda v: v.reshape(1, -1).astype(F32)
    z = jnp.zeros((RW_W_LORA, RW_WIDTH), F32)
    lw = jnp.concatenate([jnp.concatenate([W['rw_w2'], z], axis=1),
                          jnp.concatenate([z, W['rw_a2']], axis=1)], axis=0)
    tri = (jnp.arange(C)[:, None] >= jnp.arange(C)[None, :]).astype(BF16)
    seg = _seg_avg_matrix(RW_WIDTH, 1.0)
    consts = [row(W['rw_mu']), row(W['rw_w0']), lw, row(W['rw_a0']), W['rw_g2'].astype(BF16), row(W['rw_kk']),
              row(W['rw_ka']), row(W['rw_rk']), row(W['rw_gn_g']), row(W['rw_gn_b']), tri, seg]
    out, s_t = pl.pallas_call(
        functools.partial(_rwkv_body, chunk=C),
        grid=(B, T // C),
        in_specs=[pl.BlockSpec((None, C, RW_COLS), lambda b, c: (b, c, 0)),
                  pl.BlockSpec((None, 1, RW_COLS), lambda b, c: (b, 0, 0)),
                  pl.BlockSpec((None, RW_HEADS, HEAD_DIM, HEAD_DIM), lambda b, c: (b, 0, 0, 0))]
                 + [_const_spec(c.shape) for c in consts],
        out_specs=[pl.BlockSpec((None, C, RW_WIDTH), lambda b, c: (b, c, 0)),
                   pl.BlockSpec((None, RW_HEADS, HEAD_DIM, HEAD_DIM), lambda b, c: (b, 0, 0, 0))],
        out_shape=[jax.ShapeDtypeStruct((B, T, RW_WIDTH), F32),
                   jax.ShapeDtypeStruct((B, RW_HEADS, HEAD_DIM, HEAD_DIM), F32)],
        scratch_shapes=[pltpu.VMEM((1, RW_COLS), F32), pltpu.VMEM((RW_HEADS, HEAD_DIM, HEAD_DIM), F32)],
        compiler_params=pltpu.CompilerParams(dimension_semantics=("parallel", "arbitrary"),
                                             vmem_limit_bytes=VMEM_LIMIT),
        name="rwkv7",
    )(p_rw, shift_prev, s0, *consts)
    return out, s_t


def _mlstm_body(p_ref, gc_ref, gt_ref, convp_ref, c0_ref, n0_ref, m0_ref, cw_ref, cb_ref, wq_ref, wk_ref,
                bias_r_ref, bias_c_ref, ng_ref, skip_ref, tril_ref, triu_ref,
                o_ref, ct_ref, nt_ref, mt_ref, ext_ref, c_ref, n_ref, m_ref, *, chunk):
    L = chunk
    N = HEAD_DIM
    H = ML_HEADS

    @pl.when(pl.program_id(1) == 0)
    def _():
        ext_ref[0:8, :] = convp_ref[...]
        c_ref[...] = c0_ref[...]
        n_ref[...] = n0_ref[...]
        m_ref[...] = m0_ref[...]

    p = p_ref[...]
    x = p[:, 0:ML_WIDTH]
    v = p[:, ML_WIDTH:2 * ML_WIDTH]
    o_pre = p[:, 2 * ML_WIDTH:3 * ML_WIDTH]
    ext_ref[8:8 + L, :] = x
    cw = cw_ref[...]
    conv = (cb_ref[...] + cw[3:4, :] * x + cw[2:3, :] * ext_ref[7:7 + L, :]
            + cw[1:2, :] * ext_ref[6:6 + L, :] + cw[0:1, :] * ext_ref[5:5 + L, :])
    tail = ext_ref[L:L + 8, :]
    ext_ref[0:8, :] = tail
    ca = conv * _sigmoid(conv)
    q = _dot(ca, wq_ref[...])
    k = _dot(ca, wk_ref[...]) * (N ** -0.5)

    gcb = gc_ref[...] + bias_r_ref[...]
    lf_c = -_softplus(-gcb)
    bcum_c = _dot_sel(tril_ref[...], lf_c)
    gtb = gt_ref[...] + bias_c_ref[...][:, 0:1]
    lf_r = -_softplus(-gtb)
    bcum_r = _dot_x_sel(lf_r, triu_ref[...])

    ti = lax.broadcasted_iota(jnp.int32, (L, L), 0)
    si = lax.broadcasted_iota(jnp.int32, (L, L), 1)
    causal = si <= ti

    for h in range(H):
        sl = slice(h * N, (h + 1) * N)
        qh, kh, vh = q[:, sl], k[:, sl], v[:, sl]
        b_c = bcum_c[:, H + h:H + h + 1]
        i_c = gcb[:, h:h + 1]
        b_r = bcum_r[H + h:H + h + 1, :]
        i_r = gtb[h:h + 1, :]
        m_prev = m_ref[h:h + 1, 0:1]
        dmat = jnp.where(causal, b_c - b_r + i_r, -jnp.inf)
        inter = b_c + m_prev
        m_t = jnp.maximum(inter, jnp.max(dmat, axis=-1, keepdims=True))
        a = _dot_nt(qh, kh) * jnp.exp(dmat - m_t)
        sc = jnp.exp(inter - m_t)
        cs = c_ref[h]
        ns = n_ref[h:h + 1, :]
        num = sc * _dot_nt(qh, cs) + _dot(a, vh)
        den = sc * jnp.sum(qh * ns, axis=-1, keepdims=True) + jnp.sum(a, axis=-1, keepdims=True)
        hh = num / jnp.maximum(jnp.abs(den), jnp.exp(-m_t))
        b_end = b_c[L - 1:L, :]
        gs_r = b_end - b_r + i_r
        m_new = jnp.maximum(b_end + m_prev, jnp.max(gs_r, axis=-1, keepdims=True))
        dec = jnp.exp(b_end + m_prev - m_new)
        wg = jnp.exp(b_end - b_c + i_c - m_new)
        c_ref[h] = dec * cs + _dot_tn(wg * vh, kh)
        n_ref[h:h + 1, :] = dec * ns + jnp.sum(wg * kh, axis=0, keepdims=True)
        m_ref[h:h + 1, :] = jnp.broadcast_to(m_new, (1, LANES))
        hn = hh * lax.rsqrt(jnp.mean(hh * hh, axis=-1, keepdims=True) + NORM_EPS)
        o_ref[:, sl] = (hn * ng_ref[:, sl] + skip_ref[:, sl] * ca[:, sl]) * _sigmoid(o_pre[:, sl])

    @pl.when(pl.program_id(1) == pl.num_programs(1) - 1)
    def _():
        ct_ref[...] = c_ref[...]
        nt_ref[...] = n_ref[...]
        mt_ref[...] = m_ref[...]


def _block_diag(w):
    H, N, _ = w.shape
    eye = jnp.eye(H, dtype=w.dtype)
    return (eye[:, None, :, None] * w[:, :, None, :]).reshape(H * N, H * N)


def _mlstm(p_ml, gates, conv_prev, c0, n0, m0, W):
    B, T, _ = p_ml.shape
    H, N = ML_HEADS, HEAD_DIM
    L = ML_CHUNK if T % ML_CHUNK == 0 else T
    nc = T // L
    gt = jnp.swapaxes(gates[:, :, :2 * H].reshape(B, nc, L, 2 * H), 2, 3)
    convp = jnp.concatenate([jnp.zeros((B, 8 - (ML_CONV - 1), ML_WIDTH), F32), conv_prev], axis=1)
    m0b = jnp.broadcast_to(jnp.pad(m0, ((0, 0), (0, 8 - H)))[:, :, None], (B, 8, LANES))
    bias = W['ml_gate_b'].reshape(2 * H)
    bias_r = jnp.pad(bias, (0, LANES - 2 * H)).reshape(1, LANES)
    bias_c = jnp.broadcast_to(bias[:, None], (2 * H, LANES))
    tril = (jnp.arange(L)[:, None] >= jnp.arange(L)[None, :]).astype(BF16)
    row = lambda z: z.reshape(1, -1).astype(F32)
    consts = [W['ml_conv_w'], row(W['ml_conv_b']), _block_diag(W['ml_wq']).astype(BF16),
              _block_diag(W['ml_wk']).astype(BF16), bias_r, bias_c, row(W['ml_norm_g']), row(W['ml_skip']),
              tril, tril.T]
    per_b = lambda *shape: pl.BlockSpec((None,) + shape, lambda b, c: (b,) + (0,) * len(shape))
    out, c_t, n_t, m_t = pl.pallas_call(
        functools.partial(_mlstm_body, chunk=L),
        grid=(B, nc),
        in_specs=[pl.BlockSpec((None, L, 3 * ML_WIDTH), lambda b, c: (b, c, 0)),
                  pl.BlockSpec((None, L, LANES), lambda b, c: (b, c, 0)),
                  pl.BlockSpec((None, None, 2 * H, L), lambda b, c: (b, c, 0, 0)),
                  per_b(8, ML_WIDTH), per_b(H, N, N), per_b(H, N), per_b(8, LANES)]
                 + [_const_spec(c.shape) for c in consts],
        out_specs=[pl.BlockSpec((None, L, ML_WIDTH), lambda b, c: (b, c, 0)),
                   per_b(H, N, N), per_b(H, N), per_b(8, LANES)],
        out_shape=[jax.ShapeDtypeStruct((B, T, ML_WIDTH), F32), jax.ShapeDtypeStruct((B, H, N, N), F32),
                   jax.ShapeDtypeStruct((B, H, N), F32), jax.ShapeDtypeStruct((B, 8, LANES), F32)],
        scratch_shapes=[pltpu.VMEM((L + 8, ML_WIDTH), F32), pltpu.VMEM((H, N, N), F32),
                        pltpu.VMEM((H, N), F32), pltpu.VMEM((8, LANES), F32)],
        compiler_params=pltpu.CompilerParams(dimension_semantics=("parallel", "arbitrary"),
                                             vmem_limit_bytes=VMEM_LIMIT),
        name="mlstm",
    )(p_ml, gates, gt, convp, c0, n0, m0b, *consts)
    return out, c_t, n_t, m_t[:, :H, 0]


def _compress_body(*refs, n_parts, paged):
    if paged:
        refs = refs[1:]
    part_refs = refs[:n_parts]
    wpos_ref, sel_ref, wc_ref, gk_ref, avg_ref, o_ref = refs[n_parts:]
    wpos = wpos_ref[...]
    parts = []
    for r in part_refs:
        x = r[...]
        parts += [x[:, c * LANES:(c + 1) * LANES] * wpos for c in range(x.shape[1] // LANES)]
    hi, lo = _split2(jnp.concatenate(parts, axis=1))
    sel = sel_ref[...]
    c = jnp.dot(hi, sel, preferred_element_type=F32) + jnp.dot(lo, sel, preferred_element_type=F32)
    c = _dot_hi(wc_ref[...], c)
    ck = c[:NSA_KV]
    o_ref[:NSA_KV, :] = ck * lax.rsqrt(_seg_mean_sq_rows(ck, avg_ref[...]) + NORM_EPS) * gk_ref[...]
    o_ref[NSA_KV:, :] = c[NSA_KV:]


def _compress_consts(W, n_rows):
    w = 2 * NSA_KV
    wpos = W['nsa_wpos'].reshape(NSA_BLOCK, w).T
    wc_t = _block_diag(W['nsa_wc'].reshape(2 * NSA_KV_HEADS, HEAD_DIM, HEAD_DIM)).T
    gk = jnp.tile(W['nsa_qk_g'][1], NSA_KV_HEADS).reshape(NSA_KV, 1)
    sel = (jnp.arange(n_rows)[:, None] // NSA_BLOCK == jnp.arange(n_rows // NSA_BLOCK)[None, :]).astype(BF16)
    return [jnp.concatenate([wpos, wpos], axis=1), sel, wc_t, gk, _seg_avg_matrix(NSA_KV, 1.0 / HEAD_DIM)]


def _nsa_compress_dense(kvc_t, W):
    B, w, T = kvc_t.shape
    consts = _compress_consts(W, T)
    return pl.pallas_call(
        functools.partial(_compress_body, n_parts=1, paged=False),
        grid=(B,),
        in_specs=[pl.BlockSpec((None, w, T), lambda b: (b, 0, 0))] + [_const_spec(c.shape) for c in consts],
        out_specs=pl.BlockSpec((None, w, T // NSA_BLOCK), lambda b: (b, 0, 0)),
        out_shape=jax.ShapeDtypeStruct((B, w, T // NSA_BLOCK), F32),
        compiler_params=pltpu.CompilerParams(dimension_semantics=("parallel",), vmem_limit_bytes=VMEM_LIMIT),
        name="nsa_compress_dense",
    )(kvc_t, *consts)


def _nsa_compress_paged(pool_t, page_table, W):
    B, n_pages = page_table.shape
    w = pool_t.shape[1]
    pages = min(64, n_pages)
    bpp = PAGE_SIZE // NSA_BLOCK
    consts = _compress_consts(W, pages * PAGE_SIZE)

    def page_spec(k):
        return pl.BlockSpec((None, w, PAGE_SIZE), lambda b, i, pt: (pt[b * n_pages + i * pages + k], 0, 0))

    const = lambda c: pl.BlockSpec(c.shape, lambda b, i, pt: (0,) * c.ndim)
    return pl.pallas_call(
        functools.partial(_compress_body, n_parts=pages, paged=True),
        grid_spec=pltpu.PrefetchScalarGridSpec(
            num_scalar_prefetch=1,
            grid=(B, n_pages // pages),
            in_specs=[page_spec(k) for k in range(pages)] + [const(c) for c in consts],
            out_specs=pl.BlockSpec((None, w, pages * bpp), lambda b, i, pt: (b, 0, i)),
        ),
        out_shape=jax.ShapeDtypeStruct((B, w, n_pages * bpp), F32),
        compiler_params=pltpu.CompilerParams(dimension_semantics=("parallel", "parallel"),
                                             vmem_limit_bytes=VMEM_LIMIT),
        name="nsa_compress_paged",
    )(page_table.reshape(-1), *([pool_t] * pages), *consts)


def _split_cmp(cmp_t):
    return jnp.swapaxes(cmp_t[:, :NSA_KV], 1, 2), cmp_t[:, NSA_KV:]


def _gate_bias_row(gate_b):
    return jnp.pad(gate_b, (2 * ML_HEADS, LANES - 2 * ML_HEADS - 3 * NSA_HEADS)).reshape(1, LANES)


def _gate_lane(g, r):
    return 2 * ML_HEADS + (g * NSA_GROUP + r) * 3


def _attend_t(k_tile, v_ext, qpads, mask, carry):
    out = []
    for qp, (m, acc) in zip(qpads, carry):
        s = jnp.where(mask, jnp.dot(k_tile, qp, preferred_element_type=F32), NEG)
        m_new = jnp.maximum(m, jnp.max(s, axis=0, keepdims=True))
        p = jnp.exp(s - m_new)
        acc = jnp.exp(m - m_new) * acc + jnp.dot(v_ext, p.astype(BF16), preferred_element_type=F32)
        out.append((m_new, acc))
    return tuple(out)


def _attend_t_init(tq):
    return tuple((jnp.full((1, tq), NEG, F32), jnp.zeros((HEAD_DIM + 8, tq), F32)) for _ in range(NSA_GROUP))


def _attend_t_finish(carry):
    return [acc[:HEAD_DIM] / jnp.maximum(acc[HEAD_DIM:HEAD_DIM + 1], 1e-30) for _, acc in carry]


def _nsa_prompt_body(q_ref, gates_ref, gb_ref, ck_ref, cv_ref, ks_ref, vs_ref, kw_ref, vw_ref, o_ref,
                     sel_ref, ot_ref, *, tq, tk, n_blocks):
    R, N = NSA_GROUP, HEAD_DIM
    i = pl.program_id(1)
    t0 = i * tq
    qpos = t0 + lax.broadcasted_iota(jnp.int32, (1, tq), 1)
    q_t = q_ref[...].T * (N ** -0.5)
    gt_t = _sigmoid((gates_ref[...] + gb_ref[...]).T)
    ck = ck_ref[...]
    nc = ck.shape[0]
    blk_r = lax.broadcasted_iota(jnp.int32, (nc, tq), 0)
    ones = jnp.ones((8, tk), BF16)
    zeros = jnp.zeros((N, tq), F32)
    for g in range(NSA_KV_HEADS):
        vrows = slice(g * N, (g + 1) * N)
        heads = [q_t[(g * R + r) * N:(g * R + r + 1) * N] for r in range(R)]
        qpad_f = [jnp.concatenate([h, zeros] if g == 0 else [zeros, h], axis=0) for h in heads]
        qpads = [x.astype(BF16) for x in qpad_f]

        vis = (blk_r + 1) * NSA_BLOCK - 1 <= qpos
        imp = jnp.zeros((nc, tq), F32)
        o_c = []
        for r in range(R):
            s = jnp.where(vis, _dot_hi(ck, qpad_f[r]), -jnp.inf)
            m = jnp.max(s, axis=0, keepdims=True)
            e = jnp.exp(s - jnp.where(m == -jnp.inf, 0.0, m))
            p = e / jnp.maximum(jnp.sum(e, axis=0, keepdims=True), 1e-30)
            imp = imp + p
            o_c.append(_dot(cv_ref[vrows, :], p))

        cur = jnp.right_shift(qpos, BLOCK_SHIFT)
        forced = (blk_r == 0) | (blk_r == cur) | (blk_r == cur - 1)
        score = jnp.where(forced, BIG, jnp.where(blk_r <= cur, imp, -BIG))
        rowf = blk_r.astype(F32)
        sel = jnp.zeros((nc, tq), F32)
        for _ in range(min(NSA_TOPN, n_blocks)):
            m = jnp.max(score, axis=0, keepdims=True)
            first = jnp.min(jnp.where(score == m, rowf, float(nc)), axis=0, keepdims=True)
            hit = rowf == first
            sel = jnp.where(hit, 1.0, sel)
            score = jnp.where(hit, -jnp.inf, score)
        sel_ref[...] = sel

        kpos_col = lax.broadcasted_iota(jnp.int32, (tk, 1), 0)

        def sel_step(kt, carry):
            k0 = pl.multiple_of(kt * tk, tk)
            j0 = kt * (tk // NSA_BLOCK)
            chosen = jnp.concatenate(
                [jnp.broadcast_to(sel_ref[pl.ds(j0 + m, 1), :], (NSA_BLOCK, tq)) for m in range(tk // NSA_BLOCK)],
                axis=0)
            mask = (chosen > 0.5) & (k0 + kpos_col <= qpos)
            v_ext = jnp.concatenate([vs_ref[vrows, pl.ds(k0, tk)].astype(BF16), ones], axis=0)
            return _attend_t(ks_ref[pl.ds(k0, tk), :], v_ext, qpads, mask, carry)

        n_kt = (t0 + tq + tk - 1) // tk
        o_s = _attend_t_finish(lax.fori_loop(0, n_kt, sel_step, _attend_t_init(tq)))

        kpos_w = lax.broadcasted_iota(jnp.int32, (tq, 1), 0)

        def win_step(kt, carry):
            k0 = pl.multiple_of(kt * tq, tq)
            d = qpos - (k0 + kpos_w)
            mask = (d >= 0) & (d < NSA_WINDOW)
            v_ext = jnp.concatenate([vw_ref[vrows, pl.ds(k0, tq)].astype(BF16), ones[:, :tq]], axis=0)
            return _attend_t(kw_ref[pl.ds(k0, tq), :], v_ext, qpads, mask, carry)

        lo = jnp.maximum(i - NSA_WINDOW // tq, 0)
        o_w = _attend_t_finish(lax.fori_loop(lo, i + 1, win_step, _attend_t_init(tq)))

        for r in range(R):
            c0 = _gate_lane(g, r)
            h = g * R + r
            ot_ref[h * N:(h + 1) * N, :] = (gt_t[c0:c0 + 1] * o_c[r] + gt_t[c0 + 1:c0 + 2] * o_s[r]
                                            + gt_t[c0 + 2:c0 + 3] * o_w[r])
    o_ref[...] = ot_ref[...].T


def _nsa_prompt(q, gates, ck, cv_t, ks, kvs_t, kw, kvw_t, W):
    B, T, _ = q.shape
    tq = min(128, T)
    tk = min(256, T)
    nc = ck.shape[1]
    per_b = lambda a: pl.BlockSpec((None,) + a.shape[1:], lambda b, i: (b,) + (0,) * (a.ndim - 1))
    tile = lambda w: pl.BlockSpec((None, tq, w), lambda b, i: (b, i, 0))
    v_rows = pl.BlockSpec((None, NSA_KV, T), lambda b, i: (b, 1, 0))
    gb = _gate_bias_row(W['nsa_gate_b'])
    return pl.pallas_call(
        functools.partial(_nsa_prompt_body, tq=tq, tk=tk, n_blocks=T // NSA_BLOCK),
        grid=(B, T // tq),
        in_specs=[tile(NSA_WIDTH), tile(LANES), pl.BlockSpec((1, LANES), lambda b, i: (0, 0)),
                  per_b(ck), per_b(cv_t), per_b(ks), v_rows, per_b(kw), v_rows],
        out_specs=tile(NSA_WIDTH),
        out_shape=jax.ShapeDtypeStruct((B, T, NSA_WIDTH), F32),
        scratch_shapes=[pltpu.VMEM((nc, tq), F32), pltpu.VMEM((NSA_WIDTH, tq), F32)],
        compiler_params=pltpu.CompilerParams(dimension_semantics=("parallel", "arbitrary"),
                                             vmem_limit_bytes=VMEM_LIMIT),
        name="nsa_prompt",
    )(q, gates, gb, ck, cv_t, ks, kvs_t, kw, kvw_t)


def _stack_heads(qg):
    return jnp.concatenate([qg[:, r * HEAD_DIM:(r + 1) * HEAD_DIM] for r in range(NSA_GROUP)],
                           axis=0) * (HEAD_DIM ** -0.5)


def _tile_rows(x, n):
    return jnp.concatenate([x] * n, axis=0)


def _masked_softmax(s, mask):
    s = jnp.where(mask, s, -jnp.inf)
    m = jnp.max(s, axis=-1, keepdims=True)
    m = jnp.where(m == -jnp.inf, 0.0, m)
    e = jnp.exp(s - m)
    return e / jnp.maximum(jnp.sum(e, axis=-1, keepdims=True), 1e-30)


def _select_blocks(imp, qpos, n_blocks, k):
    t, w = imp.shape
    jf = lax.broadcasted_iota(jnp.int32, (t, w), 1)
    cur = jnp.right_shift(qpos, BLOCK_SHIFT)
    forced = (jf == 0) | (jf == cur) | (jf == cur - 1)
    score = jnp.where(forced, BIG, jnp.where(jf <= cur, imp, -BIG))
    score = jnp.where(jf < n_blocks, score, -jnp.inf)
    lane = jf.astype(F32)
    idx = []
    for _ in range(k):
        m = jnp.max(score, axis=-1, keepdims=True)
        i = jnp.min(jnp.where(score == m, lane, float(w)), axis=-1, keepdims=True)
        score = jnp.where(lane == i, -jnp.inf, score)
        idx.append(i)
    return idx


def _online_step(carry, s, ok, pv):
    m, l, acc = carry
    s = jnp.where(ok, s, -jnp.inf)
    m_new = jnp.maximum(m, jnp.max(s, axis=-1, keepdims=True))
    m_safe = jnp.where(m_new == -jnp.inf, 0.0, m_new)
    p = jnp.exp(s - m_safe)
    alpha = jnp.exp(m - m_safe)
    return m_new, alpha * l + jnp.sum(p, axis=-1, keepdims=True), alpha * acc + pv(p)


def _online_init(rows):
    return (jnp.full((rows, 1), -jnp.inf, F32), jnp.zeros((rows, 1), F32), jnp.zeros((rows, HEAD_DIM), F32))


def _online_finish(carry):
    _, l, acc = carry
    return acc / jnp.maximum(l, 1e-30)


def _nsa_sample_cmp_body(q_ref, ck_ref, cv_ref, kvw_ref, winp_ref, oc_ref, ow_ref, idx_ref, *, pos0, n_blocks):
    R, N = NSA_GROUP, HEAD_DIM
    t = q_ref.shape[0]
    wb = winp_ref.shape[1]
    qpos = pos0 + lax.broadcasted_iota(jnp.int32, (t, 1), 0)
    qpos_s = _tile_rows(qpos, R)
    q = q_ref[...]
    ck = ck_ref[...]
    nc = ck.shape[0]
    w_sel = -(-n_blocks // LANES) * LANES
    new = kvw_ref[...]
    oc_heads, ow_heads = [], []
    for g in range(NSA_KV_HEADS):
        krows = slice(g * N, (g + 1) * N)
        vrows = slice(NSA_KV + g * N, NSA_KV + (g + 1) * N)
        qs = _stack_heads(q[:, g * R * N:(g + 1) * R * N])
        blk = lax.broadcasted_iota(jnp.int32, (1, nc), 1)
        p = _masked_softmax(_dot_nt_hi(qs, ck[:, krows]), (blk + 1) * NSA_BLOCK - 1 <= qpos_s)
        o_c = _dot_nt(p, cv_ref[krows, :])
        imp = p[0:t]
        for r in range(1, R):
            imp = imp + p[r * t:(r + 1) * t]
        if w_sel > nc:
            imp = jnp.concatenate([imp, jnp.zeros((t, w_sel - nc), F32)], axis=1)
        idx = _select_blocks(imp, qpos, n_blocks, min(NSA_TOPN, n_blocks))
        lane = lax.broadcasted_iota(jnp.int32, (t, LANES), 1)
        tile = jnp.zeros((t, LANES), F32)
        for kk, col in enumerate(idx):
            tile = jnp.where(lane == kk, col, tile)
        idx_ref[g] = tile.astype(jnp.int32)

        carry = _online_init(R * t)
        d = qpos_s - (pos0 - wb + lax.broadcasted_iota(jnp.int32, (1, wb), 1))
        carry = _online_step(carry, _dot(qs, winp_ref[krows, :]), (d >= 0) & (d < NSA_WINDOW),
                             lambda p: _dot_nt(p, winp_ref[vrows, :]))
        d = qpos_s - (pos0 + lax.broadcasted_iota(jnp.int32, (1, t), 1))
        carry = _online_step(carry, _dot_nt(qs, new[:, krows]), (d >= 0) & (d < NSA_WINDOW),
                             lambda p: _dot(p, new[:, vrows]))
        o_w = _online_finish(carry)
        oc_heads += [o_c[r * t:(r + 1) * t] for r in range(R)]
        ow_heads += [o_w[r * t:(r + 1) * t] for r in range(R)]
    oc_ref[...] = jnp.concatenate(oc_heads, axis=1)
    ow_ref[...] = jnp.concatenate(ow_heads, axis=1)


def _nsa_sample_cmp(q, ck, cv_t, kv_w, win_prev_t, pos0, n_blocks):
    B, T, _ = q.shape
    per_b = lambda a: pl.BlockSpec((None,) + a.shape[1:], lambda b: (b,) + (0,) * (a.ndim - 1))
    out_b = lambda *s: pl.BlockSpec((None,) + s, lambda b: (b,) + (0,) * len(s))
    return pl.pallas_call(
        functools.partial(_nsa_sample_cmp_body, pos0=pos0, n_blocks=n_blocks),
        grid=(B,),
        in_specs=[per_b(q), per_b(ck), per_b(cv_t), per_b(kv_w), per_b(win_prev_t)],
        out_specs=[out_b(T, NSA_WIDTH), out_b(T, NSA_WIDTH), out_b(NSA_KV_HEADS, T, LANES)],
        out_shape=[jax.ShapeDtypeStruct((B, T, NSA_WIDTH), F32), jax.ShapeDtypeStruct((B, T, NSA_WIDTH), F32),
                   jax.ShapeDtypeStruct((B, NSA_KV_HEADS, T, LANES), jnp.int32)],
        compiler_params=pltpu.CompilerParams(dimension_semantics=("parallel",)),
        name="nsa_sample_cmp",
    )(q, ck, cv_t, kv_w, win_prev_t)


def _nsa_sample_sel_body(idx_ref, pt_ref, *refs, n_past, pos0, topk):
    R, N, G = NSA_GROUP, HEAD_DIM, NSA_KV_HEADS
    kv_refs = refs[:2 * G * topk]
    q_ref, new_ref, oc_ref, ow_ref, gates_ref, gb_ref, o_ref = refs[2 * G * topk:]
    b = pl.program_id(0)
    t = pl.program_id(1)
    n_t = pl.num_programs(1)
    qpos = pos0 + t
    gt = _sigmoid(gates_ref[pl.ds(t, 1), :] + gb_ref[...])
    q = q_ref[pl.ds(t, 1), :]
    o_c = oc_ref[pl.ds(t, 1), :]
    o_w = ow_ref[pl.ds(t, 1), :]
    lane = lax.broadcasted_iota(jnp.int32, (1, PAGE_SIZE), 1)
    heads = []
    for g in range(G):
        qs = _stack_heads(q[:, g * R * N:(g + 1) * R * N])
        new_k = new_ref[g * N:(g + 1) * N, :]
        new_v = new_ref[NSA_KV + g * N:NSA_KV + (g + 1) * N, :]
        scores, vals = [], []
        for k in range(topk):
            j = idx_ref[((b * n_t + t) * G + g) * topk + k]
            half = j % (PAGE_SIZE // NSA_BLOCK)
            is_new = j >= n_past
            k_t = jnp.where(is_new, new_k, kv_refs[2 * (g * topk + k)][...])
            v_t = jnp.where(is_new, new_v, kv_refs[2 * (g * topk + k) + 1][...])
            kpos = (j - half) * NSA_BLOCK + lane
            ok = (jnp.right_shift(lane, BLOCK_SHIFT) == half) & (kpos <= qpos)
            scores.append(jnp.where(ok, _dot(qs, k_t), -jnp.inf))
            vals.append(v_t)
        m = scores[0].max(axis=-1, keepdims=True)
        for s in scores[1:]:
            m = jnp.maximum(m, s.max(axis=-1, keepdims=True))
        m = jnp.where(m == -jnp.inf, 0.0, m)
        l = jnp.zeros((R, 1), F32)
        acc = jnp.zeros((R, N), F32)
        for s, v_t in zip(scores, vals):
            e = jnp.exp(s - m)
            l = l + jnp.sum(e, axis=-1, keepdims=True)
            acc = acc + _dot_nt(e, v_t)
        o_s = acc / jnp.maximum(l, 1e-30)
        for r in range(R):
            c0 = (g * R + r) * N
            gl = _gate_lane(g, r)
            heads.append(gt[:, gl:gl + 1] * o_c[:, c0:c0 + N] + gt[:, gl + 1:gl + 2] * o_s[r:r + 1]
                         + gt[:, gl + 2:gl + 3] * o_w[:, c0:c0 + N])
    o_ref[pl.ds(t, 1), :] = jnp.concatenate(heads, axis=1)


def _nsa_sample_sel(q, idx, page_table, pool_t, kv_s_new, o_c, o_w, gates, W, pos0):
    B, T, _ = q.shape
    G, N = NSA_KV_HEADS, HEAD_DIM
    topk = idx.shape[-1]
    n_pages = page_table.shape[1]
    bpp = PAGE_SIZE // NSA_BLOCK
    n_past = pos0 // NSA_BLOCK
    assert n_past % bpp == 0 and T <= NSA_BLOCK
    new_t = jnp.swapaxes(jnp.pad(kv_s_new, ((0, 0), (0, PAGE_SIZE - T), (0, 0))), 1, 2)

    def kv_spec(g, k, is_v):
        def index(b, t, idx_ref, pt_ref):
            j = jnp.minimum(idx_ref[((b * T + t) * G + g) * topk + k], n_past - 1)
            return (pt_ref[b * n_pages + j // bpp], (G if is_v else 0) + g, 0)
        return pl.BlockSpec((None, N, PAGE_SIZE), index)

    per_b = lambda a: pl.BlockSpec((None,) + a.shape[1:], lambda b, t, i, p: (b,) + (0,) * (a.ndim - 1))
    gb = _gate_bias_row(W['nsa_gate_b'])
    kv_specs = [kv_spec(g, k, is_v) for g in range(G) for k in range(topk) for is_v in (False, True)]
    return pl.pallas_call(
        functools.partial(_nsa_sample_sel_body, n_past=n_past, pos0=pos0, topk=topk),
        grid_spec=pltpu.PrefetchScalarGridSpec(
            num_scalar_prefetch=2,
            grid=(B, T),
            in_specs=kv_specs + [per_b(q), per_b(new_t), per_b(o_c), per_b(o_w), per_b(gates),
                                 pl.BlockSpec((1, LANES), lambda b, t, i, p: (0, 0))],
            out_specs=pl.BlockSpec((None, T, NSA_WIDTH), lambda b, t, i, p: (b, 0, 0)),
        ),
        out_shape=jax.ShapeDtypeStruct((B, T, NSA_WIDTH), F32),
        compiler_params=pltpu.CompilerParams(dimension_semantics=("parallel", "arbitrary")),
        name="nsa_sample_sel",
    )(idx.reshape(-1), page_table.reshape(-1), *([pool_t] * len(kv_specs)), q, new_t, o_c, o_w, gates, gb)


def _nsa_sample_mixer(q, gates, kv_s, kv_w, page_table, pool_cmp_t, pool_sel_t, win_prev_t, W):
    B, T, _ = q.shape
    pos0 = page_table.shape[1] * PAGE_SIZE
    n_blocks = -(-(pos0 + T) // NSA_BLOCK)
    ck, cv_t = _split_cmp(_nsa_compress_paged(pool_cmp_t, page_table, W))
    o_c, o_w, idx = _nsa_sample_cmp(q, ck, cv_t, kv_w, win_prev_t, pos0, n_blocks)
    idx = jnp.swapaxes(idx[..., :min(NSA_TOPN, n_blocks)], 1, 2)
    return _nsa_sample_sel(q, idx, page_table, pool_sel_t, kv_s, o_c, o_w, gates, W, pos0)


def _mem_kv_body(x_ref, g_ref, w_ref, gk_ref, avg_ref, o_ref):
    h = _rms_rows(x_ref[...], g_ref[...]).astype(BF16)
    kv = lax.dot_general(w_ref[...], h, (((1,), (1,)), ((), ())), preferred_element_type=F32)
    k = kv[:MEM_WIDTH]
    o_ref[:MEM_WIDTH, :] = k * lax.rsqrt(_seg_mean_sq_rows(k, avg_ref[...]) + NORM_EPS) * gk_ref[...]
    o_ref[MEM_WIDTH:, :] = kv[MEM_WIDTH:]


def _mem_kv(mem, g, w_kv, k_g):
    B, m, d = mem.shape
    consts = [g.reshape(1, d), w_kv.T.astype(BF16), jnp.tile(k_g, MEM_HEADS).reshape(MEM_WIDTH, 1),
              _seg_avg_matrix(MEM_WIDTH, 1.0 / HEAD_DIM)]
    return pl.pallas_call(
        _mem_kv_body,
        grid=(B,),
        in_specs=[pl.BlockSpec((None, m, d), lambda b: (b, 0, 0))] + [_const_spec(c.shape) for c in consts],
        out_specs=pl.BlockSpec((None, 2 * MEM_WIDTH, m), lambda b: (b, 0, 0)),
        out_shape=jax.ShapeDtypeStruct((B, 2 * MEM_WIDTH, m), F32),
        compiler_params=pltpu.CompilerParams(dimension_semantics=("parallel",)),
        name="mem_kv",
    )(mem, *consts)


def _out_mem_body(x_ref, orw_ref, oml_ref, onsa_ref, kv_ref, w1_ref, w2_ref, w3_ref, g_ref, wq_ref, gq_ref,
                  avg_ref, wo_ref, o_ref):
    N = HEAD_DIM
    x = (x_ref[...] + _dot(orw_ref[...], w1_ref[...]) + _dot(oml_ref[...], w2_ref[...])
         + _dot(onsa_ref[...], w3_ref[...]))
    h = _rms_rows(x, g_ref[...]).astype(BF16)
    q = jnp.dot(h, wq_ref[...], preferred_element_type=F32)
    q = q * lax.rsqrt(_seg_mean_sq(q, avg_ref[...]) + NORM_EPS) * gq_ref[...] * (N ** -0.5)
    heads = []
    for hd in range(MEM_HEADS):
        s = _dot(q[:, hd * N:(hd + 1) * N], kv_ref[hd * N:(hd + 1) * N, :])
        e = jnp.exp(s - jnp.max(s, axis=-1, keepdims=True))
        p = e / jnp.sum(e, axis=-1, keepdims=True)
        heads.append(_dot_nt(p, kv_ref[MEM_WIDTH + hd * N:MEM_WIDTH + (hd + 1) * N, :]))
    o_ref[...] = x + _dot(jnp.concatenate(heads, axis=1), wo_ref[...])


def _out_mem(x, o_rw, o_ml, o_nsa, kv_t, w_out, g, w_q, q_g, w_o):
    B, T, d = x.shape
    tm = _row_tile(T, 512)
    w1 = w_out[:RW_WIDTH].astype(BF16)
    w2 = w_out[RW_WIDTH:RW_WIDTH + ML_WIDTH].astype(BF16)
    w3 = w_out[RW_WIDTH + ML_WIDTH:].astype(BF16)
    consts = [w1, w2, w3, g.reshape(1, d), w_q.astype(BF16), jnp.tile(q_g, MEM_HEADS).reshape(1, MEM_WIDTH),
              _seg_avg_matrix(MEM_WIDTH, 1.0 / HEAD_DIM), w_o.astype(BF16)]
    tile = lambda w: pl.BlockSpec((None, tm, w), lambda b, i: (b, i, 0))
    return pl.pallas_call(
        _out_mem_body,
        grid=(B, T // tm),
        in_specs=[tile(d), tile(RW_WIDTH), tile(ML_WIDTH), tile(NSA_WIDTH),
                  pl.BlockSpec((None,) + kv_t.shape[1:], lambda b, i: (b, 0, 0))]
                 + [_const_spec(c.shape) for c in consts],
        out_specs=tile(d),
        out_shape=jax.ShapeDtypeStruct((B, T, d), F32),
        compiler_params=pltpu.CompilerParams(dimension_semantics=("parallel", "parallel"),
                                             vmem_limit_bytes=VMEM_LIMIT),
        name="out_mem",
    )(x, o_rw, o_ml, o_nsa, kv_t, *consts)


def _layer(x, W, st, page_table, mem):
    B, T, d = x.shape
    is_prompt = st is None
    g = W['norm_g']
    G = NSA_KV_HEADS
    x1 = _ffn(x.reshape(B * T, d), g[0], W['ffa_up'], W['ffa_down']).reshape(B, T, d)
    proj = _in_proj(x1, g[1], W['w_in_perm'], W['nsa_qk_g'], channel_major=is_prompt)
    p_rw, p_ml, gates, q = proj[:4]
    if is_prompt:
        zeros = lambda *s: jnp.zeros(s, F32)
        st = {'rw_shift': zeros(B, 1, RW_COLS), 'rw_S': zeros(B, RW_HEADS, HEAD_DIM, HEAD_DIM),
              'ml_conv': zeros(B, ML_CONV - 1, ML_WIDTH), 'ml_C': zeros(B, ML_HEADS, HEAD_DIM, HEAD_DIM),
              'ml_n': zeros(B, ML_HEADS, HEAD_DIM), 'ml_m': zeros(B, ML_HEADS)}
        mem_kv_t = _mem_kv(mem, g[3], W['mem_w_kv'], W['mem_qk_g'][1])
    else:
        mem_kv_t = _channel_major(st['mem_kv'])
    o_rw, rw_S = _rwkv(p_rw, st['rw_shift'], st['rw_S'], W)
    o_ml, ml_C, ml_n, ml_m = _mlstm(p_ml, gates, st['ml_conv'], st['ml_C'], st['ml_n'], st['ml_m'], W)
    if is_prompt:
        ks, kw, kvc_t, kvs_t, kvw_t = proj[4:]
        ck, cv_t = _split_cmp(_nsa_compress_dense(kvc_t, W))
        o_nsa = _nsa_prompt(q, gates, ck, cv_t, ks, kvs_t, kw, kvw_t, W)
        new_kv = {'nsa_cmp': _token_major(kvc_t, 2, G), 'nsa_sel': _token_major(kvs_t, 2, G),
                  'nsa_win': _token_major(kvw_t[:, :, T - min(NSA_WINDOW, T):], 2, G)}
    else:
        kv_c, kv_s, kv_w = proj[4:]
        win_prev_t = _channel_major(st['nsa_win'])
        wb = win_prev_t.shape[2]
        o_nsa = _nsa_sample_mixer(q, gates, kv_s, kv_w, page_table, st['nsa_cmp_t'], st['nsa_sel_t'], win_prev_t, W)
        win_t = jnp.concatenate([win_prev_t, jnp.swapaxes(kv_w, 1, 2)], axis=2)
        kv5 = lambda z: z.reshape(B, T, 2, G, HEAD_DIM)
        new_kv = {'nsa_cmp': kv5(kv_c), 'nsa_sel': kv5(kv_s),
                  'nsa_win': _token_major(win_t[:, :, wb + T - min(NSA_WINDOW, wb + T):], 2, G)}
    x2 = _out_mem(x1, o_rw, o_ml, o_nsa, mem_kv_t, W['w_out'], g[2], W['mem_w_q'], W['mem_qk_g'][0], W['mem_w_o'])
    x3 = _ffn(x2.reshape(B * T, d), g[4], W['ffb_up'], W['ffb_down']).reshape(B, T, d)
    qk_in = p_ml[:, :, :ML_WIDTH]
    conv_all = jnp.concatenate([st['ml_conv'], qk_in], axis=1) if T < ML_CONV - 1 else qk_in
    new = dict(new_kv)
    new.update({'rw_shift': p_rw[:, T - 1:], 'rw_S': rw_S, 'ml_conv': conv_all[:, conv_all.shape[1] - (ML_CONV - 1):],
                'ml_C': ml_C, 'ml_n': ml_n, 'ml_m': ml_m})
    if is_prompt:
        new['mem_kv'] = _token_major(mem_kv_t, 2, MEM_HEADS)
    return x3, new


def kernel(x_prompt, x_sample, cache_nsa_cmp, cache_nsa_sel, cache_nsa_win, cache_mem_kv, state_rwkv_shift, state_rwkv_S, state_mlstm_conv, state_mlstm_C, state_mlstm_n, state_mlstm_m, page_table, mem_prompt, norm_g, ffa_up, ffa_down, ffb_up, ffb_down, w_in, w_out, rw_mu, rw_w0, rw_w2, rw_a0, rw_a2, rw_g2, rw_kk, rw_ka, rw_rk, rw_gn_g, rw_gn_b, ml_conv_w, ml_conv_b, ml_wq, ml_wk, ml_gate_b, ml_norm_g, ml_skip, nsa_qk_g, nsa_wpos, nsa_wc, nsa_gate_b, mem_w_q, mem_w_kv, mem_qk_g, mem_w_o):
    params = dict(norm_g=norm_g, ffa_up=ffa_up, ffa_down=ffa_down, ffb_up=ffb_up, ffb_down=ffb_down, w_out=w_out,
                  rw_mu=rw_mu, rw_w0=rw_w0, rw_w2=rw_w2, rw_a0=rw_a0, rw_a2=rw_a2, rw_g2=rw_g2, rw_kk=rw_kk,
                  rw_ka=rw_ka, rw_rk=rw_rk, rw_gn_g=rw_gn_g, rw_gn_b=rw_gn_b, ml_conv_w=ml_conv_w,
                  ml_conv_b=ml_conv_b, ml_wq=ml_wq, ml_wk=ml_wk, ml_gate_b=ml_gate_b, ml_norm_g=ml_norm_g,
                  ml_skip=ml_skip, nsa_qk_g=nsa_qk_g, nsa_wpos=nsa_wpos, nsa_wc=nsa_wc, nsa_gate_b=nsa_gate_b,
                  mem_w_q=mem_w_q, mem_w_kv=mem_w_kv, mem_qk_g=mem_qk_g, mem_w_o=mem_w_o)
    depth = norm_g.shape[0]
    y_p, y_s = x_prompt, x_sample
    new_p, new_s = [], []
    n_phys = cache_nsa_cmp.shape[1]
    all_pages = lambda c: _channel_major(c.reshape((depth * n_phys,) + c.shape[2:]))
    pool_cmp_t, pool_sel_t = all_pages(cache_nsa_cmp), all_pages(cache_nsa_sel)
    for l in range(depth):
        W = {name: v[l] for name, v in params.items()}
        W['w_in_perm'] = _permute_w_in(w_in[l])
        st = {'nsa_cmp_t': pool_cmp_t, 'nsa_sel_t': pool_sel_t, 'nsa_win': cache_nsa_win[l],
              'mem_kv': cache_mem_kv[l], 'rw_shift': state_rwkv_shift[l], 'rw_S': state_rwkv_S[l],
              'ml_conv': state_mlstm_conv[l], 'ml_C': state_mlstm_C[l], 'ml_n': state_mlstm_n[l],
              'ml_m': state_mlstm_m[l]}
        y_p, sp = _layer(y_p, W, None, None, mem_prompt)
        y_s, ss = _layer(y_s, W, st, page_table + l * n_phys, None)
        new_p.append(sp)
        new_s.append(ss)
    P = lambda name: jnp.stack([d[name] for d in new_p])
    S = lambda name: jnp.stack([d[name] for d in new_s])
    return (y_p, y_s,
            P('nsa_cmp'), S('nsa_cmp'), P('nsa_sel'), S('nsa_sel'), P('nsa_win'), S('nsa_win'),
            P('mem_kv'),
            P('rw_shift'), S('rw_shift'), P('rw_S'), S('rw_S'),
            P('ml_conv'), S('ml_conv'), P('ml_C'), S('ml_C'), P('ml_n'), S('ml_n'), P('ml_m'), S('ml_m'))
```

```python
import functools

import jax
import jax.numpy as jnp
from jax import lax
from jax.experimental import pallas as pl
from jax.experimental.pallas import tpu as pltpu

F32 = jnp.float32
BF16 = jnp.bfloat16

HEAD_DIM = 64
RW_HEADS = 4
RW_WIDTH = RW_HEADS * HEAD_DIM
RW_W_LORA = 64
RW_A_LORA = 64
RW_G_LORA = 128
RW_COLS = 3 * RW_WIDTH + RW_W_LORA + RW_A_LORA + RW_G_LORA
RW_GN_EPS = 64e-5
ML_HEADS = 4
ML_WIDTH = ML_HEADS * HEAD_DIM
ML_CONV = 4
ML_CHUNK = 64
ML_COLS = 3 * ML_WIDTH + 2 * ML_HEADS
NSA_HEADS = 8
NSA_KV_HEADS = 2
NSA_GROUP = NSA_HEADS // NSA_KV_HEADS
NSA_WIDTH = NSA_HEADS * HEAD_DIM
NSA_KV = NSA_KV_HEADS * HEAD_DIM
NSA_BLOCK = 64
BLOCK_SHIFT = 6
NSA_TOPN = 8
NSA_WINDOW = 512
NSA_COLS = NSA_WIDTH + 6 * NSA_KV + 3 * NSA_HEADS
MEM_HEADS = 4
MEM_WIDTH = MEM_HEADS * HEAD_DIM
PAGE_SIZE = 128
NORM_EPS = 1e-6
BIG = 1e9
NEG = -1e30
LANES = 128
VMEM_LIMIT = 56 * 1024 * 1024

P_RW = 0
P_ML = P_RW + RW_COLS
P_Q = P_ML + 3 * ML_WIDTH
P_KVC = P_Q + NSA_WIDTH
P_KVS = P_KVC + 2 * NSA_KV
P_KVW = P_KVS + 2 * NSA_KV
P_GATES = P_KVW + 2 * NSA_KV
P_TOTAL = P_GATES + LANES


def _dot(a, b):
    return jnp.dot(a.astype(BF16), b.astype(BF16), preferred_element_type=F32)


def _dot_nt(a, b):
    return lax.dot_general(a.astype(BF16), b.astype(BF16), (((1,), (1,)), ((), ())),
                           preferred_element_type=F32)


def _dot_tn(a, b):
    return lax.dot_general(a.astype(BF16), b.astype(BF16), (((0,), (0,)), ((), ())),
                           preferred_element_type=F32)


def _split2(x):
    hi = x.astype(BF16)
    lo = (x - hi.astype(F32)).astype(BF16)
    return hi, lo


def _split3(x):
    hi = x.astype(BF16)
    r = x - hi.astype(F32)
    mid = r.astype(BF16)
    lo = (r - mid.astype(F32)).astype(BF16)
    return hi, mid, lo


def _dot_sel(sel, x):
    hi, mid, lo = _split3(x)
    s = sel.astype(BF16)
    d = lambda t: jnp.dot(s, t, preferred_element_type=F32)
    return d(hi) + d(mid) + d(lo)


def _dot_x_sel(x, sel):
    hi, mid, lo = _split3(x)
    s = sel.astype(BF16)
    d = lambda t: jnp.dot(t, s, preferred_element_type=F32)
    return d(hi) + d(mid) + d(lo)


def _dot_hi(a, b):
    ah, al = _split2(a)
    bh, bl = _split2(b)
    d = lambda u, v: jnp.dot(u, v, preferred_element_type=F32)
    return d(ah, bh) + d(al, bh) + d(ah, bl)


def _dot_nt_hi(a, b):
    ah, al = _split2(a)
    bh, bl = _split2(b)
    d = lambda u, v: lax.dot_general(u, v, (((1,), (1,)), ((), ())), preferred_element_type=F32)
    return d(ah, bh) + d(al, bh) + d(ah, bl)


def _seg_mean_sq(x, seg_avg):
    hi, lo = _split2(x * x)
    d = lambda t: jnp.dot(t, seg_avg, preferred_element_type=F32)
    return d(hi) + d(lo)


def _seg_mean_sq_rows(x, seg_avg):
    hi, lo = _split2(x * x)
    d = lambda t: jnp.dot(seg_avg, t, preferred_element_type=F32)
    return d(hi) + d(lo)


def _rms_rows(x, g):
    return x * lax.rsqrt(jnp.mean(x * x, axis=-1, keepdims=True) + NORM_EPS) * g


def _sigmoid(x):
    return 1.0 / (1.0 + jnp.exp(-x))


def _softplus(x):
    return jnp.maximum(x, 0.0) + jnp.log(1.0 + jnp.exp(-jnp.abs(x)))


def _row_tile(n, target):
    t = min(n, target)
    while n % t:
        t //= 2
    return t


def _const_spec(shape):
    nd = len(shape)
    return pl.BlockSpec(shape, lambda *_: (0,) * nd, pipeline_mode=pl.Buffered(1))


def _seg_avg_matrix(width, scale):
    i = jnp.arange(width) // HEAD_DIM
    return (jnp.where(i[:, None] == i[None, :], scale, 0.0)).astype(BF16)


def _channel_major(x):
    b, t = x.shape[:2]
    return jnp.transpose(x, (0, 2, 3, 4, 1)).reshape(b, -1, t)


def _token_major(x, c, g):
    b, w, t = x.shape
    return jnp.transpose(x.reshape(b, c, g, w // (c * g), t), (0, 4, 1, 2, 3))


def _ffn_body(x_ref, g_ref, wg_ref, wu_ref, wd_ref, o_ref, *, f_chunk):
    x = x_ref[...]
    h = _rms_rows(x, g_ref[...]).astype(BF16)
    d_ff = wg_ref.shape[1]
    acc = jnp.zeros_like(x)
    for c in range(d_ff // f_chunk):
        sl = slice(c * f_chunk, (c + 1) * f_chunk)
        gate = jnp.dot(h, wg_ref[:, sl], preferred_element_type=F32)
        up = jnp.dot(h, wu_ref[:, sl], preferred_element_type=F32)
        act = (gate * _sigmoid(gate) * up).astype(BF16)
        acc = acc + jnp.dot(act, wd_ref[sl, :], preferred_element_type=F32)
    o_ref[...] = x + 0.5 * acc


def _ffn(x, g, w_up, w_down):
    n, d = x.shape
    d_ff = w_down.shape[0]
    tm = _row_tile(n, 512)
    f_chunk = d_ff // 2 if (d_ff // 2) % LANES == 0 else d_ff
    wg = w_up[:, :d_ff].astype(BF16)
    wu = w_up[:, d_ff:].astype(BF16)
    wd = w_down.astype(BF16)
    row = pl.BlockSpec((tm, d), lambda i: (i, 0))
    return pl.pallas_call(
        functools.partial(_ffn_body, f_chunk=f_chunk),
        grid=(n // tm,),
        in_specs=[row, _const_spec((1, d)), _const_spec((d, d_ff)), _const_spec((d, d_ff)),
                  _const_spec((d_ff, d))],
        out_specs=row,
        out_shape=jax.ShapeDtypeStruct((n, d), F32),
        compiler_params=pltpu.CompilerParams(dimension_semantics=("parallel",),
                                             vmem_limit_bytes=VMEM_LIMIT),
        name="ffn",
    )(x, g.reshape(1, d), wg, wu, wd)


def _in_proj_body(x_ref, g_ref, w_ref, gq_ref, gks_ref, gkw_ref, avg_ref, rw_ref, ml_ref, gates_ref, q_ref,
                  *kv_refs, channel_major):
    h = _rms_rows(x_ref[...], g_ref[...]).astype(BF16)
    p = jnp.dot(h, w_ref[...], preferred_element_type=F32)
    rw_ref[...] = p[:, P_RW:P_ML]
    ml_ref[...] = p[:, P_ML:P_Q]
    gates_ref[...] = p[:, P_GATES:P_TOTAL]
    avg = avg_ref[...]

    def head_norm(z, g):
        w = z.shape[1]
        return z * lax.rsqrt(_seg_mean_sq(z, avg[:w, :w]) + NORM_EPS) * g

    q_ref[...] = head_norm(p[:, P_Q:P_KVC], gq_ref[...])
    kvc = p[:, P_KVC:P_KVS]
    ks = head_norm(p[:, P_KVS:P_KVS + NSA_KV], gks_ref[...])
    vs = p[:, P_KVS + NSA_KV:P_KVW]
    kw = head_norm(p[:, P_KVW:P_KVW + NSA_KV], gkw_ref[...])
    vw = p[:, P_KVW + NSA_KV:P_GATES]
    if channel_major:
        ks_ref, kw_ref, kvc_t_ref, kvs_t_ref, kvw_t_ref = kv_refs
        ks_ref[...] = ks.astype(BF16)
        kw_ref[...] = kw.astype(BF16)
        kvc_t_ref[...] = kvc.T
        kvs_t_ref[:NSA_KV, :] = ks.T
        kvs_t_ref[NSA_KV:, :] = vs.T
        kvw_t_ref[:NSA_KV, :] = kw.T
        kvw_t_ref[NSA_KV:, :] = vw.T
    else:
        kvc_ref, kvs_ref, kvw_ref = kv_refs
        kvc_ref[...] = kvc
        kvs_ref[:, :NSA_KV] = ks
        kvs_ref[:, NSA_KV:] = vs
        kvw_ref[:, :NSA_KV] = kw
        kvw_ref[:, NSA_KV:] = vw


def _permute_w_in(w_in):
    d = w_in.shape[0]
    o_ml = RW_COLS
    o_nsa = RW_COLS + ML_COLS
    ml_gates = w_in[:, o_ml + 3 * ML_WIDTH:o_nsa]
    nsa_gates = w_in[:, o_nsa + NSA_WIDTH + 6 * NSA_KV:]
    pad = jnp.zeros((d, LANES - 2 * ML_HEADS - 3 * NSA_HEADS), w_in.dtype)
    return jnp.concatenate([w_in[:, :o_ml + 3 * ML_WIDTH], w_in[:, o_nsa:o_nsa + NSA_WIDTH + 6 * NSA_KV],
                            ml_gates, nsa_gates, pad], axis=1).astype(BF16)


def _in_proj(x, g, w_perm, qk_g, channel_major):
    B, T, d = x.shape
    n = B * T
    tm = _row_tile(T if channel_major else n, 512)
    tpb = T // tm
    row = lambda w: pl.BlockSpec((tm, w), lambda i: (i, 0))
    widths = [RW_COLS, 3 * ML_WIDTH, LANES, NSA_WIDTH]
    out_specs = [row(w) for w in widths]
    out_shape = [jax.ShapeDtypeStruct((n, w), F32) for w in widths]
    w_kv = 2 * NSA_KV
    if channel_major:
        out_specs += [row(NSA_KV)] * 2 + [pl.BlockSpec((None, w_kv, tm), lambda i: (i // tpb, 0, i % tpb))] * 3
        out_shape += [jax.ShapeDtypeStruct((n, NSA_KV), BF16)] * 2 + [jax.ShapeDtypeStruct((B, w_kv, T), F32)] * 3
    else:
        out_specs += [row(w_kv)] * 3
        out_shape += [jax.ShapeDtypeStruct((n, w_kv), F32)] * 3
    gq = jnp.tile(qk_g[0], NSA_HEADS).reshape(1, NSA_WIDTH)
    gks = jnp.tile(qk_g[2], NSA_KV_HEADS).reshape(1, NSA_KV)
    gkw = jnp.tile(qk_g[3], NSA_KV_HEADS).reshape(1, NSA_KV)
    avg = _seg_avg_matrix(NSA_WIDTH, 1.0 / HEAD_DIM)
    outs = pl.pallas_call(
        functools.partial(_in_proj_body, channel_major=channel_major),
        grid=(n // tm,),
        in_specs=[row(d), _const_spec((1, d)), _const_spec((d, P_TOTAL)), _const_spec((1, NSA_WIDTH)),
                  _const_spec((1, NSA_KV)), _const_spec((1, NSA_KV)), _const_spec((NSA_WIDTH, NSA_WIDTH))],
        out_specs=out_specs,
        out_shape=out_shape,
        compiler_params=pltpu.CompilerParams(dimension_semantics=("parallel",),
                                             vmem_limit_bytes=VMEM_LIMIT),
        name="in_proj",
    )(x.reshape(n, d), g.reshape(1, d), w_perm, gq, gks, gkw, avg)
    return [o.reshape(B, T, -1) if o.shape[0] == n else o for o in outs]


def _rwkv_chunk_inputs(p, prev_row, mu_ref, w0_ref, lw_ref, a0_ref, g2_ref, kks_ref, ka_ref, rk_ref, tri_ref,
                       seg_ref):
    C = p.shape[0]
    rows = lax.broadcasted_iota(jnp.int32, (C, 1), 0)
    prev = jnp.where(rows == 0, prev_row, pltpu.roll(p, 1, axis=0))
    xm = p + (prev - p) * mu_ref[...]
    r = xm[:, 0:RW_WIDTH]
    k = xm[:, RW_WIDTH:2 * RW_WIDTH]
    v = xm[:, 2 * RW_WIDTH:3 * RW_WIDTH]
    lin = xm[:, 3 * RW_WIDTH:3 * RW_WIDTH + RW_W_LORA + RW_A_LORA]
    lane = lax.broadcasted_iota(jnp.int32, lin.shape, 1)
    lora = _dot_hi(jnp.where(lane < RW_W_LORA, jnp.tanh(lin), lin), lw_ref[...])
    w = -_softplus(-(w0_ref[...] + lora[:, :RW_WIDTH])) - 0.5
    a = _sigmoid(a0_ref[...] + lora[:, RW_WIDTH:])
    g = _dot(_sigmoid(xm[:, 3 * RW_WIDTH + RW_W_LORA + RW_A_LORA:]), g2_ref[...])
    kk = k * kks_ref[...]
    hi, lo = _split2(kk * kk)
    seg = seg_ref[...]
    ss = jnp.dot(hi, seg, preferred_element_type=F32) + jnp.dot(lo, seg, preferred_element_type=F32)
    kk = kk / jnp.maximum(jnp.sqrt(ss), 1e-12)
    k = k * (1.0 + (a - 1.0) * ka_ref[...])
    logdec = -jnp.exp(w)
    tri = tri_ref[...]
    G = _dot_sel(tri, logdec)
    g_end = G[C - 1:C, :]
    e_g = jnp.exp(G)
    e_gi = jnp.exp(-G)
    kkd = kk * jnp.exp(G - logdec)
    rd = r * e_g
    b = kk * a
    bi = b * e_gi
    ki = k * e_gi
    e_end = jnp.exp(g_end - G)
    bi2 = b * e_end
    ki2 = k * e_end
    dec_end = jnp.exp(g_end)
    rkk = r * k * rk_ref[...]
    return dict(v=v, g=g, kkd=kkd, rd=rd, bi=bi, ki=ki, bi2=bi2, ki2=ki2, dec_end=dec_end, rkk=rkk)


def _rwkv_body(p_ref, shift_ref, s0_ref, mu_ref, w0_ref, lw_ref, a0_ref, g2_ref, kks_ref, ka_ref,
               rk_ref, gng_ref, gnb_ref, tri_ref, seg_ref, o_ref, st_ref, carry_ref, s_ref, *, chunk, nb):
    C = chunk
    N = HEAD_DIM

    @pl.when(pl.program_id(1) == 0)
    def _():
        carry_ref[...] = shift_ref[...]
        s_ref[...] = s0_ref[...]

    pre = []
    for bi in range(nb):
        p = p_ref[bi]
        pre.append(_rwkv_chunk_inputs(p, carry_ref[bi], mu_ref, w0_ref, lw_ref, a0_ref, g2_ref, kks_ref, ka_ref,
                                      rk_ref, tri_ref, seg_ref))
        carry_ref[bi] = p[C - 1:C, :]

    ti = lax.broadcasted_iota(jnp.int32, (C, C), 0)
    si = lax.broadcasted_iota(jnp.int32, (C, C), 1)
    strict = si < ti
    eye = (si == ti).astype(F32)
    s2 = lax.broadcasted_iota(jnp.int32, (C, 2 * C), 1)
    incl2 = jnp.where(s2 < C, s2, s2 - C) <= lax.broadcasted_iota(jnp.int32, (C, 2 * C), 0)

    units = [(bi, h) for bi in range(nb) for h in range(RW_HEADS)]
    col = lambda name, u: pre[u[0]][name][:, u[1] * N:(u[1] + 1) * N]
    lhs = [jnp.concatenate([col('kkd', u), col('rd', u)], axis=0) for u in units]
    rhs = [jnp.concatenate([col('bi', u), col('ki', u)], axis=0) for u in units]
    vh = [col('v', u) for u in units]
    s0 = [s_ref[bi, h] for bi, h in units]
    m4 = [_dot_nt(a, b) for a, b in zip(lhs, rhs)]
    ks = [_dot_nt(a, s) for a, s in zip(lhs, s0)]
    pw = [jnp.where(strict, -m[:C, :C], 0.0) for m in m4]
    lk = [jnp.where(strict, m[:C, C:], 0.0) for m in m4]
    mbk = [jnp.where(incl2, m[C:, :], 0.0) for m in m4]
    lkv = [_dot(a, b) for a, b in zip(lk, vh)]
    t_inv = [eye + n for n in pw]
    span = 2
    while span < C:
        pw = [_dot(x, x) for x in pw]
        t_inv = [t + _dot(t, x) for t, x in zip(t_inv, pw)]
        span *= 2
    u_ = [_dot(t, -(k_[:C] + l)) for t, k_, l in zip(t_inv, ks, lkv)]
    uv = [jnp.concatenate([a, b], axis=0) for a, b in zip(u_, vh)]
    y = [k_[C:] + _dot(m, x) for k_, m, x in zip(ks, mbk, uv)]
    s_new = [s * col('dec_end', u) + _dot_tn(x, jnp.concatenate([col('bi2', u), col('ki2', u)], axis=0))
             for s, x, u in zip(s0, uv, units)]
    for (bi, h), s, yy, v_ in zip(units, s_new, y, vh):
        sl = slice(h * N, (h + 1) * N)
        s_ref[bi, h] = s
        mean = jnp.mean(yy, axis=-1, keepdims=True)
        var = jnp.mean(jnp.square(yy - mean), axis=-1, keepdims=True)
        yn = (yy - mean) * lax.rsqrt(var + RW_GN_EPS) * gng_ref[:, sl] + gnb_ref[:, sl]
        bonus = jnp.sum(pre[bi]['rkk'][:, sl], axis=-1, keepdims=True) * v_
        o_ref[bi, :, sl] = (yn + bonus) * pre[bi]['g'][:, sl]

    @pl.when(pl.program_id(1) == pl.num_programs(1) - 1)
    def _():
        st_ref[...] = s_ref[...]


def _rwkv(p_rw, shift_prev, s0, W):
    B, T, _ = p_rw.shape
    C = ML_CHUNK if T % ML_CHUNK == 0 else T
    nb = 2 if B % 2 == 0 else 1
    row = lambda v: v.reshape(1, -1).astype(F32)
    z = jnp.zeros((RW_W_LORA, RW_WIDTH), F32)
    lw = jnp.concatenate([jnp.concatenate([W['rw_w2'], z], axis=1),
                          jnp.concatenate([z, W['rw_a2']], axis=1)], axis=0)
    tri = (jnp.arange(C)[:, None] >= jnp.arange(C)[None, :]).astype(BF16)
    seg = _seg_avg_matrix(RW_WIDTH, 1.0)
    consts = [row(W['rw_mu']), row(W['rw_w0']), lw, row(W['rw_a0']), W['rw_g2'].astype(BF16), row(W['rw_kk']),
              row(W['rw_ka']), row(W['rw_rk']), row(W['rw_gn_g']), row(W['rw_gn_b']), tri, seg]
    out, s_t = pl.pallas_call(
        functools.partial(_rwkv_body, chunk=C, nb=nb),
        grid=(B // nb, T // C),
        in_specs=[pl.BlockSpec((nb, C, RW_COLS), lambda b, c: (b, c, 0)),
                  pl.BlockSpec((nb, 1, RW_COLS), lambda b, c: (b, 0, 0)),
                  pl.BlockSpec((nb, RW_HEADS, HEAD_DIM, HEAD_DIM), lambda b, c: (b, 0, 0, 0))]
                 + [_const_spec(c.shape) for c in consts],
        out_specs=[pl.BlockSpec((nb, C, RW_WIDTH), lambda b, c: (b, c, 0)),
                   pl.BlockSpec((nb, RW_HEADS, HEAD_DIM, HEAD_DIM), lambda b, c: (b, 0, 0, 0))],
        out_shape=[jax.ShapeDtypeStruct((B, T, RW_WIDTH), F32),
                   jax.ShapeDtypeStruct((B, RW_HEADS, HEAD_DIM, HEAD_DIM), F32)],
        scratch_shapes=[pltpu.VMEM((nb, 1, RW_COLS), F32), pltpu.VMEM((nb, RW_HEADS, HEAD_DIM, HEAD_DIM), F32)],
        compiler_params=pltpu.CompilerParams(dimension_semantics=("parallel", "arbitrary"),
                                             vmem_limit_bytes=VMEM_LIMIT),
        name="rwkv7",
    )(p_rw, shift_prev, s0, *consts)
    return out, s_t


def _mlstm_body(p_ref, gc_ref, gt_ref, convp_ref, c0_ref, n0_ref, m0_ref, cw_ref, cb_ref, wq_ref, wk_ref,
                bias_r_ref, bias_c_ref, ng_ref, skip_ref, tril_ref, triu_ref,
                o_ref, ct_ref, nt_ref, mt_ref, ext_ref, c_ref, n_ref, m_ref, *, chunk, nb):
    L = chunk
    N = HEAD_DIM
    H = ML_HEADS

    @pl.when(pl.program_id(1) == 0)
    def _():
        ext_ref[:, 0:8, :] = convp_ref[...]
        c_ref[...] = c0_ref[...]
        n_ref[...] = n0_ref[...]
        m_ref[...] = m0_ref[...]

    cw = cw_ref[...]
    pre = []
    for bi in range(nb):
        p = p_ref[bi]
        x = p[:, 0:ML_WIDTH]
        ext_ref[bi, 8:8 + L, :] = x
        conv = (cb_ref[...] + cw[3:4, :] * x + cw[2:3, :] * ext_ref[bi, 7:7 + L, :]
                + cw[1:2, :] * ext_ref[bi, 6:6 + L, :] + cw[0:1, :] * ext_ref[bi, 5:5 + L, :])
        tail = ext_ref[bi, L:L + 8, :]
        ext_ref[bi, 0:8, :] = tail
        ca = conv * _sigmoid(conv)
        gcb = gc_ref[bi] + bias_r_ref[...]
        gtb = gt_ref[bi, 0] + bias_c_ref[...][:, 0:1]
        pre.append(dict(
            v=p[:, ML_WIDTH:2 * ML_WIDTH], o_pre=p[:, 2 * ML_WIDTH:3 * ML_WIDTH], ca=ca,
            q=_dot(ca, wq_ref[...]), k=_dot(ca, wk_ref[...]) * (N ** -0.5), gcb=gcb, gtb=gtb,
            bcum_c=_dot_sel(tril_ref[...], -_softplus(-gcb)),
            bcum_r=_dot_x_sel(-_softplus(-gtb), triu_ref[...])))

    ti = lax.broadcasted_iota(jnp.int32, (L, L), 0)
    si = lax.broadcasted_iota(jnp.int32, (L, L), 1)
    causal = si <= ti

    units = [(bi, h) for bi in range(nb) for h in range(H)]
    col = lambda name, u: pre[u[0]][name][:, u[1] * N:(u[1] + 1) * N]
    qh = [col('q', u) for u in units]
    kh = [col('k', u) for u in units]
    vh = [col('v', u) for u in units]
    cs = [c_ref[bi, h] for bi, h in units]
    ns = [n_ref[bi, h:h + 1, :] for bi, h in units]
    m_prev = [m_ref[bi, h:h + 1, 0:1] for bi, h in units]
    b_c = [pre[bi]['bcum_c'][:, H + h:H + h + 1] for bi, h in units]
    i_c = [pre[bi]['gcb'][:, h:h + 1] for bi, h in units]
    b_r = [pre[bi]['bcum_r'][H + h:H + h + 1, :] for bi, h in units]
    i_r = [pre[bi]['gtb'][h:h + 1, :] for bi, h in units]
    qk = [_dot_nt(a, b) for a, b in zip(qh, kh)]
    qc = [_dot_nt(a, c) for a, c in zip(qh, cs)]
    dmat = [jnp.where(causal, bc - br + ir, -jnp.inf) for bc, br, ir in zip(b_c, b_r, i_r)]
    inter = [bc + mp for bc, mp in zip(b_c, m_prev)]
    m_t = [jnp.maximum(it, jnp.max(dm, axis=-1, keepdims=True)) for it, dm in zip(inter, dmat)]
    a = [x * jnp.exp(dm - mt) for x, dm, mt in zip(qk, dmat, m_t)]
    av = [_dot(x, v_) for x, v_ in zip(a, vh)]
    b_end = [bc[L - 1:L, :] for bc in b_c]
    m_new = [jnp.maximum(be + mp, jnp.max(be - br + ir, axis=-1, keepdims=True))
             for be, mp, br, ir in zip(b_end, m_prev, b_r, i_r)]
    dec = [jnp.exp(be + mp - mn) for be, mp, mn in zip(b_end, m_prev, m_new)]
    wg = [jnp.exp(be - bc + ic - mn) for be, bc, ic, mn in zip(b_end, b_c, i_c, m_new)]
    c_new = [d * c + _dot_tn(w * v_, k_) for d, c, w, v_, k_ in zip(dec, cs, wg, vh, kh)]
    for idx, (bi, h) in enumerate(units):
        sl = slice(h * N, (h + 1) * N)
        sc = jnp.exp(inter[idx] - m_t[idx])
        num = sc * qc[idx] + av[idx]
        den = (sc * jnp.sum(qh[idx] * ns[idx], axis=-1, keepdims=True)
               + jnp.sum(a[idx], axis=-1, keepdims=True))
        hh = num / jnp.maximum(jnp.abs(den), jnp.exp(-m_t[idx]))
        c_ref[bi, h] = c_new[idx]
        n_ref[bi, h:h + 1, :] = dec[idx] * ns[idx] + jnp.sum(wg[idx] * kh[idx], axis=0, keepdims=True)
        m_ref[bi, h:h + 1, :] = jnp.broadcast_to(m_new[idx], (1, LANES))
        hn = hh * lax.rsqrt(jnp.mean(hh * hh, axis=-1, keepdims=True) + NORM_EPS)
        o_ref[bi, :, sl] = ((hn * ng_ref[:, sl] + skip_ref[:, sl] * pre[bi]['ca'][:, sl])
                            * _sigmoid(pre[bi]['o_pre'][:, sl]))

    @pl.when(pl.program_id(1) == pl.num_programs(1) - 1)
    def _():
        ct_ref[...] = c_ref[...]
        nt_ref[...] = n_ref[...]
        mt_ref[...] = m_ref[...]


def _block_diag(w):
    H, N, _ = w.shape
    eye = jnp.eye(H, dtype=w.dtype)
    return (eye[:, None, :, None] * w[:, :, None, :]).reshape(H * N, H * N)


def _mlstm(p_ml, gates, conv_prev, c0, n0, m0, W):
    B, T, _ = p_ml.shape
    H, N = ML_HEADS, HEAD_DIM
    L = ML_CHUNK if T % ML_CHUNK == 0 else T
    nc = T // L
    gt = jnp.swapaxes(gates[:, :, :2 * H].reshape(B, nc, L, 2 * H), 2, 3)
    convp = jnp.concatenate([jnp.zeros((B, 8 - (ML_CONV - 1), ML_WIDTH), F32), conv_prev], axis=1)
    m0b = jnp.broadcast_to(jnp.pad(m0, ((0, 0), (0, 8 - H)))[:, :, None], (B, 8, LANES))
    bias = W['ml_gate_b'].reshape(2 * H)
    bias_r = jnp.pad(bias, (0, LANES - 2 * H)).reshape(1, LANES)
    bias_c = jnp.broadcast_to(bias[:, None], (2 * H, LANES))
    tril = (jnp.arange(L)[:, None] >= jnp.arange(L)[None, :]).astype(BF16)
    row = lambda z: z.reshape(1, -1).astype(F32)
    consts = [W['ml_conv_w'], row(W['ml_conv_b']), _block_diag(W['ml_wq']).astype(BF16),
              _block_diag(W['ml_wk']).astype(BF16), bias_r, bias_c, row(W['ml_norm_g']), row(W['ml_skip']),
              tril, tril.T]
    nb = 2 if B % 2 == 0 else 1
    per_b = lambda *shape: pl.BlockSpec((nb,) + shape, lambda b, c: (b,) + (0,) * len(shape))
    out, c_t, n_t, m_t = pl.pallas_call(
        functools.partial(_mlstm_body, chunk=L, nb=nb),
        grid=(B // nb, nc),
        in_specs=[pl.BlockSpec((nb, L, 3 * ML_WIDTH), lambda b, c: (b, c, 0)),
                  pl.BlockSpec((nb, L, LANES), lambda b, c: (b, c, 0)),
                  pl.BlockSpec((nb, 1, 2 * H, L), lambda b, c: (b, c, 0, 0)),
                  per_b(8, ML_WIDTH), per_b(H, N, N), per_b(H, N), per_b(8, LANES)]
                 + [_const_spec(c.shape) for c in consts],
        out_specs=[pl.BlockSpec((nb, L, ML_WIDTH), lambda b, c: (b, c, 0)),
                   per_b(H, N, N), per_b(H, N), per_b(8, LANES)],
        out_shape=[jax.ShapeDtypeStruct((B, T, ML_WIDTH), F32), jax.ShapeDtypeStruct((B, H, N, N), F32),
                   jax.ShapeDtypeStruct((B, H, N), F32), jax.ShapeDtypeStruct((B, 8, LANES), F32)],
        scratch_shapes=[pltpu.VMEM((nb, L + 8, ML_WIDTH), F32), pltpu.VMEM((nb, H, N, N), F32),
                        pltpu.VMEM((nb, H, N), F32), pltpu.VMEM((nb, 8, LANES), F32)],
        compiler_params=pltpu.CompilerParams(dimension_semantics=("parallel", "arbitrary"),
                                             vmem_limit_bytes=VMEM_LIMIT),
        name="mlstm",
    )(p_ml, gates, gt, convp, c0, n0, m0b, *consts)
    return out, c_t, n_t, m_t[:, :H, 0]


def _compress_body(*refs, n_parts, paged):
    if paged:
        refs = refs[1:]
    part_refs = refs[:n_parts]
    wpos_ref, sel_ref, wc_ref, gk_ref, avg_ref, o_ref = refs[n_parts:]
    wpos = wpos_ref[...]
    parts = []
    for r in part_refs:
        x = r[...]
        parts += [x[:, c * LANES:(c + 1) * LANES] * wpos for c in range(x.shape[1] // LANES)]
    hi, lo = _split2(jnp.concatenate(parts, axis=1))
    sel = sel_ref[...]
    c = jnp.dot(hi, sel, preferred_element_type=F32) + jnp.dot(lo, sel, preferred_element_type=F32)
    c = _dot_hi(wc_ref[...], c)
    ck = c[:NSA_KV]
    o_ref[:NSA_KV, :] = ck * lax.rsqrt(_seg_mean_sq_rows(ck, avg_ref[...]) + NORM_EPS) * gk_ref[...]
    o_ref[NSA_KV:, :] = c[NSA_KV:]


def _compress_consts(W, n_rows):
    w = 2 * NSA_KV
    wpos = W['nsa_wpos'].reshape(NSA_BLOCK, w).T
    wc_t = _block_diag(W['nsa_wc'].reshape(2 * NSA_KV_HEADS, HEAD_DIM, HEAD_DIM)).T
    gk = jnp.tile(W['nsa_qk_g'][1], NSA_KV_HEADS).reshape(NSA_KV, 1)
    sel = (jnp.arange(n_rows)[:, None] // NSA_BLOCK == jnp.arange(n_rows // NSA_BLOCK)[None, :]).astype(BF16)
    return [jnp.concatenate([wpos, wpos], axis=1), sel, wc_t, gk, _seg_avg_matrix(NSA_KV, 1.0 / HEAD_DIM)]


def _nsa_compress_dense(kvc_t, W):
    B, w, T = kvc_t.shape
    consts = _compress_consts(W, T)
    return pl.pallas_call(
        functools.partial(_compress_body, n_parts=1, paged=False),
        grid=(B,),
        in_specs=[pl.BlockSpec((None, w, T), lambda b: (b, 0, 0))] + [_const_spec(c.shape) for c in consts],
        out_specs=pl.BlockSpec((None, w, T // NSA_BLOCK), lambda b: (b, 0, 0)),
        out_shape=jax.ShapeDtypeStruct((B, w, T // NSA_BLOCK), F32),
        compiler_params=pltpu.CompilerParams(dimension_semantics=("parallel",), vmem_limit_bytes=VMEM_LIMIT),
        name="nsa_compress_dense",
    )(kvc_t, *consts)


def _nsa_compress_paged(pool_t, page_table, W):
    B, n_pages = page_table.shape
    w = pool_t.shape[1]
    pages = min(64, n_pages)
    bpp = PAGE_SIZE // NSA_BLOCK
    consts = _compress_consts(W, pages * PAGE_SIZE)

    def page_spec(k):
        return pl.BlockSpec((None, w, PAGE_SIZE), lambda b, i, pt: (pt[b * n_pages + i * pages + k], 0, 0))

    const = lambda c: pl.BlockSpec(c.shape, lambda b, i, pt: (0,) * c.ndim)
    return pl.pallas_call(
        functools.partial(_compress_body, n_parts=pages, paged=True),
        grid_spec=pltpu.PrefetchScalarGridSpec(
            num_scalar_prefetch=1,
            grid=(B, n_pages // pages),
            in_specs=[page_spec(k) for k in range(pages)] + [const(c) for c in consts],
            out_specs=pl.BlockSpec((None, w, pages * bpp), lambda b, i, pt: (b, 0, i)),
        ),
        out_shape=jax.ShapeDtypeStruct((B, w, n_pages * bpp), F32),
        compiler_params=pltpu.CompilerParams(dimension_semantics=("parallel", "parallel"),
                                             vmem_limit_bytes=VMEM_LIMIT),
        name="nsa_compress_paged",
    )(page_table.reshape(-1), *([pool_t] * pages), *consts)


def _split_cmp(cmp_t):
    return jnp.swapaxes(cmp_t[:, :NSA_KV], 1, 2), cmp_t[:, NSA_KV:]


def _gate_bias_row(gate_b):
    return jnp.pad(gate_b, (2 * ML_HEADS, LANES - 2 * ML_HEADS - 3 * NSA_HEADS)).reshape(1, LANES)


def _gate_lane(g, r):
    return 2 * ML_HEADS + (g * NSA_GROUP + r) * 3


def _attend_t(k_tile, v_ext, q_all, bias, carry):
    m, acc = carry
    s = jnp.dot(k_tile, q_all, preferred_element_type=F32) + jnp.concatenate([bias] * NSA_GROUP, axis=1)
    m_new = jnp.maximum(m, jnp.max(s, axis=0, keepdims=True))
    p = jnp.exp(s - m_new)
    return m_new, jnp.exp(m - m_new) * acc + jnp.dot(v_ext, p.astype(BF16), preferred_element_type=F32)


def _attend_t_init(width):
    return jnp.full((1, width), NEG, F32), jnp.zeros((HEAD_DIM + 8, width), F32)


def _attend_t_finish(carry):
    _, acc = carry
    return acc[:HEAD_DIM] / jnp.maximum(acc[HEAD_DIM:HEAD_DIM + 1], 1e-30)


def _nsa_prompt_body(q_ref, gates_ref, gb_ref, ck_ref, cv_ref, ks_ref, vs_ref, kw_ref, vw_ref, o_ref,
                     sel_ref, ot_ref, *, tq, tk, n_blocks):
    R, N, G = NSA_GROUP, HEAD_DIM, NSA_KV_HEADS
    W = R * tq
    i = pl.program_id(1)
    t0 = i * tq
    qpos = t0 + lax.broadcasted_iota(jnp.int32, (1, tq), 1)
    q_t = q_ref[...].T * (N ** -0.5)
    gt_t = _sigmoid((gates_ref[...] + gb_ref[...]).T)
    ck = ck_ref[...]
    nc = ck.shape[0]
    zeros = jnp.zeros((N, W), F32)
    q_f = []
    for g in range(G):
        hs = jnp.concatenate([q_t[(g * R + r) * N:(g * R + r + 1) * N] for r in range(R)], axis=1)
        q_f.append(jnp.concatenate([hs, zeros] if g == 0 else [zeros, hs], axis=0))
    q_b = [x.astype(BF16) for x in q_f]

    blk_w = lax.broadcasted_iota(jnp.int32, (nc, W), 0)
    vis = (blk_w + 1) * NSA_BLOCK - 1 <= jnp.concatenate([qpos] * R, axis=1)
    o_c, imp = [], []
    for g in range(G):
        s = jnp.where(vis, _dot_hi(ck, q_f[g]), -jnp.inf)
        m = jnp.max(s, axis=0, keepdims=True)
        e = jnp.exp(s - jnp.where(m == -jnp.inf, 0.0, m))
        p = e / jnp.maximum(jnp.sum(e, axis=0, keepdims=True), 1e-30)
        o_c.append(_dot(cv_ref[g * N:(g + 1) * N, :], p))
        pg = p[:, 0:tq]
        for r in range(1, R):
            pg = pg + p[:, r * tq:(r + 1) * tq]
        imp.append(pg)
    imp = jnp.concatenate(imp, axis=1)

    blk_r = lax.broadcasted_iota(jnp.int32, (nc, G * tq), 0)
    cur = jnp.right_shift(jnp.concatenate([qpos] * G, axis=1), BLOCK_SHIFT)
    forced = (blk_r == 0) | (blk_r == cur) | (blk_r == cur - 1)
    score = jnp.where(forced, BIG, jnp.where(blk_r <= cur, imp, -BIG))
    rowf = blk_r.astype(F32)
    sel_bias = jnp.full((nc, G * tq), NEG, F32)
    for _ in range(min(NSA_TOPN, n_blocks)):
        m = jnp.max(score, axis=0, keepdims=True)
        first = jnp.min(jnp.where(score == m, rowf, float(nc)), axis=0, keepdims=True)
        hit = rowf == first
        sel_bias = jnp.where(hit, 0.0, sel_bias)
        score = jnp.where(hit, -jnp.inf, score)
    sel_ref[...] = sel_bias

    ones = jnp.ones((8, tk), BF16)
    kpos_col = lax.broadcasted_iota(jnp.int32, (tk, 1), 0)
    bpt = tk // NSA_BLOCK

    def sel_tile(kt, carry):
        k0 = pl.multiple_of(kt * tk, tk)
        k_tile = ks_ref[pl.ds(k0, tk), :]
        chosen = jnp.concatenate(
            [jnp.broadcast_to(sel_ref[pl.ds(kt * bpt + m, 1), :], (NSA_BLOCK, G * tq)) for m in range(bpt)], axis=0)
        causal = jnp.where(k0 + kpos_col <= qpos, 0.0, NEG)
        out = []
        for g in range(G):
            v_ext = jnp.concatenate([vs_ref[g * N:(g + 1) * N, pl.ds(k0, tk)].astype(BF16), ones], axis=0)
            out.append(_attend_t(k_tile, v_ext, q_b[g], chosen[:, g * tq:(g + 1) * tq] + causal, carry[g]))
        return tuple(out)

    def sel_pair(pp, carry):
        return sel_tile(2 * pp + 1, sel_tile(2 * pp, carry))

    n_kt = (t0 + tq + tk - 1) // tk
    init = tuple(_attend_t_init(W) for _ in range(G))
    if (ks_ref.shape[0] // tk) % 2 == 0:
        carry = lax.fori_loop(0, (n_kt + 1) // 2, sel_pair, init)
    else:
        carry = lax.fori_loop(0, n_kt, sel_tile, init)
    o_s = [_attend_t_finish(c) for c in carry]

    kpos_w = lax.broadcasted_iota(jnp.int32, (tq, 1), 0)
    carry = init
    for back in range(NSA_WINDOW // tq, -1, -1):
        kt = jnp.maximum(i - back, 0)
        k0 = pl.multiple_of(kt * tq, tq)
        d = qpos - (k0 + kpos_w)
        bias = jnp.where((d >= 0) & (d < NSA_WINDOW) & (i - back >= 0), 0.0, NEG)
        k_tile = kw_ref[pl.ds(k0, tq), :]
        out = []
        for g in range(G):
            v_ext = jnp.concatenate([vw_ref[g * N:(g + 1) * N, pl.ds(k0, tq)].astype(BF16), ones[:, :tq]], axis=0)
            out.append(_attend_t(k_tile, v_ext, q_b[g], bias, carry[g]))
        carry = tuple(out)
    o_w = [_attend_t_finish(c) for c in carry]

    for g in range(G):
        for r in range(R):
            c0 = _gate_lane(g, r)
            h = g * R + r
            cols = slice(r * tq, (r + 1) * tq)
            ot_ref[h * N:(h + 1) * N, :] = (gt_t[c0:c0 + 1] * o_c[g][:, cols] + gt_t[c0 + 1:c0 + 2] * o_s[g][:, cols]
                                            + gt_t[c0 + 2:c0 + 3] * o_w[g][:, cols])
    o_ref[...] = ot_ref[...].T


def _nsa_prompt(q, gates, ck, cv_t, ks, kvs_t, kw, kvw_t, W):
    B, T, _ = q.shape
    tq = min(128, T)
    tk = min(256, T)
    nc = ck.shape[1]
    per_b = lambda a: pl.BlockSpec((None,) + a.shape[1:], lambda b, i: (b,) + (0,) * (a.ndim - 1))
    tile = lambda w: pl.BlockSpec((None, tq, w), lambda b, i: (b, i, 0))
    v_rows = pl.BlockSpec((None, NSA_KV, T), lambda b, i: (b, 1, 0))
    gb = _gate_bias_row(W['nsa_gate_b'])
    return pl.pallas_call(
        functools.partial(_nsa_prompt_body, tq=tq, tk=tk, n_blocks=T // NSA_BLOCK),
        grid=(B, T // tq),
        in_specs=[tile(NSA_WIDTH), tile(LANES), pl.BlockSpec((1, LANES), lambda b, i: (0, 0)),
                  per_b(ck), per_b(cv_t), per_b(ks), v_rows, per_b(kw), v_rows],
        out_specs=tile(NSA_WIDTH),
        out_shape=jax.ShapeDtypeStruct((B, T, NSA_WIDTH), F32),
        scratch_shapes=[pltpu.VMEM((nc, NSA_KV_HEADS * tq), F32), pltpu.VMEM((NSA_WIDTH, tq), F32)],
        compiler_params=pltpu.CompilerParams(dimension_semantics=("parallel", "arbitrary"),
                                             vmem_limit_bytes=VMEM_LIMIT),
        name="nsa_prompt",
    )(q, gates, gb, ck, cv_t, ks, kvs_t, kw, kvw_t)


def _stack_heads(qg):
    return jnp.concatenate([qg[:, r * HEAD_DIM:(r + 1) * HEAD_DIM] for r in range(NSA_GROUP)],
                           axis=0) * (HEAD_DIM ** -0.5)


def _tile_rows(x, n):
    return jnp.concatenate([x] * n, axis=0)


def _masked_softmax(s, mask):
    s = jnp.where(mask, s, -jnp.inf)
    m = jnp.max(s, axis=-1, keepdims=True)
    m = jnp.where(m == -jnp.inf, 0.0, m)
    e = jnp.exp(s - m)
    return e / jnp.maximum(jnp.sum(e, axis=-1, keepdims=True), 1e-30)


def _select_blocks(imp, qpos, n_blocks, k):
    t, w = imp.shape
    jf = lax.broadcasted_iota(jnp.int32, (t, w), 1)
    cur = jnp.right_shift(qpos, BLOCK_SHIFT)
    forced = (jf == 0) | (jf == cur) | (jf == cur - 1)
    score = jnp.where(forced, BIG, jnp.where(jf <= cur, imp, -BIG))
    score = jnp.where(jf < n_blocks, score, -jnp.inf)
    lane = jf.astype(F32)
    idx = []
    for _ in range(k):
        m = jnp.max(score, axis=-1, keepdims=True)
        i = jnp.min(jnp.where(score == m, lane, float(w)), axis=-1, keepdims=True)
        score = jnp.where(lane == i, -jnp.inf, score)
        idx.append(i)
    return idx


def _online_step(carry, s, ok, pv):
    m, l, acc = carry
    s = jnp.where(ok, s, -jnp.inf)
    m_new = jnp.maximum(m, jnp.max(s, axis=-1, keepdims=True))
    m_safe = jnp.where(m_new == -jnp.inf, 0.0, m_new)
    p = jnp.exp(s - m_safe)
    alpha = jnp.exp(m - m_safe)
    return m_new, alpha * l + jnp.sum(p, axis=-1, keepdims=True), alpha * acc + pv(p)


def _online_init(rows):
    return (jnp.full((rows, 1), -jnp.inf, F32), jnp.zeros((rows, 1), F32), jnp.zeros((rows, HEAD_DIM), F32))


def _online_finish(carry):
    _, l, acc = carry
    return acc / jnp.maximum(l, 1e-30)


def _nsa_sample_cmp_body(q_ref, ck_ref, cv_ref, kvw_ref, winp_ref, oc_ref, ow_ref, idx_ref, *, pos0, n_blocks):
    R, N = NSA_GROUP, HEAD_DIM
    t = q_ref.shape[0]
    wb = winp_ref.shape[1]
    qpos = pos0 + lax.broadcasted_iota(jnp.int32, (t, 1), 0)
    qpos_s = _tile_rows(qpos, R)
    q = q_ref[...]
    ck = ck_ref[...]
    nc = ck.shape[0]
    w_sel = -(-n_blocks // LANES) * LANES
    new = kvw_ref[...]
    oc_heads, ow_heads = [], []
    for g in range(NSA_KV_HEADS):
        krows = slice(g * N, (g + 1) * N)
        vrows = slice(NSA_KV + g * N, NSA_KV + (g + 1) * N)
        qs = _stack_heads(q[:, g * R * N:(g + 1) * R * N])
        blk = lax.broadcasted_iota(jnp.int32, (1, nc), 1)
        p = _masked_softmax(_dot_nt_hi(qs, ck[:, krows]), (blk + 1) * NSA_BLOCK - 1 <= qpos_s)
        o_c = _dot_nt(p, cv_ref[krows, :])
        imp = p[0:t]
        for r in range(1, R):
            imp = imp + p[r * t:(r + 1) * t]
        if w_sel > nc:
            imp = jnp.concatenate([imp, jnp.zeros((t, w_sel - nc), F32)], axis=1)
        idx = _select_blocks(imp, qpos, n_blocks, min(NSA_TOPN, n_blocks))
        lane = lax.broadcasted_iota(jnp.int32, (t, LANES), 1)
        tile = jnp.zeros((t, LANES), F32)
        for kk, col in enumerate(idx):
            tile = jnp.where(lane == kk, col, tile)
        idx_ref[g] = tile.astype(jnp.int32)

        carry = _online_init(R * t)
        d = qpos_s - (pos0 - wb + lax.broadcasted_iota(jnp.int32, (1, wb), 1))
        carry = _online_step(carry, _dot(qs, winp_ref[krows, :]), (d >= 0) & (d < NSA_WINDOW),
                             lambda p: _dot_nt(p, winp_ref[vrows, :]))
        d = qpos_s - (pos0 + lax.broadcasted_iota(jnp.int32, (1, t), 1))
        carry = _online_step(carry, _dot_nt(qs, new[:, krows]), (d >= 0) & (d < NSA_WINDOW),
                             lambda p: _dot(p, new[:, vrows]))
        o_w = _online_finish(carry)
        oc_heads += [o_c[r * t:(r + 1) * t] for r in range(R)]
        ow_heads += [o_w[r * t:(r + 1) * t] for r in range(R)]
    oc_ref[...] = jnp.concatenate(oc_heads, axis=1)
    ow_ref[...] = jnp.concatenate(ow_heads, axis=1)


def _nsa_sample_cmp(q, ck, cv_t, kv_w, win_prev_t, pos0, n_blocks):
    B, T, _ = q.shape
    per_b = lambda a: pl.BlockSpec((None,) + a.shape[1:], lambda b: (b,) + (0,) * (a.ndim - 1))
    out_b = lambda *s: pl.BlockSpec((None,) + s, lambda b: (b,) + (0,) * len(s))
    return pl.pallas_call(
        functools.partial(_nsa_sample_cmp_body, pos0=pos0, n_blocks=n_blocks),
        grid=(B,),
        in_specs=[per_b(q), per_b(ck), per_b(cv_t), per_b(kv_w), per_b(win_prev_t)],
        out_specs=[out_b(T, NSA_WIDTH), out_b(T, NSA_WIDTH), out_b(NSA_KV_HEADS, T, LANES)],
        out_shape=[jax.ShapeDtypeStruct((B, T, NSA_WIDTH), F32), jax.ShapeDtypeStruct((B, T, NSA_WIDTH), F32),
                   jax.ShapeDtypeStruct((B, NSA_KV_HEADS, T, LANES), jnp.int32)],
        compiler_params=pltpu.CompilerParams(dimension_semantics=("parallel",)),
        name="nsa_sample_cmp",
    )(q, ck, cv_t, kv_w, win_prev_t)


def _nsa_sample_sel_body(idx_ref, pt_ref, *refs, n_past, pos0, topk):
    R, N, G = NSA_GROUP, HEAD_DIM, NSA_KV_HEADS
    kv_refs = refs[:2 * G * topk]
    q_ref, new_ref, oc_ref, ow_ref, gates_ref, gb_ref, o_ref = refs[2 * G * topk:]
    b = pl.program_id(0)
    t = pl.program_id(1)
    n_t = pl.num_programs(1)
    qpos = pos0 + t
    gt = _sigmoid(gates_ref[pl.ds(t, 1), :] + gb_ref[...])
    q = q_ref[pl.ds(t, 1), :]
    o_c = oc_ref[pl.ds(t, 1), :]
    o_w = ow_ref[pl.ds(t, 1), :]
    lane = lax.broadcasted_iota(jnp.int32, (1, PAGE_SIZE), 1)
    heads = []
    for g in range(G):
        qs = _stack_heads(q[:, g * R * N:(g + 1) * R * N])
        new_k = new_ref[g * N:(g + 1) * N, :]
        new_v = new_ref[NSA_KV + g * N:NSA_KV + (g + 1) * N, :]
        scores, vals = [], []
        for k in range(topk):
            j = idx_ref[((b * n_t + t) * G + g) * topk + k]
            half = j % (PAGE_SIZE // NSA_BLOCK)
            is_new = j >= n_past
            k_t = jnp.where(is_new, new_k, kv_refs[2 * (g * topk + k)][...])
            v_t = jnp.where(is_new, new_v, kv_refs[2 * (g * topk + k) + 1][...])
            kpos = (j - half) * NSA_BLOCK + lane
            ok = (jnp.right_shift(lane, BLOCK_SHIFT) == half) & (kpos <= qpos)
            scores.append(jnp.where(ok, _dot(qs, k_t), -jnp.inf))
            vals.append(v_t)
        m = scores[0].max(axis=-1, keepdims=True)
        for s in scores[1:]:
            m = jnp.maximum(m, s.max(axis=-1, keepdims=True))
        m = jnp.where(m == -jnp.inf, 0.0, m)
        l = jnp.zeros((R, 1), F32)
        acc = jnp.zeros((R, N), F32)
        for s, v_t in zip(scores, vals):
            e = jnp.exp(s - m)
            l = l + jnp.sum(e, axis=-1, keepdims=True)
            acc = acc + _dot_nt(e, v_t)
        o_s = acc / jnp.maximum(l, 1e-30)
        for r in range(R):
            c0 = (g * R + r) * N
            gl = _gate_lane(g, r)
            heads.append(gt[:, gl:gl + 1] * o_c[:, c0:c0 + N] + gt[:, gl + 1:gl + 2] * o_s[r:r + 1]
                         + gt[:, gl + 2:gl + 3] * o_w[:, c0:c0 + N])
    o_ref[pl.ds(t, 1), :] = jnp.concatenate(heads, axis=1)


def _nsa_sample_sel(q, idx, page_table, pool_t, kv_s_new, o_c, o_w, gates, W, pos0):
    B, T, _ = q.shape
    G, N = NSA_KV_HEADS, HEAD_DIM
    topk = idx.shape[-1]
    n_pages = page_table.shape[1]
    bpp = PAGE_SIZE // NSA_BLOCK
    n_past = pos0 // NSA_BLOCK
    assert n_past % bpp == 0 and T <= NSA_BLOCK
    new_t = jnp.swapaxes(jnp.pad(kv_s_new, ((0, 0), (0, PAGE_SIZE - T), (0, 0))), 1, 2)

    def kv_spec(g, k, is_v):
        def index(b, t, idx_ref, pt_ref):
            j = jnp.minimum(idx_ref[((b * T + t) * G + g) * topk + k], n_past - 1)
            return (pt_ref[b * n_pages + j // bpp], (G if is_v else 0) + g, 0)
        return pl.BlockSpec((None, N, PAGE_SIZE), index)

    per_b = lambda a: pl.BlockSpec((None,) + a.shape[1:], lambda b, t, i, p: (b,) + (0,) * (a.ndim - 1))
    gb = _gate_bias_row(W['nsa_gate_b'])
    kv_specs = [kv_spec(g, k, is_v) for g in range(G) for k in range(topk) for is_v in (False, True)]
    return pl.pallas_call(
        functools.partial(_nsa_sample_sel_body, n_past=n_past, pos0=pos0, topk=topk),
        grid_spec=pltpu.PrefetchScalarGridSpec(
            num_scalar_prefetch=2,
            grid=(B, T),
            in_specs=kv_specs + [per_b(q), per_b(new_t), per_b(o_c), per_b(o_w), per_b(gates),
                                 pl.BlockSpec((1, LANES), lambda b, t, i, p: (0, 0))],
            out_specs=pl.BlockSpec((None, T, NSA_WIDTH), lambda b, t, i, p: (b, 0, 0)),
        ),
        out_shape=jax.ShapeDtypeStruct((B, T, NSA_WIDTH), F32),
        compiler_params=pltpu.CompilerParams(dimension_semantics=("parallel", "arbitrary")),
        name="nsa_sample_sel",
    )(idx.reshape(-1), page_table.reshape(-1), *([pool_t] * len(kv_specs)), q, new_t, o_c, o_w, gates, gb)


def _nsa_sample_mixer(q, gates, kv_s, kv_w, page_table, pool_cmp_t, pool_sel_t, win_prev_t, W):
    B, T, _ = q.shape
    pos0 = page_table.shape[1] * PAGE_SIZE
    n_blocks = -(-(pos0 + T) // NSA_BLOCK)
    ck, cv_t = _split_cmp(_nsa_compress_paged(pool_cmp_t, page_table, W))
    o_c, o_w, idx = _nsa_sample_cmp(q, ck, cv_t, kv_w, win_prev_t, pos0, n_blocks)
    idx = jnp.swapaxes(idx[..., :min(NSA_TOPN, n_blocks)], 1, 2)
    return _nsa_sample_sel(q, idx, page_table, pool_sel_t, kv_s, o_c, o_w, gates, W, pos0)


def _mem_kv_body(x_ref, g_ref, w_ref, gk_ref, avg_ref, o_ref):
    h = _rms_rows(x_ref[...], g_ref[...]).astype(BF16)
    kv = lax.dot_general(w_ref[...], h, (((1,), (1,)), ((), ())), preferred_element_type=F32)
    k = kv[:MEM_WIDTH]
    o_ref[:MEM_WIDTH, :] = k * lax.rsqrt(_seg_mean_sq_rows(k, avg_ref[...]) + NORM_EPS) * gk_ref[...]
    o_ref[MEM_WIDTH:, :] = kv[MEM_WIDTH:]


def _mem_kv(mem, g, w_kv, k_g):
    B, m, d = mem.shape
    consts = [g.reshape(1, d), w_kv.T.astype(BF16), jnp.tile(k_g, MEM_HEADS).reshape(MEM_WIDTH, 1),
              _seg_avg_matrix(MEM_WIDTH, 1.0 / HEAD_DIM)]
    return pl.pallas_call(
        _mem_kv_body,
        grid=(B,),
        in_specs=[pl.BlockSpec((None, m, d), lambda b: (b, 0, 0))] + [_const_spec(c.shape) for c in consts],
        out_specs=pl.BlockSpec((None, 2 * MEM_WIDTH, m), lambda b: (b, 0, 0)),
        out_shape=jax.ShapeDtypeStruct((B, 2 * MEM_WIDTH, m), F32),
        compiler_params=pltpu.CompilerParams(dimension_semantics=("parallel",)),
        name="mem_kv",
    )(mem, *consts)


def _out_mem_body(x_ref, orw_ref, oml_ref, onsa_ref, kv_ref, w1_ref, w2_ref, w3_ref, g_ref, wq_ref, gq_ref,
                  avg_ref, wo_ref, o_ref):
    N = HEAD_DIM
    x = (x_ref[...] + _dot(orw_ref[...], w1_ref[...]) + _dot(oml_ref[...], w2_ref[...])
         + _dot(onsa_ref[...], w3_ref[...]))
    h = _rms_rows(x, g_ref[...]).astype(BF16)
    q = jnp.dot(h, wq_ref[...], preferred_element_type=F32)
    q = q * lax.rsqrt(_seg_mean_sq(q, avg_ref[...]) + NORM_EPS) * gq_ref[...] * (N ** -0.5)
    heads = []
    for hd in range(MEM_HEADS):
        s = _dot(q[:, hd * N:(hd + 1) * N], kv_ref[hd * N:(hd + 1) * N, :])
        e = jnp.exp(s - jnp.max(s, axis=-1, keepdims=True))
        p = e / jnp.sum(e, axis=-1, keepdims=True)
        heads.append(_dot_nt(p, kv_ref[MEM_WIDTH + hd * N:MEM_WIDTH + (hd + 1) * N, :]))
    o_ref[...] = x + _dot(jnp.concatenate(heads, axis=1), wo_ref[...])


def _out_mem(x, o_rw, o_ml, o_nsa, kv_t, w_out, g, w_q, q_g, w_o):
    B, T, d = x.shape
    tm = _row_tile(T, 512)
    w1 = w_out[:RW_WIDTH].astype(BF16)
    w2 = w_out[RW_WIDTH:RW_WIDTH + ML_WIDTH].astype(BF16)
    w3 = w_out[RW_WIDTH + ML_WIDTH:].astype(BF16)
    consts = [w1, w2, w3, g.reshape(1, d), w_q.astype(BF16), jnp.tile(q_g, MEM_HEADS).reshape(1, MEM_WIDTH),
              _seg_avg_matrix(MEM_WIDTH, 1.0 / HEAD_DIM), w_o.astype(BF16)]
    tile = lambda w: pl.BlockSpec((None, tm, w), lambda b, i: (b, i, 0))
    return pl.pallas_call(
        _out_mem_body,
        grid=(B, T // tm),
        in_specs=[tile(d), tile(RW_WIDTH), tile(ML_WIDTH), tile(NSA_WIDTH),
                  pl.BlockSpec((None,) + kv_t.shape[1:], lambda b, i: (b, 0, 0))]
                 + [_const_spec(c.shape) for c in consts],
        out_specs=tile(d),
        out_shape=jax.ShapeDtypeStruct((B, T, d), F32),
        compiler_params=pltpu.CompilerParams(dimension_semantics=("parallel", "parallel"),
                                             vmem_limit_bytes=VMEM_LIMIT),
        name="out_mem",
    )(x, o_rw, o_ml, o_nsa, kv_t, *consts)


def _layer(x, W, st, page_table, mem):
    B, T, d = x.shape
    is_prompt = st is None
    g = W['norm_g']
    G = NSA_KV_HEADS
    x1 = _ffn(x.reshape(B * T, d), g[0], W['ffa_up'], W['ffa_down']).reshape(B, T, d)
    proj = _in_proj(x1, g[1], W['w_in_perm'], W['nsa_qk_g'], channel_major=is_prompt)
    p_rw, p_ml, gates, q = proj[:4]
    if is_prompt:
        zeros = lambda *s: jnp.zeros(s, F32)
        st = {'rw_shift': zeros(B, 1, RW_COLS), 'rw_S': zeros(B, RW_HEADS, HEAD_DIM, HEAD_DIM),
              'ml_conv': zeros(B, ML_CONV - 1, ML_WIDTH), 'ml_C': zeros(B, ML_HEADS, HEAD_DIM, HEAD_DIM),
              'ml_n': zeros(B, ML_HEADS, HEAD_DIM), 'ml_m': zeros(B, ML_HEADS)}
        mem_kv_t = _mem_kv(mem, g[3], W['mem_w_kv'], W['mem_qk_g'][1])
    else:
        mem_kv_t = _channel_major(st['mem_kv'])
    o_rw, rw_S = _rwkv(p_rw, st['rw_shift'], st['rw_S'], W)
    o_ml, ml_C, ml_n, ml_m = _mlstm(p_ml, gates, st['ml_conv'], st['ml_C'], st['ml_n'], st['ml_m'], W)
    if is_prompt:
        ks, kw, kvc_t, kvs_t, kvw_t = proj[4:]
        ck, cv_t = _split_cmp(_nsa_compress_dense(kvc_t, W))
        o_nsa = _nsa_prompt(q, gates, ck, cv_t, ks, kvs_t, kw, kvw_t, W)
        new_kv = {'nsa_cmp': _token_major(kvc_t, 2, G), 'nsa_sel': _token_major(kvs_t, 2, G),
                  'nsa_win': _token_major(kvw_t[:, :, T - min(NSA_WINDOW, T):], 2, G)}
    else:
        kv_c, kv_s, kv_w = proj[4:]
        win_prev_t = _channel_major(st['nsa_win'])
        wb = win_prev_t.shape[2]
        o_nsa = _nsa_sample_mixer(q, gates, kv_s, kv_w, page_table, st['nsa_cmp_t'], st['nsa_sel_t'], win_prev_t, W)
        win_t = jnp.concatenate([win_prev_t, jnp.swapaxes(kv_w, 1, 2)], axis=2)
        kv5 = lambda z: z.reshape(B, T, 2, G, HEAD_DIM)
        new_kv = {'nsa_cmp': kv5(kv_c), 'nsa_sel': kv5(kv_s),
                  'nsa_win': _token_major(win_t[:, :, wb + T - min(NSA_WINDOW, wb + T):], 2, G)}
    x2 = _out_mem(x1, o_rw, o_ml, o_nsa, mem_kv_t, W['w_out'], g[2], W['mem_w_q'], W['mem_qk_g'][0], W['mem_w_o'])
    x3 = _ffn(x2.reshape(B * T, d), g[4], W['ffb_up'], W['ffb_down']).reshape(B, T, d)
    qk_in = p_ml[:, :, :ML_WIDTH]
    conv_all = jnp.concatenate([st['ml_conv'], qk_in], axis=1) if T < ML_CONV - 1 else qk_in
    new = dict(new_kv)
    new.update({'rw_shift': p_rw[:, T - 1:], 'rw_S': rw_S, 'ml_conv': conv_all[:, conv_all.shape[1] - (ML_CONV - 1):],
                'ml_C': ml_C, 'ml_n': ml_n, 'ml_m': ml_m})
    if is_prompt:
        new['mem_kv'] = _token_major(mem_kv_t, 2, MEM_HEADS)
    return x3, new


def kernel(x_prompt, x_sample, cache_nsa_cmp, cache_nsa_sel, cache_nsa_win, cache_mem_kv, state_rwkv_shift, state_rwkv_S, state_mlstm_conv, state_mlstm_C, state_mlstm_n, state_mlstm_m, page_table, mem_prompt, norm_g, ffa_up, ffa_down, ffb_up, ffb_down, w_in, w_out, rw_mu, rw_w0, rw_w2, rw_a0, rw_a2, rw_g2, rw_kk, rw_ka, rw_rk, rw_gn_g, rw_gn_b, ml_conv_w, ml_conv_b, ml_wq, ml_wk, ml_gate_b, ml_norm_g, ml_skip, nsa_qk_g, nsa_wpos, nsa_wc, nsa_gate_b, mem_w_q, mem_w_kv, mem_qk_g, mem_w_o):
    params = dict(norm_g=norm_g, ffa_up=ffa_up, ffa_down=ffa_down, ffb_up=ffb_up, ffb_down=ffb_down, w_out=w_out,
                  rw_mu=rw_mu, rw_w0=rw_w0, rw_w2=rw_w2, rw_a0=rw_a0, rw_a2=rw_a2, rw_g2=rw_g2, rw_kk=rw_kk,
                  rw_ka=rw_ka, rw_rk=rw_rk, rw_gn_g=rw_gn_g, rw_gn_b=rw_gn_b, ml_conv_w=ml_conv_w,
                  ml_conv_b=ml_conv_b, ml_wq=ml_wq, ml_wk=ml_wk, ml_gate_b=ml_gate_b, ml_norm_g=ml_norm_g,
                  ml_skip=ml_skip, nsa_qk_g=nsa_qk_g, nsa_wpos=nsa_wpos, nsa_wc=nsa_wc, nsa_gate_b=nsa_gate_b,
                  mem_w_q=mem_w_q, mem_w_kv=mem_w_kv, mem_qk_g=mem_qk_g, mem_w_o=mem_w_o)
    depth = norm_g.shape[0]
    y_p, y_s = x_prompt, x_sample
    new_p, new_s = [], []
    n_phys = cache_nsa_cmp.shape[1]
    all_pages = lambda c: _channel_major(c.reshape((depth * n_phys,) + c.shape[2:]))
    pool_cmp_t, pool_sel_t = all_pages(cache_nsa_cmp), all_pages(cache_nsa_sel)
    for l in range(depth):
        W = {name: v[l] for name, v in params.items()}
        W['w_in_perm'] = _permute_w_in(w_in[l])
        st = {'nsa_cmp_t': pool_cmp_t, 'nsa_sel_t': pool_sel_t, 'nsa_win': cache_nsa_win[l],
              'mem_kv': cache_mem_kv[l], 'rw_shift': state_rwkv_shift[l], 'rw_S': state_rwkv_S[l],
              'ml_conv': state_mlstm_conv[l], 'ml_C': state_mlstm_C[l], 'ml_n': state_mlstm_n[l],
              'ml_m': state_mlstm_m[l]}
        y_p, sp = _layer(y_p, W, None, None, mem_prompt)
        y_s, ss = _layer(y_s, W, st, page_table + l * n_phys, None)
        new_p.append(sp)
        new_s.append(ss)
    P = lambda name: jnp.stack([d[name] for d in new_p])
    S = lambda name: jnp.stack([d[name] for d in new_s])
    return (y_p, y_s,
            P('nsa_cmp'), S('nsa_cmp'), P('nsa_sel'), S('nsa_sel'), P('nsa_win'), S('nsa_win'),
            P('mem_kv'),
            P('rw_shift'), S('rw_shift'), P('rw_S'), S('rw_S'),
            P('ml_conv'), S('ml_conv'), P('ml_C'), S('ml_C'), P('ml_n'), S('ml_n'), P('ml_m'), S('ml_m'))
```

```python
import functools

import jax
import jax.numpy as jnp
from jax import lax
from jax.experimental import pallas as pl
from jax.experimental.pallas import tpu as pltpu

F32 = jnp.float32
BF16 = jnp.bfloat16

HEAD_DIM = 64
RW_HEADS = 4
RW_WIDTH = RW_HEADS * HEAD_DIM
RW_W_LORA = 64
RW_A_LORA = 64
RW_G_LORA = 128
RW_COLS = 3 * RW_WIDTH + RW_W_LORA + RW_A_LORA + RW_G_LORA
RW_GN_EPS = 64e-5
ML_HEADS = 4
ML_WIDTH = ML_HEADS * HEAD_DIM
ML_CONV = 4
RW_CHUNKS = (64,)
ML_CHUNKS = (128, 64)
ML_COLS = 3 * ML_WIDTH + 2 * ML_HEADS
NSA_HEADS = 8
NSA_KV_HEADS = 2
NSA_GROUP = NSA_HEADS // NSA_KV_HEADS
NSA_WIDTH = NSA_HEADS * HEAD_DIM
NSA_KV = NSA_KV_HEADS * HEAD_DIM
NSA_BLOCK = 64
BLOCK_SHIFT = 6
NSA_TOPN = 8
NSA_WINDOW = 512
NSA_COLS = NSA_WIDTH + 6 * NSA_KV + 3 * NSA_HEADS
MEM_HEADS = 4
MEM_WIDTH = MEM_HEADS * HEAD_DIM
PAGE_SIZE = 128
NORM_EPS = 1e-6
BIG = 1e9
NEG = -1e30
LOG2_E = 1.4426950408889634
RW_CHAIN_BATCH = 4
ML_CHAIN_BATCH = 2
LANES = 128
VMEM_LIMIT = 56 * 1024 * 1024

P_RW = 0
P_ML = P_RW + RW_COLS
P_Q = P_ML + 3 * ML_WIDTH
P_KVC = P_Q + NSA_WIDTH
P_KVS = P_KVC + 2 * NSA_KV
P_KVW = P_KVS + 2 * NSA_KV
P_GATES = P_KVW + 2 * NSA_KV
P_TOTAL = P_GATES + LANES


def _dot(a, b):
    return jnp.dot(a.astype(BF16), b.astype(BF16), preferred_element_type=F32)


def _dot_nt(a, b):
    return lax.dot_general(a.astype(BF16), b.astype(BF16), (((1,), (1,)), ((), ())),
                           preferred_element_type=F32)


def _dot_tn(a, b):
    return lax.dot_general(a.astype(BF16), b.astype(BF16), (((0,), (0,)), ((), ())),
                           preferred_element_type=F32)


def _split2(x):
    hi = x.astype(BF16)
    lo = (x - hi.astype(F32)).astype(BF16)
    return hi, lo


def _split3(x):
    hi = x.astype(BF16)
    r = x - hi.astype(F32)
    mid = r.astype(BF16)
    lo = (r - mid.astype(F32)).astype(BF16)
    return hi, mid, lo


def _dot_sel(sel, x):
    hi, mid, lo = _split3(x)
    s = sel.astype(BF16)
    d = lambda t: jnp.dot(s, t, preferred_element_type=F32)
    return d(hi) + d(mid) + d(lo)


def _dot_x_sel(x, sel):
    hi, mid, lo = _split3(x)
    s = sel.astype(BF16)
    d = lambda t: jnp.dot(t, s, preferred_element_type=F32)
    return d(hi) + d(mid) + d(lo)


def _dot_hi(a, b):
    ah, al = _split2(a)
    bh, bl = _split2(b)
    d = lambda u, v: jnp.dot(u, v, preferred_element_type=F32)
    return d(ah, bh) + d(al, bh) + d(ah, bl)


def _dot_nt_hi(a, b):
    ah, al = _split2(a)
    bh, bl = _split2(b)
    d = lambda u, v: lax.dot_general(u, v, (((1,), (1,)), ((), ())), preferred_element_type=F32)
    return d(ah, bh) + d(al, bh) + d(ah, bl)


def _seg_mean_sq(x, seg_avg):
    hi, lo = _split2(x * x)
    d = lambda t: jnp.dot(t, seg_avg, preferred_element_type=F32)
    return d(hi) + d(lo)


def _seg_mean_sq_rows(x, seg_avg):
    hi, lo = _split2(x * x)
    d = lambda t: jnp.dot(seg_avg, t, preferred_element_type=F32)
    return d(hi) + d(lo)


def _rms_rows(x, g):
    return x * lax.rsqrt(jnp.mean(x * x, axis=-1, keepdims=True) + NORM_EPS) * g


def _sigmoid(x):
    return 1.0 / (1.0 + jnp.exp(-x))


def _softplus(x):
    return jnp.maximum(x, 0.0) + jnp.log(1.0 + jnp.exp(-jnp.abs(x)))


def _row_tile(n, target):
    t = min(n, target)
    while n % t:
        t //= 2
    return t


def _scan_chunk(t, candidates):
    for c in candidates:
        if t % c == 0:
            return c
    return t


def _const_spec(shape):
    nd = len(shape)
    return pl.BlockSpec(shape, lambda *_: (0,) * nd, pipeline_mode=pl.Buffered(1))


def _seg_avg_matrix(width, scale):
    i = jnp.arange(width) // HEAD_DIM
    return (jnp.where(i[:, None] == i[None, :], scale, 0.0)).astype(BF16)


def _channel_major(x):
    b, t = x.shape[:2]
    return jnp.transpose(x, (0, 2, 3, 4, 1)).reshape(b, -1, t)


def _token_major(x, c, g):
    b, w, t = x.shape
    return jnp.transpose(x.reshape(b, c, g, w // (c * g), t), (0, 4, 1, 2, 3))


def _ffn_body(x_ref, g_ref, wg_ref, wu_ref, wd_ref, o_ref, *, f_chunk):
    x = x_ref[...]
    h = _rms_rows(x, g_ref[...]).astype(BF16)
    d_ff = wg_ref.shape[1]
    acc = jnp.zeros_like(x)
    for c in range(d_ff // f_chunk):
        sl = slice(c * f_chunk, (c + 1) * f_chunk)
        gate = jnp.dot(h, wg_ref[:, sl], preferred_element_type=F32)
        up = jnp.dot(h, wu_ref[:, sl], preferred_element_type=F32)
        act = (gate * _sigmoid(gate) * up).astype(BF16)
        acc = acc + jnp.dot(act, wd_ref[sl, :], preferred_element_type=F32)
    o_ref[...] = x + 0.5 * acc


def _ffn(x, g, w_up, w_down):
    n, d = x.shape
    d_ff = w_down.shape[0]
    tm = _row_tile(n, 512)
    f_chunk = d_ff // 2 if (d_ff // 2) % LANES == 0 else d_ff
    wg = w_up[:, :d_ff].astype(BF16)
    wu = w_up[:, d_ff:].astype(BF16)
    wd = w_down.astype(BF16)
    row = pl.BlockSpec((tm, d), lambda i: (i, 0))
    return pl.pallas_call(
        functools.partial(_ffn_body, f_chunk=f_chunk),
        grid=(n // tm,),
        in_specs=[row, _const_spec((1, d)), _const_spec((d, d_ff)), _const_spec((d, d_ff)),
                  _const_spec((d_ff, d))],
        out_specs=row,
        out_shape=jax.ShapeDtypeStruct((n, d), F32),
        compiler_params=pltpu.CompilerParams(dimension_semantics=("parallel",),
                                             vmem_limit_bytes=VMEM_LIMIT),
        name="ffn",
    )(x, g.reshape(1, d), wg, wu, wd)


def _in_proj_body(x_ref, g_ref, w_ref, gq_ref, gks_ref, gkw_ref, avg_ref, rw_ref, ml_ref, gates_ref, q_ref,
                  *kv_refs, channel_major):
    h = _rms_rows(x_ref[...], g_ref[...]).astype(BF16)
    p = jnp.dot(h, w_ref[...], preferred_element_type=F32)
    rw_ref[...] = p[:, P_RW:P_ML]
    ml_ref[...] = p[:, P_ML:P_Q]
    gates_ref[...] = p[:, P_GATES:P_TOTAL]
    avg = avg_ref[...]

    def head_norm(z, g):
        w = z.shape[1]
        return z * lax.rsqrt(_seg_mean_sq(z, avg[:w, :w]) + NORM_EPS) * g

    q_ref[...] = head_norm(p[:, P_Q:P_KVC], gq_ref[...])
    kvc = p[:, P_KVC:P_KVS]
    ks = head_norm(p[:, P_KVS:P_KVS + NSA_KV], gks_ref[...])
    vs = p[:, P_KVS + NSA_KV:P_KVW]
    kw = head_norm(p[:, P_KVW:P_KVW + NSA_KV], gkw_ref[...])
    vw = p[:, P_KVW + NSA_KV:P_GATES]
    if channel_major:
        ks_ref, kw_ref, kvc_t_ref, kvs_t_ref, kvw_t_ref = kv_refs
        ks_ref[...] = ks.astype(BF16)
        kw_ref[...] = kw.astype(BF16)
        kvc_t_ref[...] = kvc.T
        kvs_t_ref[:NSA_KV, :] = ks.T
        kvs_t_ref[NSA_KV:, :] = vs.T
        kvw_t_ref[:NSA_KV, :] = kw.T
        kvw_t_ref[NSA_KV:, :] = vw.T
    else:
        kvc_ref, kvs_ref, kvw_ref = kv_refs
        kvc_ref[...] = kvc
        kvs_ref[:, :NSA_KV] = ks
        kvs_ref[:, NSA_KV:] = vs
        kvw_ref[:, :NSA_KV] = kw
        kvw_ref[:, NSA_KV:] = vw


def _permute_w_in(w_in):
    d = w_in.shape[0]
    o_ml = RW_COLS
    o_nsa = RW_COLS + ML_COLS
    ml_gates = w_in[:, o_ml + 3 * ML_WIDTH:o_nsa]
    nsa_gates = w_in[:, o_nsa + NSA_WIDTH + 6 * NSA_KV:]
    pad = jnp.zeros((d, LANES - 2 * ML_HEADS - 3 * NSA_HEADS), w_in.dtype)
    return jnp.concatenate([w_in[:, :o_ml + 3 * ML_WIDTH], w_in[:, o_nsa:o_nsa + NSA_WIDTH + 6 * NSA_KV],
                            ml_gates, nsa_gates, pad], axis=1).astype(BF16)


def _in_proj(x, g, w_perm, qk_g, channel_major):
    B, T, d = x.shape
    n = B * T
    tm = _row_tile(T if channel_major else n, 512)
    tpb = T // tm
    row = lambda w: pl.BlockSpec((tm, w), lambda i: (i, 0))
    widths = [RW_COLS, 3 * ML_WIDTH, LANES, NSA_WIDTH]
    out_specs = [row(w) for w in widths]
    out_shape = [jax.ShapeDtypeStruct((n, w), F32) for w in widths]
    w_kv = 2 * NSA_KV
    if channel_major:
        out_specs += [row(NSA_KV)] * 2 + [pl.BlockSpec((None, w_kv, tm), lambda i: (i // tpb, 0, i % tpb))] * 3
        out_shape += [jax.ShapeDtypeStruct((n, NSA_KV), BF16)] * 2 + [jax.ShapeDtypeStruct((B, w_kv, T), F32)] * 3
    else:
        out_specs += [row(w_kv)] * 3
        out_shape += [jax.ShapeDtypeStruct((n, w_kv), F32)] * 3
    gq = jnp.tile(qk_g[0], NSA_HEADS).reshape(1, NSA_WIDTH)
    gks = jnp.tile(qk_g[2], NSA_KV_HEADS).reshape(1, NSA_KV)
    gkw = jnp.tile(qk_g[3], NSA_KV_HEADS).reshape(1, NSA_KV)
    avg = _seg_avg_matrix(NSA_WIDTH, 1.0 / HEAD_DIM)
    outs = pl.pallas_call(
        functools.partial(_in_proj_body, channel_major=channel_major),
        grid=(n // tm,),
        in_specs=[row(d), _const_spec((1, d)), _const_spec((d, P_TOTAL)), _const_spec((1, NSA_WIDTH)),
                  _const_spec((1, NSA_KV)), _const_spec((1, NSA_KV)), _const_spec((NSA_WIDTH, NSA_WIDTH))],
        out_specs=out_specs,
        out_shape=out_shape,
        compiler_params=pltpu.CompilerParams(dimension_semantics=("parallel",),
                                             vmem_limit_bytes=VMEM_LIMIT),
        name="in_proj",
    )(x.reshape(n, d), g.reshape(1, d), w_perm, gq, gks, gkw, avg)
    return [o.reshape(B, T, -1) if o.shape[0] == n else o for o in outs]


def _rwkv_chunk_inputs(p, prev_row, mu_ref, w0_ref, lw_ref, a0_ref, g2_ref, kks_ref, ka_ref, rk_ref, tri_ref,
                       seg_ref):
    C = p.shape[0]
    rows = lax.broadcasted_iota(jnp.int32, (C, 1), 0)
    prev = jnp.where(rows == 0, prev_row, pltpu.roll(p, 1, axis=0))
    xm = p + (prev - p) * mu_ref[...]
    r = xm[:, 0:RW_WIDTH]
    k = xm[:, RW_WIDTH:2 * RW_WIDTH]
    v = xm[:, 2 * RW_WIDTH:3 * RW_WIDTH]
    lin = xm[:, 3 * RW_WIDTH:3 * RW_WIDTH + RW_W_LORA + RW_A_LORA]
    lane = lax.broadcasted_iota(jnp.int32, lin.shape, 1)
    lora = _dot_hi(jnp.where(lane < RW_W_LORA, jnp.tanh(lin), lin), lw_ref[...])
    w = -_softplus(-(w0_ref[...] + lora[:, :RW_WIDTH])) - 0.5
    a = _sigmoid(a0_ref[...] + lora[:, RW_WIDTH:])
    g = _dot(_sigmoid(xm[:, 3 * RW_WIDTH + RW_W_LORA + RW_A_LORA:]), g2_ref[...])
    kk = k * kks_ref[...]
    hi, lo = _split2(kk * kk)
    seg = seg_ref[...]
    ss = jnp.dot(hi, seg, preferred_element_type=F32) + jnp.dot(lo, seg, preferred_element_type=F32)
    kk = kk / jnp.maximum(jnp.sqrt(ss), 1e-12)
    k = k * (1.0 + (a - 1.0) * ka_ref[...])
    logdec = -jnp.exp(w)
    tri = tri_ref[...]
    G = _dot_sel(tri, logdec)
    g_end = G[C - 1:C, :]
    e_g = jnp.exp(G)
    e_gi = jnp.exp(-G)
    kkd = kk * jnp.exp(G - logdec)
    rd = r * e_g
    b = kk * a
    bi = b * e_gi
    ki = k * e_gi
    e_end = jnp.exp(g_end - G)
    bi2 = b * e_end
    ki2 = k * e_end
    dec_end = jnp.exp(g_end)
    rkk = r * k * rk_ref[...]
    return dict(v=v, g=g, kkd=kkd, rd=rd, bi=bi, ki=ki, bi2=bi2, ki2=ki2, dec_end=dec_end, rkk=rkk)


def _rwkv_body(p_ref, shift_ref, s0_ref, mu_ref, w0_ref, lw_ref, a0_ref, g2_ref, kks_ref, ka_ref,
               rk_ref, gng_ref, gnb_ref, tri_ref, seg_ref, o_ref, st_ref, carry_ref, s_ref, *, chunk, nb):
    C = chunk
    N = HEAD_DIM

    @pl.when(pl.program_id(1) == 0)
    def _():
        carry_ref[...] = shift_ref[...]
        s_ref[...] = s0_ref[...]

    pre = []
    for bi in range(nb):
        p = p_ref[bi]
        pre.append(_rwkv_chunk_inputs(p, carry_ref[bi], mu_ref, w0_ref, lw_ref, a0_ref, g2_ref, kks_ref, ka_ref,
                                      rk_ref, tri_ref, seg_ref))
        carry_ref[bi] = p[C - 1:C, :]

    ti = lax.broadcasted_iota(jnp.int32, (C, C), 0)
    si = lax.broadcasted_iota(jnp.int32, (C, C), 1)
    strict = si < ti
    eye = (si == ti).astype(F32)
    s2 = lax.broadcasted_iota(jnp.int32, (C, 2 * C), 1)
    incl2 = jnp.where(s2 < C, s2, s2 - C) <= lax.broadcasted_iota(jnp.int32, (C, 2 * C), 0)

    units = [(bi, h) for bi in range(nb) for h in range(RW_HEADS)]
    col = lambda name, u: pre[u[0]][name][:, u[1] * N:(u[1] + 1) * N]
    lhs = [jnp.concatenate([col('kkd', u), col('rd', u)], axis=0) for u in units]
    rhs = [jnp.concatenate([col('bi', u), col('ki', u)], axis=0) for u in units]
    vh = [col('v', u) for u in units]
    s0 = [s_ref[bi, h] for bi, h in units]
    m4 = [_dot_nt(a, b) for a, b in zip(lhs, rhs)]
    ks = [_dot_nt(a, s) for a, s in zip(lhs, s0)]
    pw = [jnp.where(strict, -m[:C, :C], 0.0) for m in m4]
    lk = [jnp.where(strict, m[:C, C:], 0.0) for m in m4]
    mbk = [jnp.where(incl2, m[C:, :], 0.0) for m in m4]
    lkv = [_dot(a, b) for a, b in zip(lk, vh)]
    t_inv = [eye + n for n in pw]
    span = 2
    while span < C:
        pw = [_dot(x, x) for x in pw]
        t_inv = [t + _dot(t, x) for t, x in zip(t_inv, pw)]
        span *= 2
    u_ = [_dot(t, -(k_[:C] + l)) for t, k_, l in zip(t_inv, ks, lkv)]
    uv = [jnp.concatenate([a, b], axis=0) for a, b in zip(u_, vh)]
    y = [k_[C:] + _dot(m, x) for k_, m, x in zip(ks, mbk, uv)]
    s_new = [s * col('dec_end', u) + _dot_tn(x, jnp.concatenate([col('bi2', u), col('ki2', u)], axis=0))
             for s, x, u in zip(s0, uv, units)]
    for (bi, h), s, yy, v_ in zip(units, s_new, y, vh):
        sl = slice(h * N, (h + 1) * N)
        s_ref[bi, h] = s
        mean = jnp.mean(yy, axis=-1, keepdims=True)
        var = jnp.mean(jnp.square(yy - mean), axis=-1, keepdims=True)
        yn = (yy - mean) * lax.rsqrt(var + RW_GN_EPS) * gng_ref[:, sl] + gnb_ref[:, sl]
        bonus = jnp.sum(pre[bi]['rkk'][:, sl], axis=-1, keepdims=True) * v_
        o_ref[bi, :, sl] = (yn + bonus) * pre[bi]['g'][:, sl]

    @pl.when(pl.program_id(1) == pl.num_programs(1) - 1)
    def _():
        st_ref[...] = s_ref[...]


def _rwkv(p_rw, shift_prev, s0, W):
    B, T, _ = p_rw.shape
    C = _scan_chunk(T, RW_CHUNKS)
    nb = _row_tile(B, RW_CHAIN_BATCH)
    row = lambda v: v.reshape(1, -1).astype(F32)
    z = jnp.zeros((RW_W_LORA, RW_WIDTH), F32)
    lw = jnp.concatenate([jnp.concatenate([W['rw_w2'], z], axis=1),
                          jnp.concatenate([z, W['rw_a2']], axis=1)], axis=0)
    tri = (jnp.arange(C)[:, None] >= jnp.arange(C)[None, :]).astype(BF16)
    seg = _seg_avg_matrix(RW_WIDTH, 1.0)
    consts = [row(W['rw_mu']), row(W['rw_w0']), lw, row(W['rw_a0']), W['rw_g2'].astype(BF16), row(W['rw_kk']),
              row(W['rw_ka']), row(W['rw_rk']), row(W['rw_gn_g']), row(W['rw_gn_b']), tri, seg]
    out, s_t = pl.pallas_call(
        functools.partial(_rwkv_body, chunk=C, nb=nb),
        grid=(B // nb, T // C),
        in_specs=[pl.BlockSpec((nb, C, RW_COLS), lambda b, c: (b, c, 0)),
                  pl.BlockSpec((nb, 1, RW_COLS), lambda b, c: (b, 0, 0)),
                  pl.BlockSpec((nb, RW_HEADS, HEAD_DIM, HEAD_DIM), lambda b, c: (b, 0, 0, 0))]
                 + [_const_spec(c.shape) for c in consts],
        out_specs=[pl.BlockSpec((nb, C, RW_WIDTH), lambda b, c: (b, c, 0)),
                   pl.BlockSpec((nb, RW_HEADS, HEAD_DIM, HEAD_DIM), lambda b, c: (b, 0, 0, 0))],
        out_shape=[jax.ShapeDtypeStruct((B, T, RW_WIDTH), F32),
                   jax.ShapeDtypeStruct((B, RW_HEADS, HEAD_DIM, HEAD_DIM), F32)],
        scratch_shapes=[pltpu.VMEM((nb, 1, RW_COLS), F32), pltpu.VMEM((nb, RW_HEADS, HEAD_DIM, HEAD_DIM), F32)],
        compiler_params=pltpu.CompilerParams(dimension_semantics=("parallel", "arbitrary"),
                                             vmem_limit_bytes=VMEM_LIMIT),
        name="rwkv7",
    )(p_rw, shift_prev, s0, *consts)
    return out, s_t


def _mlstm_body(p_ref, gc_ref, gt_ref, convp_ref, c0_ref, n0_ref, m0_ref, cw_ref, cb_ref, wq_ref, wk_ref,
                bias_r_ref, bias_c_ref, ng_ref, skip_ref, tril_ref, triu_ref,
                o_ref, ct_ref, nt_ref, mt_ref, ext_ref, c_ref, n_ref, m_ref, *, chunk, nb):
    L = chunk
    N = HEAD_DIM
    H = ML_HEADS

    @pl.when(pl.program_id(1) == 0)
    def _():
        ext_ref[:, 0:8, :] = convp_ref[...]
        c_ref[...] = c0_ref[...]
        n_ref[...] = n0_ref[...]
        m_ref[...] = m0_ref[...]

    cw = cw_ref[...]
    pre = []
    for bi in range(nb):
        p = p_ref[bi]
        x = p[:, 0:ML_WIDTH]
        ext_ref[bi, 8:8 + L, :] = x
        conv = (cb_ref[...] + cw[3:4, :] * x + cw[2:3, :] * ext_ref[bi, 7:7 + L, :]
                + cw[1:2, :] * ext_ref[bi, 6:6 + L, :] + cw[0:1, :] * ext_ref[bi, 5:5 + L, :])
        tail = ext_ref[bi, L:L + 8, :]
        ext_ref[bi, 0:8, :] = tail
        ca = conv * _sigmoid(conv)
        gcb = gc_ref[bi] + bias_r_ref[...]
        gtb = gt_ref[bi, 0] + bias_c_ref[...][:, 0:1]
        pre.append(dict(
            v=p[:, ML_WIDTH:2 * ML_WIDTH], o_pre=p[:, 2 * ML_WIDTH:3 * ML_WIDTH], ca=ca,
            q=_dot(ca, wq_ref[...]), k=_dot(ca, wk_ref[...]) * (N ** -0.5), gcb=gcb, gtb=gtb,
            bcum_c=_dot_sel(tril_ref[...], -_softplus(-gcb)),
            bcum_r=_dot_x_sel(-_softplus(-gtb), triu_ref[...])))

    ti = lax.broadcasted_iota(jnp.int32, (L, L), 0)
    si = lax.broadcasted_iota(jnp.int32, (L, L), 1)
    causal = si <= ti

    units = [(bi, h) for bi in range(nb) for h in range(H)]
    col = lambda name, u: pre[u[0]][name][:, u[1] * N:(u[1] + 1) * N]
    qh = [col('q', u) for u in units]
    kh = [col('k', u) for u in units]
    vh = [col('v', u) for u in units]
    cs = [c_ref[bi, h] for bi, h in units]
    ns = [n_ref[bi, h:h + 1, :] for bi, h in units]
    m_prev = [m_ref[bi, h:h + 1, 0:1] for bi, h in units]
    b_c = [pre[bi]['bcum_c'][:, H + h:H + h + 1] for bi, h in units]
    i_c = [pre[bi]['gcb'][:, h:h + 1] for bi, h in units]
    b_r = [pre[bi]['bcum_r'][H + h:H + h + 1, :] for bi, h in units]
    i_r = [pre[bi]['gtb'][h:h + 1, :] for bi, h in units]
    qk = [_dot_nt(a, b) for a, b in zip(qh, kh)]
    qc = [_dot_nt(a, c) for a, c in zip(qh, cs)]
    dmat = [jnp.where(causal, bc - br + ir, -jnp.inf) for bc, br, ir in zip(b_c, b_r, i_r)]
    inter = [bc + mp for bc, mp in zip(b_c, m_prev)]
    m_t = [jnp.maximum(it, jnp.max(dm, axis=-1, keepdims=True)) for it, dm in zip(inter, dmat)]
    a = [x * jnp.exp(dm - mt) for x, dm, mt in zip(qk, dmat, m_t)]
    av = [_dot(x, v_) for x, v_ in zip(a, vh)]
    b_end = [bc[L - 1:L, :] for bc in b_c]
    m_new = [jnp.maximum(be + mp, jnp.max(be - br + ir, axis=-1, keepdims=True))
             for be, mp, br, ir in zip(b_end, m_prev, b_r, i_r)]
    dec = [jnp.exp(be + mp - mn) for be, mp, mn in zip(b_end, m_prev, m_new)]
    wg = [jnp.exp(be - bc + ic - mn) for be, bc, ic, mn in zip(b_end, b_c, i_c, m_new)]
    c_new = [d * c + _dot_tn(w * v_, k_) for d, c, w, v_, k_ in zip(dec, cs, wg, vh, kh)]
    for idx, (bi, h) in enumerate(units):
        sl = slice(h * N, (h + 1) * N)
        sc = jnp.exp(inter[idx] - m_t[idx])
        num = sc * qc[idx] + av[idx]
        den = (sc * jnp.sum(qh[idx] * ns[idx], axis=-1, keepdims=True)
               + jnp.sum(a[idx], axis=-1, keepdims=True))
        hh = num / jnp.maximum(jnp.abs(den), jnp.exp(-m_t[idx]))
        c_ref[bi, h] = c_new[idx]
        n_ref[bi, h:h + 1, :] = dec[idx] * ns[idx] + jnp.sum(wg[idx] * kh[idx], axis=0, keepdims=True)
        m_ref[bi, h:h + 1, :] = jnp.broadcast_to(m_new[idx], (1, LANES))
        hn = hh * lax.rsqrt(jnp.mean(hh * hh, axis=-1, keepdims=True) + NORM_EPS)
        o_ref[bi, :, sl] = ((hn * ng_ref[:, sl] + skip_ref[:, sl] * pre[bi]['ca'][:, sl])
                            * _sigmoid(pre[bi]['o_pre'][:, sl]))

    @pl.when(pl.program_id(1) == pl.num_programs(1) - 1)
    def _():
        ct_ref[...] = c_ref[...]
        nt_ref[...] = n_ref[...]
        mt_ref[...] = m_ref[...]


def _block_diag(w):
    H, N, _ = w.shape
    eye = jnp.eye(H, dtype=w.dtype)
    return (eye[:, None, :, None] * w[:, :, None, :]).reshape(H * N, H * N)


def _mlstm(p_ml, gates, conv_prev, c0, n0, m0, W):
    B, T, _ = p_ml.shape
    H, N = ML_HEADS, HEAD_DIM
    L = _scan_chunk(T, ML_CHUNKS)
    nc = T // L
    gt = jnp.swapaxes(gates[:, :, :2 * H].reshape(B, nc, L, 2 * H), 2, 3)
    convp = jnp.concatenate([jnp.zeros((B, 8 - (ML_CONV - 1), ML_WIDTH), F32), conv_prev], axis=1)
    m0b = jnp.broadcast_to(jnp.pad(m0, ((0, 0), (0, 8 - H)))[:, :, None], (B, 8, LANES))
    bias = W['ml_gate_b'].reshape(2 * H)
    bias_r = jnp.pad(bias, (0, LANES - 2 * H)).reshape(1, LANES)
    bias_c = jnp.broadcast_to(bias[:, None], (2 * H, LANES))
    tril = (jnp.arange(L)[:, None] >= jnp.arange(L)[None, :]).astype(BF16)
    row = lambda z: z.reshape(1, -1).astype(F32)
    consts = [W['ml_conv_w'], row(W['ml_conv_b']), _block_diag(W['ml_wq']).astype(BF16),
              _block_diag(W['ml_wk']).astype(BF16), bias_r, bias_c, row(W['ml_norm_g']), row(W['ml_skip']),
              tril, tril.T]
    nb = _row_tile(B, ML_CHAIN_BATCH)
    per_b = lambda *shape: pl.BlockSpec((nb,) + shape, lambda b, c: (b,) + (0,) * len(shape))
    out, c_t, n_t, m_t = pl.pallas_call(
        functools.partial(_mlstm_body, chunk=L, nb=nb),
        grid=(B // nb, nc),
        in_specs=[pl.BlockSpec((nb, L, 3 * ML_WIDTH), lambda b, c: (b, c, 0)),
                  pl.BlockSpec((nb, L, LANES), lambda b, c: (b, c, 0)),
                  pl.BlockSpec((nb, 1, 2 * H, L), lambda b, c: (b, c, 0, 0)),
                  per_b(8, ML_WIDTH), per_b(H, N, N), per_b(H, N), per_b(8, LANES)]
                 + [_const_spec(c.shape) for c in consts],
        out_specs=[pl.BlockSpec((nb, L, ML_WIDTH), lambda b, c: (b, c, 0)),
                   per_b(H, N, N), per_b(H, N), per_b(8, LANES)],
        out_shape=[jax.ShapeDtypeStruct((B, T, ML_WIDTH), F32), jax.ShapeDtypeStruct((B, H, N, N), F32),
                   jax.ShapeDtypeStruct((B, H, N), F32), jax.ShapeDtypeStruct((B, 8, LANES), F32)],
        scratch_shapes=[pltpu.VMEM((nb, L + 8, ML_WIDTH), F32), pltpu.VMEM((nb, H, N, N), F32),
                        pltpu.VMEM((nb, H, N), F32), pltpu.VMEM((nb, 8, LANES), F32)],
        compiler_params=pltpu.CompilerParams(dimension_semantics=("parallel", "arbitrary"),
                                             vmem_limit_bytes=VMEM_LIMIT),
        name="mlstm",
    )(p_ml, gates, gt, convp, c0, n0, m0b, *consts)
    return out, c_t, n_t, m_t[:, :H, 0]


def _compress_body(*refs, n_parts, paged):
    if paged:
        refs = refs[1:]
    part_refs = refs[:n_parts]
    wpos_ref, sel_ref, wc_ref, gk_ref, avg_ref, o_ref = refs[n_parts:]
    wpos = wpos_ref[...]
    parts = []
    for r in part_refs:
        x = r[...]
        parts += [x[:, c * LANES:(c + 1) * LANES] * wpos for c in range(x.shape[1] // LANES)]
    hi, lo = _split2(jnp.concatenate(parts, axis=1))
    sel = sel_ref[...]
    c = jnp.dot(hi, sel, preferred_element_type=F32) + jnp.dot(lo, sel, preferred_element_type=F32)
    c = _dot_hi(wc_ref[...], c)
    ck = c[:NSA_KV]
    o_ref[:NSA_KV, :] = ck * lax.rsqrt(_seg_mean_sq_rows(ck, avg_ref[...]) + NORM_EPS) * gk_ref[...]
    o_ref[NSA_KV:, :] = c[NSA_KV:]


def _compress_consts(W, n_rows):
    w = 2 * NSA_KV
    wpos = W['nsa_wpos'].reshape(NSA_BLOCK, w).T
    wc_t = _block_diag(W['nsa_wc'].reshape(2 * NSA_KV_HEADS, HEAD_DIM, HEAD_DIM)).T
    gk = jnp.tile(W['nsa_qk_g'][1], NSA_KV_HEADS).reshape(NSA_KV, 1)
    sel = (jnp.arange(n_rows)[:, None] // NSA_BLOCK == jnp.arange(n_rows // NSA_BLOCK)[None, :]).astype(BF16)
    return [jnp.concatenate([wpos, wpos], axis=1), sel, wc_t, gk, _seg_avg_matrix(NSA_KV, 1.0 / HEAD_DIM)]


def _nsa_compress_dense(kvc_t, W):
    B, w, T = kvc_t.shape
    consts = _compress_consts(W, T)
    return pl.pallas_call(
        functools.partial(_compress_body, n_parts=1, paged=False),
        grid=(B,),
        in_specs=[pl.BlockSpec((None, w, T), lambda b: (b, 0, 0))] + [_const_spec(c.shape) for c in consts],
        out_specs=pl.BlockSpec((None, w, T // NSA_BLOCK), lambda b: (b, 0, 0)),
        out_shape=jax.ShapeDtypeStruct((B, w, T // NSA_BLOCK), F32),
        compiler_params=pltpu.CompilerParams(dimension_semantics=("parallel",), vmem_limit_bytes=VMEM_LIMIT),
        name="nsa_compress_dense",
    )(kvc_t, *consts)


def _nsa_compress_paged(pool_t, page_table, W):
    B, n_pages = page_table.shape
    w = pool_t.shape[1]
    pages = min(64, n_pages)
    bpp = PAGE_SIZE // NSA_BLOCK
    consts = _compress_consts(W, pages * PAGE_SIZE)

    def page_spec(k):
        return pl.BlockSpec((None, w, PAGE_SIZE), lambda b, i, pt: (pt[b * n_pages + i * pages + k], 0, 0))

    const = lambda c: pl.BlockSpec(c.shape, lambda b, i, pt: (0,) * c.ndim)
    return pl.pallas_call(
        functools.partial(_compress_body, n_parts=pages, paged=True),
        grid_spec=pltpu.PrefetchScalarGridSpec(
            num_scalar_prefetch=1,
            grid=(B, n_pages // pages),
            in_specs=[page_spec(k) for k in range(pages)] + [const(c) for c in consts],
            out_specs=pl.BlockSpec((None, w, pages * bpp), lambda b, i, pt: (b, 0, i)),
        ),
        out_shape=jax.ShapeDtypeStruct((B, w, n_pages * bpp), F32),
        compiler_params=pltpu.CompilerParams(dimension_semantics=("parallel", "parallel"),
                                             vmem_limit_bytes=VMEM_LIMIT),
        name="nsa_compress_paged",
    )(page_table.reshape(-1), *([pool_t] * pages), *consts)


def _split_cmp(cmp_t):
    return jnp.swapaxes(cmp_t[:, :NSA_KV], 1, 2), cmp_t[:, NSA_KV:]


def _gate_bias_row(gate_b):
    return jnp.pad(gate_b, (2 * ML_HEADS, LANES - 2 * ML_HEADS - 3 * NSA_HEADS)).reshape(1, LANES)


def _gate_lane(g, r):
    return 2 * ML_HEADS + (g * NSA_GROUP + r) * 3


def _attend_t(k_tile, v_t, q_all, bias, carry):
    tk = k_tile.shape[0]
    ones = jnp.ones((8, tk), BF16)
    s = [jnp.dot(k_tile, q, preferred_element_type=F32) + jnp.concatenate([b] * NSA_GROUP, axis=1)
         for q, b in zip(q_all, bias)]
    m_new = [jnp.maximum(m, jnp.max(x, axis=0, keepdims=True)) for x, (m, _) in zip(s, carry)]
    p = [jnp.exp2(x - m).astype(BF16) for x, m in zip(s, m_new)]
    pv = [jnp.dot(jnp.concatenate([v.astype(BF16), ones], axis=0), x, preferred_element_type=F32)
          for v, x in zip(v_t, p)]
    return tuple((mn, jnp.exp2(m - mn) * acc + x) for mn, (m, acc), x in zip(m_new, carry, pv))


def _attend_t_init(groups, width):
    return tuple((jnp.full((1, width), NEG, F32), jnp.zeros((HEAD_DIM + 8, width), F32)) for _ in range(groups))


def _attend_t_finish(carry):
    return [acc[:HEAD_DIM] / jnp.maximum(acc[HEAD_DIM:HEAD_DIM + 1], 1e-30) for _, acc in carry]


def _nsa_prompt_body(q_ref, gates_ref, gb_ref, ck_ref, cv_ref, ks_ref, vs_ref, kw_ref, vw_ref, o_ref,
                     sel_ref, ot_ref, *, tq, tk, n_blocks):
    R, N, G = NSA_GROUP, HEAD_DIM, NSA_KV_HEADS
    W = R * tq
    i = pl.program_id(1)
    t0 = i * tq
    qpos = t0 + lax.broadcasted_iota(jnp.int32, (1, tq), 1)
    q_t = q_ref[...].T * (N ** -0.5)
    gt_t = _sigmoid((gates_ref[...] + gb_ref[...]).T)
    ck = ck_ref[...]
    nc = ck.shape[0]
    zeros = jnp.zeros((N, W), F32)
    q_f = []
    for g in range(G):
        hs = jnp.concatenate([q_t[(g * R + r) * N:(g * R + r + 1) * N] for r in range(R)], axis=1)
        q_f.append(jnp.concatenate([hs, zeros] if g == 0 else [zeros, hs], axis=0))
    q_b = [(x * LOG2_E).astype(BF16) for x in q_f]

    blk_w = lax.broadcasted_iota(jnp.int32, (nc, W), 0)
    vis = (blk_w + 1) * NSA_BLOCK - 1 <= jnp.concatenate([qpos] * R, axis=1)
    o_c, imp = [], []
    for g in range(G):
        s = jnp.where(vis, _dot_hi(ck, q_f[g]), -jnp.inf)
        m = jnp.max(s, axis=0, keepdims=True)
        e = jnp.exp(s - jnp.where(m == -jnp.inf, 0.0, m))
        p = e / jnp.maximum(jnp.sum(e, axis=0, keepdims=True), 1e-30)
        o_c.append(_dot(cv_ref[g * N:(g + 1) * N, :], p))
        pg = p[:, 0:tq]
        for r in range(1, R):
            pg = pg + p[:, r * tq:(r + 1) * tq]
        imp.append(pg)
    imp = jnp.concatenate(imp, axis=1)

    blk_r = lax.broadcasted_iota(jnp.int32, (nc, G * tq), 0)
    cur = jnp.right_shift(jnp.concatenate([qpos] * G, axis=1), BLOCK_SHIFT)
    forced = (blk_r == 0) | (blk_r == cur) | (blk_r == cur - 1)
    score = jnp.where(forced, BIG, jnp.where(blk_r <= cur, imp, -BIG))
    rowf = blk_r.astype(F32)
    sel_bias = jnp.full((nc, G * tq), NEG, F32)
    for _ in range(min(NSA_TOPN, n_blocks)):
        m = jnp.max(score, axis=0, keepdims=True)
        first = jnp.min(jnp.where(score == m, rowf, float(nc)), axis=0, keepdims=True)
        hit = rowf == first
        sel_bias = jnp.where(hit, 0.0, sel_bias)
        score = jnp.where(hit, -jnp.inf, score)
    sel_ref[...] = sel_bias

    kpos_col = lax.broadcasted_iota(jnp.int32, (tk, 1), 0)
    bpt = tk // NSA_BLOCK

    def sel_tile(kt, carry):
        k0 = pl.multiple_of(kt * tk, tk)
        chosen = jnp.concatenate(
            [jnp.broadcast_to(sel_ref[pl.ds(kt * bpt + m, 1), :], (NSA_BLOCK, G * tq)) for m in range(bpt)], axis=0)
        causal = jnp.where(k0 + kpos_col <= qpos, 0.0, NEG)
        bias = [chosen[:, g * tq:(g + 1) * tq] + causal for g in range(G)]
        v_t = [vs_ref[g * N:(g + 1) * N, pl.ds(k0, tk)] for g in range(G)]
        return _attend_t(ks_ref[pl.ds(k0, tk), :], v_t, q_b, bias, carry)

    n_kt = (t0 + tq + tk - 1) // tk
    o_s = _attend_t_finish(lax.fori_loop(0, n_kt, sel_tile, _attend_t_init(G, W)))

    tw = min(NSA_WINDOW + tq, kw_ref.shape[0])
    w0 = pl.multiple_of(jnp.maximum(t0 + tq - tw, 0), tq)
    d = qpos - (w0 + lax.broadcasted_iota(jnp.int32, (tw, 1), 0))
    bias = jnp.where((d >= 0) & (d < NSA_WINDOW), 0.0, NEG)
    v_t = [vw_ref[g * N:(g + 1) * N, pl.ds(w0, tw)] for g in range(G)]
    o_w = _attend_t_finish(_attend_t(kw_ref[pl.ds(w0, tw), :], v_t, q_b, [bias] * G, _attend_t_init(G, W)))

    for g in range(G):
        for r in range(R):
            c0 = _gate_lane(g, r)
            h = g * R + r
            cols = slice(r * tq, (r + 1) * tq)
            ot_ref[h * N:(h + 1) * N, :] = (gt_t[c0:c0 + 1] * o_c[g][:, cols] + gt_t[c0 + 1:c0 + 2] * o_s[g][:, cols]
                                            + gt_t[c0 + 2:c0 + 3] * o_w[g][:, cols])
    o_ref[...] = ot_ref[...].T


def _nsa_prompt(q, gates, ck, cv_t, ks, kvs_t, kw, kvw_t, W):
    B, T, _ = q.shape
    tq = min(128, T)
    tk = min(512, T)
    nc = ck.shape[1]
    per_b = lambda a: pl.BlockSpec((None,) + a.shape[1:], lambda b, i: (b,) + (0,) * (a.ndim - 1))
    tile = lambda w: pl.BlockSpec((None, tq, w), lambda b, i: (b, i, 0))
    v_rows = pl.BlockSpec((None, NSA_KV, T), lambda b, i: (b, 1, 0))
    gb = _gate_bias_row(W['nsa_gate_b'])
    return pl.pallas_call(
        functools.partial(_nsa_prompt_body, tq=tq, tk=tk, n_blocks=T // NSA_BLOCK),
        grid=(B, T // tq),
        in_specs=[tile(NSA_WIDTH), tile(LANES), pl.BlockSpec((1, LANES), lambda b, i: (0, 0)),
                  per_b(ck), per_b(cv_t), per_b(ks), v_rows, per_b(kw), v_rows],
        out_specs=tile(NSA_WIDTH),
        out_shape=jax.ShapeDtypeStruct((B, T, NSA_WIDTH), F32),
        scratch_shapes=[pltpu.VMEM((nc, NSA_KV_HEADS * tq), F32), pltpu.VMEM((NSA_WIDTH, tq), F32)],
        compiler_params=pltpu.CompilerParams(dimension_semantics=("parallel", "arbitrary"),
                                             vmem_limit_bytes=VMEM_LIMIT),
        name="nsa_prompt",
    )(q, gates, gb, ck, cv_t, ks, kvs_t, kw, kvw_t)


def _stack_heads(qg):
    return jnp.concatenate([qg[:, r * HEAD_DIM:(r + 1) * HEAD_DIM] for r in range(NSA_GROUP)],
                           axis=0) * (HEAD_DIM ** -0.5)


def _tile_rows(x, n):
    return jnp.concatenate([x] * n, axis=0)


def _masked_softmax(s, mask):
    s = jnp.where(mask, s, -jnp.inf)
    m = jnp.max(s, axis=-1, keepdims=True)
    m = jnp.where(m == -jnp.inf, 0.0, m)
    e = jnp.exp(s - m)
    return e / jnp.maximum(jnp.sum(e, axis=-1, keepdims=True), 1e-30)


def _select_blocks(imp, qpos, n_blocks, k):
    t, w = imp.shape
    jf = lax.broadcasted_iota(jnp.int32, (t, w), 1)
    cur = jnp.right_shift(qpos, BLOCK_SHIFT)
    forced = (jf == 0) | (jf == cur) | (jf == cur - 1)
    score = jnp.where(forced, BIG, jnp.where(jf <= cur, imp, -BIG))
    score = jnp.where(jf < n_blocks, score, -jnp.inf)
    lane = jf.astype(F32)
    idx = []
    for _ in range(k):
        m = jnp.max(score, axis=-1, keepdims=True)
        i = jnp.min(jnp.where(score == m, lane, float(w)), axis=-1, keepdims=True)
        score = jnp.where(lane == i, -jnp.inf, score)
        idx.append(i)
    return idx


def _online_step(carry, s, ok, pv):
    m, l, acc = carry
    s = jnp.where(ok, s, -jnp.inf)
    m_new = jnp.maximum(m, jnp.max(s, axis=-1, keepdims=True))
    m_safe = jnp.where(m_new == -jnp.inf, 0.0, m_new)
    p = jnp.exp(s - m_safe)
    alpha = jnp.exp(m - m_safe)
    return m_new, alpha * l + jnp.sum(p, axis=-1, keepdims=True), alpha * acc + pv(p)


def _online_init(rows):
    return (jnp.full((rows, 1), -jnp.inf, F32), jnp.zeros((rows, 1), F32), jnp.zeros((rows, HEAD_DIM), F32))


def _online_finish(carry):
    _, l, acc = carry
    return acc / jnp.maximum(l, 1e-30)


def _nsa_sample_cmp_body(q_ref, ck_ref, cv_ref, kvw_ref, winp_ref, oc_ref, ow_ref, idx_ref, *, pos0, n_blocks):
    R, N = NSA_GROUP, HEAD_DIM
    t = q_ref.shape[0]
    wb = winp_ref.shape[1]
    qpos = pos0 + lax.broadcasted_iota(jnp.int32, (t, 1), 0)
    qpos_s = _tile_rows(qpos, R)
    q = q_ref[...]
    ck = ck_ref[...]
    nc = ck.shape[0]
    w_sel = -(-n_blocks // LANES) * LANES
    new = kvw_ref[...]
    oc_heads, ow_heads = [], []
    for g in range(NSA_KV_HEADS):
        krows = slice(g * N, (g + 1) * N)
        vrows = slice(NSA_KV + g * N, NSA_KV + (g + 1) * N)
        qs = _stack_heads(q[:, g * R * N:(g + 1) * R * N])
        blk = lax.broadcasted_iota(jnp.int32, (1, nc), 1)
        p = _masked_softmax(_dot_nt_hi(qs, ck[:, krows]), (blk + 1) * NSA_BLOCK - 1 <= qpos_s)
        o_c = _dot_nt(p, cv_ref[krows, :])
        imp = p[0:t]
        for r in range(1, R):
            imp = imp + p[r * t:(r + 1) * t]
        if w_sel > nc:
            imp = jnp.concatenate([imp, jnp.zeros((t, w_sel - nc), F32)], axis=1)
        idx = _select_blocks(imp, qpos, n_blocks, min(NSA_TOPN, n_blocks))
        lane = lax.broadcasted_iota(jnp.int32, (t, LANES), 1)
        tile = jnp.zeros((t, LANES), F32)
        for kk, col in enumerate(idx):
            tile = jnp.where(lane == kk, col, tile)
        idx_ref[g] = tile.astype(jnp.int32)

        carry = _online_init(R * t)
        d = qpos_s - (pos0 - wb + lax.broadcasted_iota(jnp.int32, (1, wb), 1))
        carry = _online_step(carry, _dot(qs, winp_ref[krows, :]), (d >= 0) & (d < NSA_WINDOW),
                             lambda p: _dot_nt(p, winp_ref[vrows, :]))
        d = qpos_s - (pos0 + lax.broadcasted_iota(jnp.int32, (1, t), 1))
        carry = _online_step(carry, _dot_nt(qs, new[:, krows]), (d >= 0) & (d < NSA_WINDOW),
                             lambda p: _dot(p, new[:, vrows]))
        o_w = _online_finish(carry)
        oc_heads += [o_c[r * t:(r + 1) * t] for r in range(R)]
        ow_heads += [o_w[r * t:(r + 1) * t] for r in range(R)]
    oc_ref[...] = jnp.concatenate(oc_heads, axis=1)
    ow_ref[...] = jnp.concatenate(ow_heads, axis=1)


def _nsa_sample_cmp(q, ck, cv_t, kv_w, win_prev_t, pos0, n_blocks):
    B, T, _ = q.shape
    per_b = lambda a: pl.BlockSpec((None,) + a.shape[1:], lambda b: (b,) + (0,) * (a.ndim - 1))
    out_b = lambda *s: pl.BlockSpec((None,) + s, lambda b: (b,) + (0,) * len(s))
    return pl.pallas_call(
        functools.partial(_nsa_sample_cmp_body, pos0=pos0, n_blocks=n_blocks),
        grid=(B,),
        in_specs=[per_b(q), per_b(ck), per_b(cv_t), per_b(kv_w), per_b(win_prev_t)],
        out_specs=[out_b(T, NSA_WIDTH), out_b(T, NSA_WIDTH), out_b(NSA_KV_HEADS, T, LANES)],
        out_shape=[jax.ShapeDtypeStruct((B, T, NSA_WIDTH), F32), jax.ShapeDtypeStruct((B, T, NSA_WIDTH), F32),
                   jax.ShapeDtypeStruct((B, NSA_KV_HEADS, T, LANES), jnp.int32)],
        compiler_params=pltpu.CompilerParams(dimension_semantics=("parallel",)),
        name="nsa_sample_cmp",
    )(q, ck, cv_t, kv_w, win_prev_t)


def _nsa_sample_sel_body(idx_ref, pt_ref, *refs, n_past, pos0, topk):
    R, N, G = NSA_GROUP, HEAD_DIM, NSA_KV_HEADS
    page_refs = refs[:G * topk]
    q_ref, new_ref, oc_ref, ow_ref, gates_ref, gb_ref, o_ref = refs[G * topk:]
    b = pl.program_id(0)
    t = pl.program_id(1)
    n_t = pl.num_programs(1)
    qpos = pos0 + t
    gt = _sigmoid(gates_ref[pl.ds(t, 1), :] + gb_ref[...])
    q = q_ref[pl.ds(t, 1), :]
    o_c = oc_ref[pl.ds(t, 1), :]
    o_w = ow_ref[pl.ds(t, 1), :]
    lane = lax.broadcasted_iota(jnp.int32, (1, PAGE_SIZE), 1)
    heads = []
    for g in range(G):
        qs = _stack_heads(q[:, g * R * N:(g + 1) * R * N])
        new_k = new_ref[g * N:(g + 1) * N, :]
        new_v = new_ref[NSA_KV + g * N:NSA_KV + (g + 1) * N, :]
        scores, vals = [], []
        for k in range(topk):
            j = idx_ref[((b * n_t + t) * G + g) * topk + k]
            half = j % (PAGE_SIZE // NSA_BLOCK)
            is_new = j >= n_past
            page = page_refs[g * topk + k]
            k_t = jnp.where(is_new, new_k, page[g * N:(g + 1) * N, :])
            v_t = jnp.where(is_new, new_v, page[NSA_KV + g * N:NSA_KV + (g + 1) * N, :])
            kpos = (j - half) * NSA_BLOCK + lane
            ok = (jnp.right_shift(lane, BLOCK_SHIFT) == half) & (kpos <= qpos)
            scores.append(jnp.where(ok, _dot(qs, k_t), -jnp.inf))
            vals.append(v_t)
        m = scores[0].max(axis=-1, keepdims=True)
        for s in scores[1:]:
            m = jnp.maximum(m, s.max(axis=-1, keepdims=True))
        m = jnp.where(m == -jnp.inf, 0.0, m)
        l = jnp.zeros((R, 1), F32)
        acc = jnp.zeros((R, N), F32)
        for s, v_t in zip(scores, vals):
            e = jnp.exp(s - m)
            l = l + jnp.sum(e, axis=-1, keepdims=True)
            acc = acc + _dot_nt(e, v_t)
        o_s = acc / jnp.maximum(l, 1e-30)
        for r in range(R):
            c0 = (g * R + r) * N
            gl = _gate_lane(g, r)
            heads.append(gt[:, gl:gl + 1] * o_c[:, c0:c0 + N] + gt[:, gl + 1:gl + 2] * o_s[r:r + 1]
                         + gt[:, gl + 2:gl + 3] * o_w[:, c0:c0 + N])
    o_ref[pl.ds(t, 1), :] = jnp.concatenate(heads, axis=1)


def _nsa_sample_sel(q, idx, page_table, pool_t, kv_s_new, o_c, o_w, gates, W, pos0):
    B, T, _ = q.shape
    G, N = NSA_KV_HEADS, HEAD_DIM
    topk = idx.shape[-1]
    n_pages = page_table.shape[1]
    bpp = PAGE_SIZE // NSA_BLOCK
    n_past = pos0 // NSA_BLOCK
    assert n_past % bpp == 0 and T <= NSA_BLOCK
    new_t = jnp.swapaxes(jnp.pad(kv_s_new, ((0, 0), (0, PAGE_SIZE - T), (0, 0))), 1, 2)

    def page_spec(g, k):
        def index(b, t, idx_ref, pt_ref):
            j = jnp.minimum(idx_ref[((b * T + t) * G + g) * topk + k], n_past - 1)
            return (pt_ref[b * n_pages + j // bpp], 0, 0)
        return pl.BlockSpec((None,) + pool_t.shape[1:], index)

    per_b = lambda a: pl.BlockSpec((None,) + a.shape[1:], lambda b, t, i, p: (b,) + (0,) * (a.ndim - 1))
    gb = _gate_bias_row(W['nsa_gate_b'])
    kv_specs = [page_spec(g, k) for g in range(G) for k in range(topk)]
    return pl.pallas_call(
        functools.partial(_nsa_sample_sel_body, n_past=n_past, pos0=pos0, topk=topk),
        grid_spec=pltpu.PrefetchScalarGridSpec(
            num_scalar_prefetch=2,
            grid=(B, T),
            in_specs=kv_specs + [per_b(q), per_b(new_t), per_b(o_c), per_b(o_w), per_b(gates),
                                 pl.BlockSpec((1, LANES), lambda b, t, i, p: (0, 0))],
            out_specs=pl.BlockSpec((None, T, NSA_WIDTH), lambda b, t, i, p: (b, 0, 0)),
        ),
        out_shape=jax.ShapeDtypeStruct((B, T, NSA_WIDTH), F32),
        compiler_params=pltpu.CompilerParams(dimension_semantics=("parallel", "arbitrary")),
        name="nsa_sample_sel",
    )(idx.reshape(-1), page_table.reshape(-1), *([pool_t] * len(kv_specs)), q, new_t, o_c, o_w, gates, gb)


def _nsa_sample_mixer(q, gates, kv_s, kv_w, page_table, pool_cmp_t, pool_sel_t, win_prev_t, W):
    B, T, _ = q.shape
    pos0 = page_table.shape[1] * PAGE_SIZE
    n_blocks = -(-(pos0 + T) // NSA_BLOCK)
    ck, cv_t = _split_cmp(_nsa_compress_paged(pool_cmp_t, page_table, W))
    o_c, o_w, idx = _nsa_sample_cmp(q, ck, cv_t, kv_w, win_prev_t, pos0, n_blocks)
    idx = jnp.swapaxes(idx[..., :min(NSA_TOPN, n_blocks)], 1, 2)
    return _nsa_sample_sel(q, idx, page_table, pool_sel_t, kv_s, o_c, o_w, gates, W, pos0)


def _mem_kv_body(x_ref, g_ref, w_ref, gk_ref, avg_ref, o_ref):
    h = _rms_rows(x_ref[...], g_ref[...]).astype(BF16)
    kv = lax.dot_general(w_ref[...], h, (((1,), (1,)), ((), ())), preferred_element_type=F32)
    k = kv[:MEM_WIDTH]
    o_ref[:MEM_WIDTH, :] = k * lax.rsqrt(_seg_mean_sq_rows(k, avg_ref[...]) + NORM_EPS) * gk_ref[...]
    o_ref[MEM_WIDTH:, :] = kv[MEM_WIDTH:]


def _mem_kv(mem, g, w_kv, k_g):
    B, m, d = mem.shape
    consts = [g.reshape(1, d), w_kv.T.astype(BF16), jnp.tile(k_g, MEM_HEADS).reshape(MEM_WIDTH, 1),
              _seg_avg_matrix(MEM_WIDTH, 1.0 / HEAD_DIM)]
    return pl.pallas_call(
        _mem_kv_body,
        grid=(B,),
        in_specs=[pl.BlockSpec((None, m, d), lambda b: (b, 0, 0))] + [_const_spec(c.shape) for c in consts],
        out_specs=pl.BlockSpec((None, 2 * MEM_WIDTH, m), lambda b: (b, 0, 0)),
        out_shape=jax.ShapeDtypeStruct((B, 2 * MEM_WIDTH, m), F32),
        compiler_params=pltpu.CompilerParams(dimension_semantics=("parallel",)),
        name="mem_kv",
    )(mem, *consts)


def _out_mem_body(x_ref, orw_ref, oml_ref, onsa_ref, kv_ref, w1_ref, w2_ref, w3_ref, g_ref, wq_ref, gq_ref,
                  avg_ref, wo_ref, o_ref):
    N = HEAD_DIM
    x = (x_ref[...] + _dot(orw_ref[...], w1_ref[...]) + _dot(oml_ref[...], w2_ref[...])
         + _dot(onsa_ref[...], w3_ref[...]))
    h = _rms_rows(x, g_ref[...]).astype(BF16)
    q = jnp.dot(h, wq_ref[...], preferred_element_type=F32)
    q = q * lax.rsqrt(_seg_mean_sq(q, avg_ref[...]) + NORM_EPS) * gq_ref[...] * (N ** -0.5)
    heads = []
    for hd in range(MEM_HEADS):
        s = _dot(q[:, hd * N:(hd + 1) * N], kv_ref[hd * N:(hd + 1) * N, :])
        e = jnp.exp(s - jnp.max(s, axis=-1, keepdims=True))
        p = e / jnp.sum(e, axis=-1, keepdims=True)
        heads.append(_dot_nt(p, kv_ref[MEM_WIDTH + hd * N:MEM_WIDTH + (hd + 1) * N, :]))
    o_ref[...] = x + _dot(jnp.concatenate(heads, axis=1), wo_ref[...])


def _out_mem(x, o_rw, o_ml, o_nsa, kv_t, w_out, g, w_q, q_g, w_o):
    B, T, d = x.shape
    tm = _row_tile(T, 512)
    w1 = w_out[:RW_WIDTH].astype(BF16)
    w2 = w_out[RW_WIDTH:RW_WIDTH + ML_WIDTH].astype(BF16)
    w3 = w_out[RW_WIDTH + ML_WIDTH:].astype(BF16)
    consts = [w1, w2, w3, g.reshape(1, d), w_q.astype(BF16), jnp.tile(q_g, MEM_HEADS).reshape(1, MEM_WIDTH),
              _seg_avg_matrix(MEM_WIDTH, 1.0 / HEAD_DIM), w_o.astype(BF16)]
    tile = lambda w: pl.BlockSpec((None, tm, w), lambda b, i: (b, i, 0))
    return pl.pallas_call(
        _out_mem_body,
        grid=(B, T // tm),
        in_specs=[tile(d), tile(RW_WIDTH), tile(ML_WIDTH), tile(NSA_WIDTH),
                  pl.BlockSpec((None,) + kv_t.shape[1:], lambda b, i: (b, 0, 0))]
                 + [_const_spec(c.shape) for c in consts],
        out_specs=tile(d),
        out_shape=jax.ShapeDtypeStruct((B, T, d), F32),
        compiler_params=pltpu.CompilerParams(dimension_semantics=("parallel", "parallel"),
                                             vmem_limit_bytes=VMEM_LIMIT),
        name="out_mem",
    )(x, o_rw, o_ml, o_nsa, kv_t, *consts)


def _layer(x, W, st, page_table, mem):
    B, T, d = x.shape
    is_prompt = st is None
    g = W['norm_g']
    G = NSA_KV_HEADS
    x1 = _ffn(x.reshape(B * T, d), g[0], W['ffa_up'], W['ffa_down']).reshape(B, T, d)
    proj = _in_proj(x1, g[1], W['w_in_perm'], W['nsa_qk_g'], channel_major=is_prompt)
    p_rw, p_ml, gates, q = proj[:4]
    if is_prompt:
        zeros = lambda *s: jnp.zeros(s, F32)
        st = {'rw_shift': zeros(B, 1, RW_COLS), 'rw_S': zeros(B, RW_HEADS, HEAD_DIM, HEAD_DIM),
              'ml_conv': zeros(B, ML_CONV - 1, ML_WIDTH), 'ml_C': zeros(B, ML_HEADS, HEAD_DIM, HEAD_DIM),
              'ml_n': zeros(B, ML_HEADS, HEAD_DIM), 'ml_m': zeros(B, ML_HEADS)}
        mem_kv_t = _mem_kv(mem, g[3], W['mem_w_kv'], W['mem_qk_g'][1])
    else:
        mem_kv_t = _channel_major(st['mem_kv'])
    o_rw, rw_S = _rwkv(p_rw, st['rw_shift'], st['rw_S'], W)
    o_ml, ml_C, ml_n, ml_m = _mlstm(p_ml, gates, st['ml_conv'], st['ml_C'], st['ml_n'], st['ml_m'], W)
    if is_prompt:
        ks, kw, kvc_t, kvs_t, kvw_t = proj[4:]
        ck, cv_t = _split_cmp(_nsa_compress_dense(kvc_t, W))
        o_nsa = _nsa_prompt(q, gates, ck, cv_t, ks, kvs_t, kw, kvw_t, W)
        new_kv = {'nsa_cmp': _token_major(kvc_t, 2, G), 'nsa_sel': _token_major(kvs_t, 2, G),
                  'nsa_win': _token_major(kvw_t[:, :, T - min(NSA_WINDOW, T):], 2, G)}
    else:
        kv_c, kv_s, kv_w = proj[4:]
        win_prev_t = _channel_major(st['nsa_win'])
        wb = win_prev_t.shape[2]
        o_nsa = _nsa_sample_mixer(q, gates, kv_s, kv_w, page_table, st['nsa_cmp_t'], st['nsa_sel_t'], win_prev_t, W)
        win_t = jnp.concatenate([win_prev_t, jnp.swapaxes(kv_w, 1, 2)], axis=2)
        kv5 = lambda z: z.reshape(B, T, 2, G, HEAD_DIM)
        new_kv = {'nsa_cmp': kv5(kv_c), 'nsa_sel': kv5(kv_s),
                  'nsa_win': _token_major(win_t[:, :, wb + T - min(NSA_WINDOW, wb + T):], 2, G)}
    x2 = _out_mem(x1, o_rw, o_ml, o_nsa, mem_kv_t, W['w_out'], g[2], W['mem_w_q'], W['mem_qk_g'][0], W['mem_w_o'])
    x3 = _ffn(x2.reshape(B * T, d), g[4], W['ffb_up'], W['ffb_down']).reshape(B, T, d)
    qk_in = p_ml[:, :, :ML_WIDTH]
    conv_all = jnp.concatenate([st['ml_conv'], qk_in], axis=1) if T < ML_CONV - 1 else qk_in
    new = dict(new_kv)
    new.update({'rw_shift': p_rw[:, T - 1:], 'rw_S': rw_S, 'ml_conv': conv_all[:, conv_all.shape[1] - (ML_CONV - 1):],
                'ml_C': ml_C, 'ml_n': ml_n, 'ml_m': ml_m})
    if is_prompt:
        new['mem_kv'] = _token_major(mem_kv_t, 2, MEM_HEADS)
    return x3, new


def kernel(x_prompt, x_sample, cache_nsa_cmp, cache_nsa_sel, cache_nsa_win, cache_mem_kv, state_rwkv_shift, state_rwkv_S, state_mlstm_conv, state_mlstm_C, state_mlstm_n, state_mlstm_m, page_table, mem_prompt, norm_g, ffa_up, ffa_down, ffb_up, ffb_down, w_in, w_out, rw_mu, rw_w0, rw_w2, rw_a0, rw_a2, rw_g2, rw_kk, rw_ka, rw_rk, rw_gn_g, rw_gn_b, ml_conv_w, ml_conv_b, ml_wq, ml_wk, ml_gate_b, ml_norm_g, ml_skip, nsa_qk_g, nsa_wpos, nsa_wc, nsa_gate_b, mem_w_q, mem_w_kv, mem_qk_g, mem_w_o):
    params = dict(norm_g=norm_g, ffa_up=ffa_up, ffa_down=ffa_down, ffb_up=ffb_up, ffb_down=ffb_down, w_out=w_out,
                  rw_mu=rw_mu, rw_w0=rw_w0, rw_w2=rw_w2, rw_a0=rw_a0, rw_a2=rw_a2, rw_g2=rw_g2, rw_kk=rw_kk,
                  rw_ka=rw_ka, rw_rk=rw_rk, rw_gn_g=rw_gn_g, rw_gn_b=rw_gn_b, ml_conv_w=ml_conv_w,
                  ml_conv_b=ml_conv_b, ml_wq=ml_wq, ml_wk=ml_wk, ml_gate_b=ml_gate_b, ml_norm_g=ml_norm_g,
                  ml_skip=ml_skip, nsa_qk_g=nsa_qk_g, nsa_wpos=nsa_wpos, nsa_wc=nsa_wc, nsa_gate_b=nsa_gate_b,
                  mem_w_q=mem_w_q, mem_w_kv=mem_w_kv, mem_qk_g=mem_qk_g, mem_w_o=mem_w_o)
    depth = norm_g.shape[0]
    y_p, y_s = x_prompt, x_sample
    new_p, new_s = [], []
    n_phys = cache_nsa_cmp.shape[1]
    all_pages = lambda c: _channel_major(c.reshape((depth * n_phys,) + c.shape[2:]))
    pool_cmp_t, pool_sel_t = all_pages(cache_nsa_cmp), all_pages(cache_nsa_sel)
    for l in range(depth):
        W = {name: v[l] for name, v in params.items()}
        W['w_in_perm'] = _permute_w_in(w_in[l])
        st = {'nsa_cmp_t': pool_cmp_t, 'nsa_sel_t': pool_sel_t, 'nsa_win': cache_nsa_win[l],
              'mem_kv': cache_mem_kv[l], 'rw_shift': state_rwkv_shift[l], 'rw_S': state_rwkv_S[l],
              'ml_conv': state_mlstm_conv[l], 'ml_C': state_mlstm_C[l], 'ml_n': state_mlstm_n[l],
              'ml_m': state_mlstm_m[l]}
        y_p, sp = _layer(y_p, W, None, None, mem_prompt)
        y_s, ss = _layer(y_s, W, st, page_table + l * n_phys, None)
        new_p.append(sp)
        new_s.append(ss)
    P = lambda name: jnp.stack([d[name] for d in new_p])
    S = lambda name: jnp.stack([d[name] for d in new_s])
    return (y_p, y_s,
            P('nsa_cmp'), S('nsa_cmp'), P('nsa_sel'), S('nsa_sel'), P('nsa_win'), S('nsa_win'),
            P('mem_kv'),
            P('rw_shift'), S('rw_shift'), P('rw_S'), S('rw_S'),
            P('ml_conv'), S('ml_conv'), P('ml_C'), S('ml_C'), P('ml_n'), S('ml_n'), P('ml_m'), S('ml_m'))
```

```python
import functools

import jax
import jax.numpy as jnp
from jax import lax
from jax.experimental import pallas as pl
from jax.experimental.pallas import tpu as pltpu

F32 = jnp.float32
BF16 = jnp.bfloat16

HEAD_DIM = 64
RW_HEADS = 4
RW_WIDTH = RW_HEADS * HEAD_DIM
RW_W_LORA = 64
RW_A_LORA = 64
RW_G_LORA = 128
RW_COLS = 3 * RW_WIDTH + RW_W_LORA + RW_A_LORA + RW_G_LORA
RW_GN_EPS = 64e-5
ML_HEADS = 4
ML_WIDTH = ML_HEADS * HEAD_DIM
ML_CONV = 4
RW_CHUNKS = (64,)
ML_CHUNKS = (128, 64)
ML_COLS = 3 * ML_WIDTH + 2 * ML_HEADS
NSA_HEADS = 8
NSA_KV_HEADS = 2
NSA_GROUP = NSA_HEADS // NSA_KV_HEADS
NSA_WIDTH = NSA_HEADS * HEAD_DIM
NSA_KV = NSA_KV_HEADS * HEAD_DIM
NSA_BLOCK = 64
BLOCK_SHIFT = 6
NSA_TOPN = 8
NSA_WINDOW = 512
NSA_COLS = NSA_WIDTH + 6 * NSA_KV + 3 * NSA_HEADS
MEM_HEADS = 4
MEM_WIDTH = MEM_HEADS * HEAD_DIM
PAGE_SIZE = 128
NORM_EPS = 1e-6
BIG = 1e9
NEG = -1e30
LOG2_E = 1.4426950408889634
RW_CHAIN_BATCH = 4
ML_CHAIN_BATCH = 4
LANES = 128
VMEM_LIMIT = 56 * 1024 * 1024

P_RW = 0
P_ML = P_RW + RW_COLS
P_Q = P_ML + 3 * ML_WIDTH
P_KVC = P_Q + NSA_WIDTH
P_KVS = P_KVC + 2 * NSA_KV
P_KVW = P_KVS + 2 * NSA_KV
P_GATES = P_KVW + 2 * NSA_KV
P_TOTAL = P_GATES + LANES


def _dot(a, b):
    return jnp.dot(a.astype(BF16), b.astype(BF16), preferred_element_type=F32)


def _dot_nt(a, b):
    return lax.dot_general(a.astype(BF16), b.astype(BF16), (((1,), (1,)), ((), ())),
                           preferred_element_type=F32)


def _dot_tn(a, b):
    return lax.dot_general(a.astype(BF16), b.astype(BF16), (((0,), (0,)), ((), ())),
                           preferred_element_type=F32)


def _split2(x):
    hi = x.astype(BF16)
    lo = (x - hi.astype(F32)).astype(BF16)
    return hi, lo


def _split3(x):
    hi = x.astype(BF16)
    r = x - hi.astype(F32)
    mid = r.astype(BF16)
    lo = (r - mid.astype(F32)).astype(BF16)
    return hi, mid, lo


def _dot_sel(sel, x):
    hi, mid, lo = _split3(x)
    s = sel.astype(BF16)
    d = lambda t: jnp.dot(s, t, preferred_element_type=F32)
    return d(hi) + d(mid) + d(lo)


def _dot_x_sel(x, sel):
    hi, mid, lo = _split3(x)
    s = sel.astype(BF16)
    d = lambda t: jnp.dot(t, s, preferred_element_type=F32)
    return d(hi) + d(mid) + d(lo)


def _dot_hi(a, b):
    ah, al = _split2(a)
    bh, bl = _split2(b)
    d = lambda u, v: jnp.dot(u, v, preferred_element_type=F32)
    return d(ah, bh) + d(al, bh) + d(ah, bl)


def _dot_nt_hi(a, b):
    ah, al = _split2(a)
    bh, bl = _split2(b)
    d = lambda u, v: lax.dot_general(u, v, (((1,), (1,)), ((), ())), preferred_element_type=F32)
    return d(ah, bh) + d(al, bh) + d(ah, bl)


def _seg_mean_sq(x, seg_avg):
    hi, lo = _split2(x * x)
    d = lambda t: jnp.dot(t, seg_avg, preferred_element_type=F32)
    return d(hi) + d(lo)


def _seg_mean_sq_rows(x, seg_avg):
    hi, lo = _split2(x * x)
    d = lambda t: jnp.dot(seg_avg, t, preferred_element_type=F32)
    return d(hi) + d(lo)


def _rms_rows(x, g):
    return x * lax.rsqrt(jnp.mean(x * x, axis=-1, keepdims=True) + NORM_EPS) * g


def _sigmoid(x):
    return 1.0 / (1.0 + jnp.exp(-x))


def _softplus(x):
    return jnp.maximum(x, 0.0) + jnp.log(1.0 + jnp.exp(-jnp.abs(x)))


def _row_tile(n, target):
    t = min(n, target)
    while n % t:
        t //= 2
    return t


def _scan_chunk(t, candidates):
    for c in candidates:
        if t % c == 0:
            return c
    return t


def _const_spec(shape):
    nd = len(shape)
    return pl.BlockSpec(shape, lambda *_: (0,) * nd, pipeline_mode=pl.Buffered(1))


def _seg_avg_matrix(width, scale):
    i = jnp.arange(width) // HEAD_DIM
    return (jnp.where(i[:, None] == i[None, :], scale, 0.0)).astype(BF16)


def _channel_major(x):
    b, t = x.shape[:2]
    return jnp.transpose(x, (0, 2, 3, 4, 1)).reshape(b, -1, t)


def _token_major(x, c, g):
    b, w, t = x.shape
    return jnp.transpose(x.reshape(b, c, g, w // (c * g), t), (0, 4, 1, 2, 3))


def _ffn_body(x_ref, g_ref, wg_ref, wu_ref, wd_ref, o_ref, *, f_chunk):
    x = x_ref[...]
    h = _rms_rows(x, g_ref[...]).astype(BF16)
    d_ff = wg_ref.shape[1]
    acc = jnp.zeros_like(x)
    for c in range(d_ff // f_chunk):
        sl = slice(c * f_chunk, (c + 1) * f_chunk)
        gate = jnp.dot(h, wg_ref[:, sl], preferred_element_type=F32)
        up = jnp.dot(h, wu_ref[:, sl], preferred_element_type=F32)
        act = (gate * _sigmoid(gate) * up).astype(BF16)
        acc = acc + jnp.dot(act, wd_ref[sl, :], preferred_element_type=F32)
    o_ref[...] = x + 0.5 * acc


def _ffn(x, g, w_up, w_down):
    n, d = x.shape
    d_ff = w_down.shape[0]
    tm = _row_tile(n, 512)
    f_chunk = d_ff
    wg = w_up[:, :d_ff].astype(BF16)
    wu = w_up[:, d_ff:].astype(BF16)
    wd = w_down.astype(BF16)
    row = pl.BlockSpec((tm, d), lambda i: (i, 0))
    return pl.pallas_call(
        functools.partial(_ffn_body, f_chunk=f_chunk),
        grid=(n // tm,),
        in_specs=[row, _const_spec((1, d)), _const_spec((d, d_ff)), _const_spec((d, d_ff)),
                  _const_spec((d_ff, d))],
        out_specs=row,
        out_shape=jax.ShapeDtypeStruct((n, d), F32),
        compiler_params=pltpu.CompilerParams(dimension_semantics=("parallel",),
                                             vmem_limit_bytes=VMEM_LIMIT),
        name="ffn",
    )(x, g.reshape(1, d), wg, wu, wd)


def _in_proj_body(x_ref, g_ref, w_ref, gq_ref, gks_ref, gkw_ref, avg_ref, rw_ref, ml_ref, gates_ref, q_ref,
                  *kv_refs, channel_major):
    h = _rms_rows(x_ref[...], g_ref[...]).astype(BF16)
    p = jnp.dot(h, w_ref[...], preferred_element_type=F32)
    rw_ref[...] = p[:, P_RW:P_ML]
    ml_ref[...] = p[:, P_ML:P_Q]
    gates_ref[...] = p[:, P_GATES:P_TOTAL]
    avg = avg_ref[...]

    def head_norm(z, g):
        w = z.shape[1]
        return z * lax.rsqrt(_seg_mean_sq(z, avg[:w, :w]) + NORM_EPS) * g

    q_ref[...] = head_norm(p[:, P_Q:P_KVC], gq_ref[...])
    kvc = p[:, P_KVC:P_KVS]
    ks = head_norm(p[:, P_KVS:P_KVS + NSA_KV], gks_ref[...])
    vs = p[:, P_KVS + NSA_KV:P_KVW]
    kw = head_norm(p[:, P_KVW:P_KVW + NSA_KV], gkw_ref[...])
    vw = p[:, P_KVW + NSA_KV:P_GATES]
    if channel_major:
        ks_ref, kw_ref, kvc_t_ref, kvs_t_ref, kvw_t_ref = kv_refs
        ks_ref[...] = ks.astype(BF16)
        kw_ref[...] = kw.astype(BF16)
        kvc_t_ref[...] = kvc.T
        kvs_t_ref[:NSA_KV, :] = ks.T
        kvs_t_ref[NSA_KV:, :] = vs.T
        kvw_t_ref[:NSA_KV, :] = kw.T
        kvw_t_ref[NSA_KV:, :] = vw.T
    else:
        kvc_ref, kvs_ref, kvw_ref = kv_refs
        kvc_ref[...] = kvc
        kvs_ref[:, :NSA_KV] = ks
        kvs_ref[:, NSA_KV:] = vs
        kvw_ref[:, :NSA_KV] = kw
        kvw_ref[:, NSA_KV:] = vw


def _permute_w_in(w_in):
    d = w_in.shape[0]
    o_ml = RW_COLS
    o_nsa = RW_COLS + ML_COLS
    ml_gates = w_in[:, o_ml + 3 * ML_WIDTH:o_nsa]
    nsa_gates = w_in[:, o_nsa + NSA_WIDTH + 6 * NSA_KV:]
    pad = jnp.zeros((d, LANES - 2 * ML_HEADS - 3 * NSA_HEADS), w_in.dtype)
    return jnp.concatenate([w_in[:, :o_ml + 3 * ML_WIDTH], w_in[:, o_nsa:o_nsa + NSA_WIDTH + 6 * NSA_KV],
                            ml_gates, nsa_gates, pad], axis=1).astype(BF16)


def _in_proj(x, g, w_perm, qk_g, channel_major):
    B, T, d = x.shape
    n = B * T
    tm = _row_tile(T if channel_major else n, 512)
    tpb = T // tm
    row = lambda w: pl.BlockSpec((tm, w), lambda i: (i, 0))
    widths = [RW_COLS, 3 * ML_WIDTH, LANES, NSA_WIDTH]
    out_specs = [row(w) for w in widths]
    out_shape = [jax.ShapeDtypeStruct((n, w), F32) for w in widths]
    w_kv = 2 * NSA_KV
    if channel_major:
        out_specs += [row(NSA_KV)] * 2 + [pl.BlockSpec((None, w_kv, tm), lambda i: (i // tpb, 0, i % tpb))] * 3
        out_shape += [jax.ShapeDtypeStruct((n, NSA_KV), BF16)] * 2 + [jax.ShapeDtypeStruct((B, w_kv, T), F32)] * 3
    else:
        out_specs += [row(w_kv)] * 3
        out_shape += [jax.ShapeDtypeStruct((n, w_kv), F32)] * 3
    gq = jnp.tile(qk_g[0], NSA_HEADS).reshape(1, NSA_WIDTH)
    gks = jnp.tile(qk_g[2], NSA_KV_HEADS).reshape(1, NSA_KV)
    gkw = jnp.tile(qk_g[3], NSA_KV_HEADS).reshape(1, NSA_KV)
    avg = _seg_avg_matrix(NSA_WIDTH, 1.0 / HEAD_DIM)
    outs = pl.pallas_call(
        functools.partial(_in_proj_body, channel_major=channel_major),
        grid=(n // tm,),
        in_specs=[row(d), _const_spec((1, d)), _const_spec((d, P_TOTAL)), _const_spec((1, NSA_WIDTH)),
                  _const_spec((1, NSA_KV)), _const_spec((1, NSA_KV)), _const_spec((NSA_WIDTH, NSA_WIDTH))],
        out_specs=out_specs,
        out_shape=out_shape,
        compiler_params=pltpu.CompilerParams(dimension_semantics=("parallel",),
                                             vmem_limit_bytes=VMEM_LIMIT),
        name="in_proj",
    )(x.reshape(n, d), g.reshape(1, d), w_perm, gq, gks, gkw, avg)
    return [o.reshape(B, T, -1) if o.shape[0] == n else o for o in outs]


def _rwkv_chunk_inputs(p, prev_row, mu_ref, w0_ref, lw_ref, a0_ref, g2_ref, kks_ref, ka_ref, rk_ref, tri_ref,
                       seg_ref):
    C = p.shape[0]
    rows = lax.broadcasted_iota(jnp.int32, (C, 1), 0)
    prev = jnp.where(rows == 0, prev_row, pltpu.roll(p, 1, axis=0))
    xm = p + (prev - p) * mu_ref[...]
    r = xm[:, 0:RW_WIDTH]
    k = xm[:, RW_WIDTH:2 * RW_WIDTH]
    v = xm[:, 2 * RW_WIDTH:3 * RW_WIDTH]
    lin = xm[:, 3 * RW_WIDTH:3 * RW_WIDTH + RW_W_LORA + RW_A_LORA]
    lane = lax.broadcasted_iota(jnp.int32, lin.shape, 1)
    lora = _dot_hi(jnp.where(lane < RW_W_LORA, jnp.tanh(lin), lin), lw_ref[...])
    w = -_softplus(-(w0_ref[...] + lora[:, :RW_WIDTH])) - 0.5
    a = _sigmoid(a0_ref[...] + lora[:, RW_WIDTH:])
    g = _dot(_sigmoid(xm[:, 3 * RW_WIDTH + RW_W_LORA + RW_A_LORA:]), g2_ref[...])
    kk = k * kks_ref[...]
    hi, lo = _split2(kk * kk)
    seg = seg_ref[...]
    ss = jnp.dot(hi, seg, preferred_element_type=F32) + jnp.dot(lo, seg, preferred_element_type=F32)
    kk = kk / jnp.maximum(jnp.sqrt(ss), 1e-12)
    k = k * (1.0 + (a - 1.0) * ka_ref[...])
    logdec = -jnp.exp(w)
    tri = tri_ref[...]
    G = _dot_sel(tri, logdec)
    g_end = G[C - 1:C, :]
    e_g = jnp.exp(G)
    e_gi = jnp.exp(-G)
    kkd = kk * jnp.exp(G - logdec)
    rd = r * e_g
    b = kk * a
    bi = b * e_gi
    ki = k * e_gi
    e_end = jnp.exp(g_end - G)
    bi2 = b * e_end
    ki2 = k * e_end
    dec_end = jnp.exp(g_end)
    rkk = r * k * rk_ref[...]
    return dict(v=v, g=g, kkd=kkd, rd=rd, bi=bi, ki=ki, bi2=bi2, ki2=ki2, dec_end=dec_end, rkk=rkk)


def _rwkv_body(p_ref, shift_ref, s0_ref, mu_ref, w0_ref, lw_ref, a0_ref, g2_ref, kks_ref, ka_ref,
               rk_ref, gng_ref, gnb_ref, tri_ref, seg_ref, o_ref, st_ref, carry_ref, s_ref, *, chunk, nb):
    C = chunk
    N = HEAD_DIM

    @pl.when(pl.program_id(1) == 0)
    def _():
        carry_ref[...] = shift_ref[...]
        s_ref[...] = s0_ref[...]

    pre = []
    for bi in range(nb):
        p = p_ref[bi]
        pre.append(_rwkv_chunk_inputs(p, carry_ref[bi], mu_ref, w0_ref, lw_ref, a0_ref, g2_ref, kks_ref, ka_ref,
                                      rk_ref, tri_ref, seg_ref))
        carry_ref[bi] = p[C - 1:C, :]

    ti = lax.broadcasted_iota(jnp.int32, (C, C), 0)
    si = lax.broadcasted_iota(jnp.int32, (C, C), 1)
    strict = si < ti
    eye = (si == ti).astype(F32)
    s2 = lax.broadcasted_iota(jnp.int32, (C, 2 * C), 1)
    incl2 = jnp.where(s2 < C, s2, s2 - C) <= lax.broadcasted_iota(jnp.int32, (C, 2 * C), 0)

    units = [(bi, h) for bi in range(nb) for h in range(RW_HEADS)]
    col = lambda name, u: pre[u[0]][name][:, u[1] * N:(u[1] + 1) * N]
    lhs = [jnp.concatenate([col('kkd', u), col('rd', u)], axis=0) for u in units]
    rhs = [jnp.concatenate([col('bi', u), col('ki', u)], axis=0) for u in units]
    vh = [col('v', u) for u in units]
    s0 = [s_ref[bi, h] for bi, h in units]
    m4 = [_dot_nt(a, b) for a, b in zip(lhs, rhs)]
    ks = [_dot_nt(a, s) for a, s in zip(lhs, s0)]
    pw = [jnp.where(strict, -m[:C, :C], 0.0) for m in m4]
    lk = [jnp.where(strict, m[:C, C:], 0.0) for m in m4]
    mbk = [jnp.where(incl2, m[C:, :], 0.0) for m in m4]
    lkv = [_dot(a, b) for a, b in zip(lk, vh)]
    t_inv = [eye + n for n in pw]
    span = 2
    while span < C:
        pw = [_dot(x, x) for x in pw]
        t_inv = [t + _dot(t, x) for t, x in zip(t_inv, pw)]
        span *= 2
    u_ = [_dot(t, -(k_[:C] + l)) for t, k_, l in zip(t_inv, ks, lkv)]
    uv = [jnp.concatenate([a, b], axis=0) for a, b in zip(u_, vh)]
    y = [k_[C:] + _dot(m, x) for k_, m, x in zip(ks, mbk, uv)]
    s_new = [s * col('dec_end', u) + _dot_tn(x, jnp.concatenate([col('bi2', u), col('ki2', u)], axis=0))
             for s, x, u in zip(s0, uv, units)]
    for (bi, h), s, yy, v_ in zip(units, s_new, y, vh):
        sl = slice(h * N, (h + 1) * N)
        s_ref[bi, h] = s
        mean = jnp.mean(yy, axis=-1, keepdims=True)
        var = jnp.mean(jnp.square(yy - mean), axis=-1, keepdims=True)
        yn = (yy - mean) * lax.rsqrt(var + RW_GN_EPS) * gng_ref[:, sl] + gnb_ref[:, sl]
        bonus = jnp.sum(pre[bi]['rkk'][:, sl], axis=-1, keepdims=True) * v_
        o_ref[bi, :, sl] = (yn + bonus) * pre[bi]['g'][:, sl]

    @pl.when(pl.program_id(1) == pl.num_programs(1) - 1)
    def _():
        st_ref[...] = s_ref[...]


def _rwkv(p_rw, shift_prev, s0, W):
    B, T, _ = p_rw.shape
    C = _scan_chunk(T, RW_CHUNKS)
    nb = _row_tile(B, RW_CHAIN_BATCH)
    row = lambda v: v.reshape(1, -1).astype(F32)
    z = jnp.zeros((RW_W_LORA, RW_WIDTH), F32)
    lw = jnp.concatenate([jnp.concatenate([W['rw_w2'], z], axis=1),
                          jnp.concatenate([z, W['rw_a2']], axis=1)], axis=0)
    tri = (jnp.arange(C)[:, None] >= jnp.arange(C)[None, :]).astype(BF16)
    seg = _seg_avg_matrix(RW_WIDTH, 1.0)
    consts = [row(W['rw_mu']), row(W['rw_w0']), lw, row(W['rw_a0']), W['rw_g2'].astype(BF16), row(W['rw_kk']),
              row(W['rw_ka']), row(W['rw_rk']), row(W['rw_gn_g']), row(W['rw_gn_b']), tri, seg]
    out, s_t = pl.pallas_call(
        functools.partial(_rwkv_body, chunk=C, nb=nb),
        grid=(B // nb, T // C),
        in_specs=[pl.BlockSpec((nb, C, RW_COLS), lambda b, c: (b, c, 0)),
                  pl.BlockSpec((nb, 1, RW_COLS), lambda b, c: (b, 0, 0)),
                  pl.BlockSpec((nb, RW_HEADS, HEAD_DIM, HEAD_DIM), lambda b, c: (b, 0, 0, 0))]
                 + [_const_spec(c.shape) for c in consts],
        out_specs=[pl.BlockSpec((nb, C, RW_WIDTH), lambda b, c: (b, c, 0)),
                   pl.BlockSpec((nb, RW_HEADS, HEAD_DIM, HEAD_DIM), lambda b, c: (b, 0, 0, 0))],
        out_shape=[jax.ShapeDtypeStruct((B, T, RW_WIDTH), F32),
                   jax.ShapeDtypeStruct((B, RW_HEADS, HEAD_DIM, HEAD_DIM), F32)],
        scratch_shapes=[pltpu.VMEM((nb, 1, RW_COLS), F32), pltpu.VMEM((nb, RW_HEADS, HEAD_DIM, HEAD_DIM), F32)],
        compiler_params=pltpu.CompilerParams(dimension_semantics=("parallel", "arbitrary"),
                                             vmem_limit_bytes=VMEM_LIMIT),
        name="rwkv7",
    )(p_rw, shift_prev, s0, *consts)
    return out, s_t


def _mlstm_body(p_ref, gc_ref, gt_ref, convp_ref, c0_ref, n0_ref, m0_ref, cw_ref, cb_ref, wq_ref, wk_ref,
                bias_r_ref, bias_c_ref, ng_ref, skip_ref, tril_ref, triu_ref,
                o_ref, ct_ref, nt_ref, mt_ref, ext_ref, c_ref, n_ref, m_ref, *, chunk, nb):
    L = chunk
    N = HEAD_DIM
    H = ML_HEADS

    @pl.when(pl.program_id(1) == 0)
    def _():
        ext_ref[:, 0:8, :] = convp_ref[...]
        c_ref[...] = c0_ref[...]
        n_ref[...] = n0_ref[...]
        m_ref[...] = m0_ref[...]

    cw = cw_ref[...]
    pre = []
    for bi in range(nb):
        p = p_ref[bi]
        x = p[:, 0:ML_WIDTH]
        ext_ref[bi, 8:8 + L, :] = x
        conv = (cb_ref[...] + cw[3:4, :] * x + cw[2:3, :] * ext_ref[bi, 7:7 + L, :]
                + cw[1:2, :] * ext_ref[bi, 6:6 + L, :] + cw[0:1, :] * ext_ref[bi, 5:5 + L, :])
        tail = ext_ref[bi, L:L + 8, :]
        ext_ref[bi, 0:8, :] = tail
        ca = conv * _sigmoid(conv)
        gcb = gc_ref[bi] + bias_r_ref[...]
        gtb = gt_ref[bi, 0] + bias_c_ref[...][:, 0:1]
        pre.append(dict(
            v=p[:, ML_WIDTH:2 * ML_WIDTH], o_pre=p[:, 2 * ML_WIDTH:3 * ML_WIDTH], ca=ca,
            q=_dot(ca, wq_ref[...]), k=_dot(ca, wk_ref[...]) * (N ** -0.5), gcb=gcb, gtb=gtb,
            bcum_c=_dot_sel(tril_ref[...], -_softplus(-gcb)),
            bcum_r=_dot_x_sel(-_softplus(-gtb), triu_ref[...])))

    ti = lax.broadcasted_iota(jnp.int32, (L, L), 0)
    si = lax.broadcasted_iota(jnp.int32, (L, L), 1)
    causal = si <= ti

    units = [(bi, h) for bi in range(nb) for h in range(H)]
    col = lambda name, u: pre[u[0]][name][:, u[1] * N:(u[1] + 1) * N]
    qh = [col('q', u) for u in units]
    kh = [col('k', u) for u in units]
    vh = [col('v', u) for u in units]
    cs = [c_ref[bi, h] for bi, h in units]
    ns = [n_ref[bi, h:h + 1, :] for bi, h in units]
    m_prev = [m_ref[bi, h:h + 1, 0:1] for bi, h in units]
    b_c = [pre[bi]['bcum_c'][:, H + h:H + h + 1] for bi, h in units]
    i_c = [pre[bi]['gcb'][:, h:h + 1] for bi, h in units]
    b_r = [pre[bi]['bcum_r'][H + h:H + h + 1, :] for bi, h in units]
    i_r = [pre[bi]['gtb'][h:h + 1, :] for bi, h in units]
    qk = [_dot_nt(a, b) for a, b in zip(qh, kh)]
    qc = [_dot_nt(a, c) for a, c in zip(qh, cs)]
    dmat = [jnp.where(causal, bc - br + ir, -jnp.inf) for bc, br, ir in zip(b_c, b_r, i_r)]
    inter = [bc + mp for bc, mp in zip(b_c, m_prev)]
    m_t = [jnp.maximum(it, jnp.max(dm, axis=-1, keepdims=True)) for it, dm in zip(inter, dmat)]
    a = [x * jnp.exp(dm - mt) for x, dm, mt in zip(qk, dmat, m_t)]
    av = [_dot(x, v_) for x, v_ in zip(a, vh)]
    b_end = [bc[L - 1:L, :] for bc in b_c]
    m_new = [jnp.maximum(be + mp, jnp.max(be - br + ir, axis=-1, keepdims=True))
             for be, mp, br, ir in zip(b_end, m_prev, b_r, i_r)]
    dec = [jnp.exp(be + mp - mn) for be, mp, mn in zip(b_end, m_prev, m_new)]
    wg = [jnp.exp(be - bc + ic - mn) for be, bc, ic, mn in zip(b_end, b_c, i_c, m_new)]
    c_new = [d * c + _dot_tn(w * v_, k_) for d, c, w, v_, k_ in zip(dec, cs, wg, vh, kh)]
    for idx, (bi, h) in enumerate(units):
        sl = slice(h * N, (h + 1) * N)
        sc = jnp.exp(inter[idx] - m_t[idx])
        num = sc * qc[idx] + av[idx]
        den = (sc * jnp.sum(qh[idx] * ns[idx], axis=-1, keepdims=True)
               + jnp.sum(a[idx], axis=-1, keepdims=True))
        hh = num / jnp.maximum(jnp.abs(den), jnp.exp(-m_t[idx]))
        c_ref[bi, h] = c_new[idx]
        n_ref[bi, h:h + 1, :] = dec[idx] * ns[idx] + jnp.sum(wg[idx] * kh[idx], axis=0, keepdims=True)
        m_ref[bi, h:h + 1, :] = jnp.broadcast_to(m_new[idx], (1, LANES))
        hn = hh * lax.rsqrt(jnp.mean(hh * hh, axis=-1, keepdims=True) + NORM_EPS)
        o_ref[bi, :, sl] = ((hn * ng_ref[:, sl] + skip_ref[:, sl] * pre[bi]['ca'][:, sl])
                            * _sigmoid(pre[bi]['o_pre'][:, sl]))

    @pl.when(pl.program_id(1) == pl.num_programs(1) - 1)
    def _():
        ct_ref[...] = c_ref[...]
        nt_ref[...] = n_ref[...]
        mt_ref[...] = m_ref[...]


def _block_diag(w):
    H, N, _ = w.shape
    eye = jnp.eye(H, dtype=w.dtype)
    return (eye[:, None, :, None] * w[:, :, None, :]).reshape(H * N, H * N)


def _mlstm(p_ml, gates, conv_prev, c0, n0, m0, W):
    B, T, _ = p_ml.shape
    H, N = ML_HEADS, HEAD_DIM
    L = _scan_chunk(T, ML_CHUNKS)
    nc = T // L
    gt = jnp.swapaxes(gates[:, :, :2 * H].reshape(B, nc, L, 2 * H), 2, 3)
    convp = jnp.concatenate([jnp.zeros((B, 8 - (ML_CONV - 1), ML_WIDTH), F32), conv_prev], axis=1)
    m0b = jnp.broadcast_to(jnp.pad(m0, ((0, 0), (0, 8 - H)))[:, :, None], (B, 8, LANES))
    bias = W['ml_gate_b'].reshape(2 * H)
    bias_r = jnp.pad(bias, (0, LANES - 2 * H)).reshape(1, LANES)
    bias_c = jnp.broadcast_to(bias[:, None], (2 * H, LANES))
    tril = (jnp.arange(L)[:, None] >= jnp.arange(L)[None, :]).astype(BF16)
    row = lambda z: z.reshape(1, -1).astype(F32)
    consts = [W['ml_conv_w'], row(W['ml_conv_b']), _block_diag(W['ml_wq']).astype(BF16),
              _block_diag(W['ml_wk']).astype(BF16), bias_r, bias_c, row(W['ml_norm_g']), row(W['ml_skip']),
              tril, tril.T]
    nb = _row_tile(B, ML_CHAIN_BATCH)
    per_b = lambda *shape: pl.BlockSpec((nb,) + shape, lambda b, c: (b,) + (0,) * len(shape))
    out, c_t, n_t, m_t = pl.pallas_call(
        functools.partial(_mlstm_body, chunk=L, nb=nb),
        grid=(B // nb, nc),
        in_specs=[pl.BlockSpec((nb, L, 3 * ML_WIDTH), lambda b, c: (b, c, 0)),
                  pl.BlockSpec((nb, L, LANES), lambda b, c: (b, c, 0)),
                  pl.BlockSpec((nb, 1, 2 * H, L), lambda b, c: (b, c, 0, 0)),
                  per_b(8, ML_WIDTH), per_b(H, N, N), per_b(H, N), per_b(8, LANES)]
                 + [_const_spec(c.shape) for c in consts],
        out_specs=[pl.BlockSpec((nb, L, ML_WIDTH), lambda b, c: (b, c, 0)),
                   per_b(H, N, N), per_b(H, N), per_b(8, LANES)],
        out_shape=[jax.ShapeDtypeStruct((B, T, ML_WIDTH), F32), jax.ShapeDtypeStruct((B, H, N, N), F32),
                   jax.ShapeDtypeStruct((B, H, N), F32), jax.ShapeDtypeStruct((B, 8, LANES), F32)],
        scratch_shapes=[pltpu.VMEM((nb, L + 8, ML_WIDTH), F32), pltpu.VMEM((nb, H, N, N), F32),
                        pltpu.VMEM((nb, H, N), F32), pltpu.VMEM((nb, 8, LANES), F32)],
        compiler_params=pltpu.CompilerParams(dimension_semantics=("parallel", "arbitrary"),
                                             vmem_limit_bytes=VMEM_LIMIT),
        name="mlstm",
    )(p_ml, gates, gt, convp, c0, n0, m0b, *consts)
    return out, c_t, n_t, m_t[:, :H, 0]


def _compress_body(*refs, n_parts, paged):
    if paged:
        refs = refs[1:]
    part_refs = refs[:n_parts]
    wpos_ref, sel_ref, wc_ref, gk_ref, avg_ref, o_ref = refs[n_parts:]
    wpos = wpos_ref[...]
    parts = []
    for r in part_refs:
        x = r[...]
        parts += [x[:, c * LANES:(c + 1) * LANES] * wpos for c in range(x.shape[1] // LANES)]
    c = _dot(jnp.concatenate(parts, axis=1), sel_ref[...])
    c = _dot_hi(wc_ref[...], c)
    ck = c[:NSA_KV]
    o_ref[:NSA_KV, :] = ck * lax.rsqrt(_seg_mean_sq_rows(ck, avg_ref[...]) + NORM_EPS) * gk_ref[...]
    o_ref[NSA_KV:, :] = c[NSA_KV:]


def _compress_consts(W, n_rows):
    w = 2 * NSA_KV
    wpos = W['nsa_wpos'].reshape(NSA_BLOCK, w).T
    wc_t = _block_diag(W['nsa_wc'].reshape(2 * NSA_KV_HEADS, HEAD_DIM, HEAD_DIM)).T
    gk = jnp.tile(W['nsa_qk_g'][1], NSA_KV_HEADS).reshape(NSA_KV, 1)
    sel = (jnp.arange(n_rows)[:, None] // NSA_BLOCK == jnp.arange(n_rows // NSA_BLOCK)[None, :]).astype(BF16)
    return [jnp.concatenate([wpos, wpos], axis=1), sel, wc_t, gk, _seg_avg_matrix(NSA_KV, 1.0 / HEAD_DIM)]


def _nsa_compress_dense(kvc_t, W):
    B, w, T = kvc_t.shape
    consts = _compress_consts(W, T)
    return pl.pallas_call(
        functools.partial(_compress_body, n_parts=1, paged=False),
        grid=(B,),
        in_specs=[pl.BlockSpec((None, w, T), lambda b: (b, 0, 0))] + [_const_spec(c.shape) for c in consts],
        out_specs=pl.BlockSpec((None, w, T // NSA_BLOCK), lambda b: (b, 0, 0)),
        out_shape=jax.ShapeDtypeStruct((B, w, T // NSA_BLOCK), F32),
        compiler_params=pltpu.CompilerParams(dimension_semantics=("parallel",), vmem_limit_bytes=VMEM_LIMIT),
        name="nsa_compress_dense",
    )(kvc_t, *consts)


def _nsa_compress_paged(pool_t, page_table, W):
    B, n_pages = page_table.shape
    w = pool_t.shape[1]
    pages = min(64, n_pages)
    bpp = PAGE_SIZE // NSA_BLOCK
    consts = _compress_consts(W, pages * PAGE_SIZE)

    def page_spec(k):
        return pl.BlockSpec((None, w, PAGE_SIZE), lambda b, i, pt: (pt[b * n_pages + i * pages + k], 0, 0))

    const = lambda c: pl.BlockSpec(c.shape, lambda b, i, pt: (0,) * c.ndim)
    return pl.pallas_call(
        functools.partial(_compress_body, n_parts=pages, paged=True),
        grid_spec=pltpu.PrefetchScalarGridSpec(
            num_scalar_prefetch=1,
            grid=(B, n_pages // pages),
            in_specs=[page_spec(k) for k in range(pages)] + [const(c) for c in consts],
            out_specs=pl.BlockSpec((None, w, pages * bpp), lambda b, i, pt: (b, 0, i)),
        ),
        out_shape=jax.ShapeDtypeStruct((B, w, n_pages * bpp), F32),
        compiler_params=pltpu.CompilerParams(dimension_semantics=("parallel", "parallel"),
                                             vmem_limit_bytes=VMEM_LIMIT),
        name="nsa_compress_paged",
    )(page_table.reshape(-1), *([pool_t] * pages), *consts)


def _split_cmp(cmp_t):
    return jnp.swapaxes(cmp_t[:, :NSA_KV], 1, 2), cmp_t[:, NSA_KV:]


def _gate_bias_row(gate_b):
    return jnp.pad(gate_b, (2 * ML_HEADS, LANES - 2 * ML_HEADS - 3 * NSA_HEADS)).reshape(1, LANES)


def _gate_lane(g, r):
    return 2 * ML_HEADS + (g * NSA_GROUP + r) * 3


def _attend_t(k_tile, v_t, q_all, bias, carry):
    tk = k_tile.shape[0]
    ones = jnp.ones((8, tk), BF16)
    s = [jnp.dot(k_tile, q, preferred_element_type=F32) + jnp.concatenate([b] * NSA_GROUP, axis=1)
         for q, b in zip(q_all, bias)]
    m_new = [jnp.maximum(m, jnp.max(x, axis=0, keepdims=True)) for x, (m, _) in zip(s, carry)]
    p = [jnp.exp2(x - m).astype(BF16) for x, m in zip(s, m_new)]
    pv = [jnp.dot(jnp.concatenate([v.astype(BF16), ones], axis=0), x, preferred_element_type=F32)
          for v, x in zip(v_t, p)]
    return tuple((mn, jnp.exp2(m - mn) * acc + x) for mn, (m, acc), x in zip(m_new, carry, pv))


def _attend_t_init(groups, width):
    return tuple((jnp.full((1, width), NEG, F32), jnp.zeros((HEAD_DIM + 8, width), F32)) for _ in range(groups))


def _attend_t_finish(carry):
    return [acc[:HEAD_DIM] / jnp.maximum(acc[HEAD_DIM:HEAD_DIM + 1], 1e-30) for _, acc in carry]


def _nsa_prompt_body(q_ref, gates_ref, gb_ref, ck_ref, cv_ref, ks_ref, vs_ref, kw_ref, vw_ref, o_ref,
                     sel_ref, ot_ref, *, tq, tk, n_blocks):
    R, N, G = NSA_GROUP, HEAD_DIM, NSA_KV_HEADS
    W = R * tq
    i = pl.program_id(1)
    t0 = i * tq
    qpos = t0 + lax.broadcasted_iota(jnp.int32, (1, tq), 1)
    q_t = q_ref[...].T * (N ** -0.5)
    gt_t = _sigmoid((gates_ref[...] + gb_ref[...]).T)
    ck = ck_ref[...]
    nc = ck.shape[0]
    zeros = jnp.zeros((N, W), F32)
    q_f = []
    for g in range(G):
        hs = jnp.concatenate([q_t[(g * R + r) * N:(g * R + r + 1) * N] for r in range(R)], axis=1)
        q_f.append(jnp.concatenate([hs, zeros] if g == 0 else [zeros, hs], axis=0))
    q_b = [(x * LOG2_E).astype(BF16) for x in q_f]

    blk_w = lax.broadcasted_iota(jnp.int32, (nc, W), 0)
    vis = (blk_w + 1) * NSA_BLOCK - 1 <= jnp.concatenate([qpos] * R, axis=1)
    o_c, imp = [], []
    for g in range(G):
        s = jnp.where(vis, _dot_hi(ck, q_f[g]), -jnp.inf)
        m = jnp.max(s, axis=0, keepdims=True)
        e = jnp.exp(s - jnp.where(m == -jnp.inf, 0.0, m))
        p = e / jnp.maximum(jnp.sum(e, axis=0, keepdims=True), 1e-30)
        o_c.append(_dot(cv_ref[g * N:(g + 1) * N, :], p))
        pg = p[:, 0:tq]
        for r in range(1, R):
            pg = pg + p[:, r * tq:(r + 1) * tq]
        imp.append(pg)
    imp = jnp.concatenate(imp, axis=1)

    blk_r = lax.broadcasted_iota(jnp.int32, (nc, G * tq), 0)
    cur = jnp.right_shift(jnp.concatenate([qpos] * G, axis=1), BLOCK_SHIFT)
    forced = (blk_r == 0) | (blk_r == cur) | (blk_r == cur - 1)
    score = jnp.where(forced, BIG, jnp.where(blk_r <= cur, imp, -BIG))
    rowf = blk_r.astype(F32)
    sel_bias = jnp.full((nc, G * tq), NEG, F32)
    for _ in range(min(NSA_TOPN, n_blocks)):
        m = jnp.max(score, axis=0, keepdims=True)
        first = jnp.min(jnp.where(score == m, rowf, float(nc)), axis=0, keepdims=True)
        hit = rowf == first
        sel_bias = jnp.where(hit, 0.0, sel_bias)
        score = jnp.where(hit, -jnp.inf, score)
    sel_ref[...] = sel_bias

    kpos_col = lax.broadcasted_iota(jnp.int32, (tk, 1), 0)
    bpt = tk // NSA_BLOCK

    def sel_tile(kt, carry, diagonal):
        k0 = pl.multiple_of(kt * tk, tk)
        chosen = jnp.concatenate(
            [jnp.broadcast_to(sel_ref[pl.ds(kt * bpt + m, 1), :], (NSA_BLOCK, G * tq)) for m in range(bpt)], axis=0)
        bias = [chosen[:, g * tq:(g + 1) * tq] for g in range(G)]
        if diagonal:
            causal = jnp.where(k0 + kpos_col <= qpos, 0.0, NEG)
            bias = [b + causal for b in bias]
        v_t = [vs_ref[g * N:(g + 1) * N, pl.ds(k0, tk)] for g in range(G)]
        return _attend_t(ks_ref[pl.ds(k0, tk), :], v_t, q_b, bias, carry)

    last = t0 // tk
    carry = lax.fori_loop(0, last, functools.partial(sel_tile, diagonal=False), _attend_t_init(G, W))
    o_s = _attend_t_finish(sel_tile(last, carry, diagonal=True))

    tw = min(NSA_WINDOW + tq, kw_ref.shape[0])
    w0 = pl.multiple_of(jnp.maximum(t0 + tq - tw, 0), tq)
    d = qpos - (w0 + lax.broadcasted_iota(jnp.int32, (tw, 1), 0))
    bias = jnp.where((d >= 0) & (d < NSA_WINDOW), 0.0, NEG)
    v_t = [vw_ref[g * N:(g + 1) * N, pl.ds(w0, tw)] for g in range(G)]
    o_w = _attend_t_finish(_attend_t(kw_ref[pl.ds(w0, tw), :], v_t, q_b, [bias] * G, _attend_t_init(G, W)))

    for g in range(G):
        for r in range(R):
            c0 = _gate_lane(g, r)
            h = g * R + r
            cols = slice(r * tq, (r + 1) * tq)
            ot_ref[h * N:(h + 1) * N, :] = (gt_t[c0:c0 + 1] * o_c[g][:, cols] + gt_t[c0 + 1:c0 + 2] * o_s[g][:, cols]
                                            + gt_t[c0 + 2:c0 + 3] * o_w[g][:, cols])
    o_ref[...] = ot_ref[...].T


def _nsa_prompt(q, gates, ck, cv_t, ks, kvs_t, kw, kvw_t, W):
    B, T, _ = q.shape
    tq = min(256, T)
    tk = min(512, T)
    assert tk % tq == 0 and T % tk == 0
    nc = ck.shape[1]
    per_b = lambda a: pl.BlockSpec((None,) + a.shape[1:], lambda b, i: (b,) + (0,) * (a.ndim - 1))
    tile = lambda w: pl.BlockSpec((None, tq, w), lambda b, i: (b, i, 0))
    v_rows = pl.BlockSpec((None, NSA_KV, T), lambda b, i: (b, 1, 0))
    gb = _gate_bias_row(W['nsa_gate_b'])
    return pl.pallas_call(
        functools.partial(_nsa_prompt_body, tq=tq, tk=tk, n_blocks=T // NSA_BLOCK),
        grid=(B, T // tq),
        in_specs=[tile(NSA_WIDTH), tile(LANES), pl.BlockSpec((1, LANES), lambda b, i: (0, 0)),
                  per_b(ck), per_b(cv_t), per_b(ks), v_rows, per_b(kw), v_rows],
        out_specs=tile(NSA_WIDTH),
        out_shape=jax.ShapeDtypeStruct((B, T, NSA_WIDTH), F32),
        scratch_shapes=[pltpu.VMEM((nc, NSA_KV_HEADS * tq), F32), pltpu.VMEM((NSA_WIDTH, tq), F32)],
        compiler_params=pltpu.CompilerParams(dimension_semantics=("parallel", "arbitrary"),
                                             vmem_limit_bytes=VMEM_LIMIT),
        name="nsa_prompt",
    )(q, gates, gb, ck, cv_t, ks, kvs_t, kw, kvw_t)


def _stack_heads(qg):
    return jnp.concatenate([qg[:, r * HEAD_DIM:(r + 1) * HEAD_DIM] for r in range(NSA_GROUP)],
                           axis=0) * (HEAD_DIM ** -0.5)


def _tile_rows(x, n):
    return jnp.concatenate([x] * n, axis=0)


def _masked_softmax(s, mask):
    s = jnp.where(mask, s, -jnp.inf)
    m = jnp.max(s, axis=-1, keepdims=True)
    m = jnp.where(m == -jnp.inf, 0.0, m)
    e = jnp.exp(s - m)
    return e / jnp.maximum(jnp.sum(e, axis=-1, keepdims=True), 1e-30)


def _select_blocks(imp, qpos, n_blocks, k):
    t, w = imp.shape
    jf = lax.broadcasted_iota(jnp.int32, (t, w), 1)
    cur = jnp.right_shift(qpos, BLOCK_SHIFT)
    forced = (jf == 0) | (jf == cur) | (jf == cur - 1)
    score = jnp.where(forced, BIG, jnp.where(jf <= cur, imp, -BIG))
    score = jnp.where(jf < n_blocks, score, -jnp.inf)
    lane = jf.astype(F32)
    idx = []
    for _ in range(k):
        m = jnp.max(score, axis=-1, keepdims=True)
        i = jnp.min(jnp.where(score == m, lane, float(w)), axis=-1, keepdims=True)
        score = jnp.where(lane == i, -jnp.inf, score)
        idx.append(i)
    return idx


def _online_step(carry, s, ok, pv):
    m, l, acc = carry
    s = jnp.where(ok, s, -jnp.inf)
    m_new = jnp.maximum(m, jnp.max(s, axis=-1, keepdims=True))
    m_safe = jnp.where(m_new == -jnp.inf, 0.0, m_new)
    p = jnp.exp(s - m_safe)
    alpha = jnp.exp(m - m_safe)
    return m_new, alpha * l + jnp.sum(p, axis=-1, keepdims=True), alpha * acc + pv(p)


def _online_init(rows):
    return (jnp.full((rows, 1), -jnp.inf, F32), jnp.zeros((rows, 1), F32), jnp.zeros((rows, HEAD_DIM), F32))


def _online_finish(carry):
    _, l, acc = carry
    return acc / jnp.maximum(l, 1e-30)


def _nsa_sample_cmp_body(q_ref, ck_ref, cv_ref, kvw_ref, winp_ref, oc_ref, ow_ref, idx_ref, *, pos0, n_blocks):
    R, N = NSA_GROUP, HEAD_DIM
    t = q_ref.shape[0]
    wb = winp_ref.shape[1]
    qpos = pos0 + lax.broadcasted_iota(jnp.int32, (t, 1), 0)
    qpos_s = _tile_rows(qpos, R)
    q = q_ref[...]
    ck = ck_ref[...]
    nc = ck.shape[0]
    w_sel = -(-n_blocks // LANES) * LANES
    new = kvw_ref[...]
    oc_heads, ow_heads = [], []
    for g in range(NSA_KV_HEADS):
        krows = slice(g * N, (g + 1) * N)
        vrows = slice(NSA_KV + g * N, NSA_KV + (g + 1) * N)
        qs = _stack_heads(q[:, g * R * N:(g + 1) * R * N])
        blk = lax.broadcasted_iota(jnp.int32, (1, nc), 1)
        p = _masked_softmax(_dot_nt_hi(qs, ck[:, krows]), (blk + 1) * NSA_BLOCK - 1 <= qpos_s)
        o_c = _dot_nt(p, cv_ref[krows, :])
        imp = p[0:t]
        for r in range(1, R):
            imp = imp + p[r * t:(r + 1) * t]
        if w_sel > nc:
            imp = jnp.concatenate([imp, jnp.zeros((t, w_sel - nc), F32)], axis=1)
        idx = _select_blocks(imp, qpos, n_blocks, min(NSA_TOPN, n_blocks))
        lane = lax.broadcasted_iota(jnp.int32, (t, LANES), 1)
        tile = jnp.zeros((t, LANES), F32)
        for kk, col in enumerate(idx):
            tile = jnp.where(lane == kk, col, tile)
        idx_ref[g] = tile.astype(jnp.int32)

        carry = _online_init(R * t)
        d = qpos_s - (pos0 - wb + lax.broadcasted_iota(jnp.int32, (1, wb), 1))
        carry = _online_step(carry, _dot(qs, winp_ref[krows, :]), (d >= 0) & (d < NSA_WINDOW),
                             lambda p: _dot_nt(p, winp_ref[vrows, :]))
        d = qpos_s - (pos0 + lax.broadcasted_iota(jnp.int32, (1, t), 1))
        carry = _online_step(carry, _dot_nt(qs, new[:, krows]), (d >= 0) & (d < NSA_WINDOW),
                             lambda p: _dot(p, new[:, vrows]))
        o_w = _online_finish(carry)
        oc_heads += [o_c[r * t:(r + 1) * t] for r in range(R)]
        ow_heads += [o_w[r * t:(r + 1) * t] for r in range(R)]
    oc_ref[...] = jnp.concatenate(oc_heads, axis=1)
    ow_ref[...] = jnp.concatenate(ow_heads, axis=1)


def _nsa_sample_cmp(q, ck, cv_t, kv_w, win_prev_t, pos0, n_blocks):
    B, T, _ = q.shape
    per_b = lambda a: pl.BlockSpec((None,) + a.shape[1:], lambda b: (b,) + (0,) * (a.ndim - 1))
    out_b = lambda *s: pl.BlockSpec((None,) + s, lambda b: (b,) + (0,) * len(s))
    return pl.pallas_call(
        functools.partial(_nsa_sample_cmp_body, pos0=pos0, n_blocks=n_blocks),
        grid=(B,),
        in_specs=[per_b(q), per_b(ck), per_b(cv_t), per_b(kv_w), per_b(win_prev_t)],
        out_specs=[out_b(T, NSA_WIDTH), out_b(T, NSA_WIDTH), out_b(NSA_KV_HEADS, T, LANES)],
        out_shape=[jax.ShapeDtypeStruct((B, T, NSA_WIDTH), F32), jax.ShapeDtypeStruct((B, T, NSA_WIDTH), F32),
                   jax.ShapeDtypeStruct((B, NSA_KV_HEADS, T, LANES), jnp.int32)],
        compiler_params=pltpu.CompilerParams(dimension_semantics=("parallel",)),
        name="nsa_sample_cmp",
    )(q, ck, cv_t, kv_w, win_prev_t)


def _nsa_sample_sel_body(idx_ref, pt_ref, *refs, n_past, pos0, topk):
    R, N, G = NSA_GROUP, HEAD_DIM, NSA_KV_HEADS
    page_refs = refs[:G * topk]
    q_ref, new_ref, oc_ref, ow_ref, gates_ref, gb_ref, o_ref = refs[G * topk:]
    b = pl.program_id(0)
    t = pl.program_id(1)
    n_t = pl.num_programs(1)
    qpos = pos0 + t
    gt = _sigmoid(gates_ref[pl.ds(t, 1), :] + gb_ref[...])
    q = q_ref[pl.ds(t, 1), :]
    o_c = oc_ref[pl.ds(t, 1), :]
    o_w = ow_ref[pl.ds(t, 1), :]
    lane = lax.broadcasted_iota(jnp.int32, (1, PAGE_SIZE), 1)
    heads = []
    for g in range(G):
        qs = _stack_heads(q[:, g * R * N:(g + 1) * R * N])
        new_k = new_ref[g * N:(g + 1) * N, :]
        new_v = new_ref[NSA_KV + g * N:NSA_KV + (g + 1) * N, :]
        scores, vals = [], []
        for k in range(topk):
            j = idx_ref[((b * n_t + t) * G + g) * topk + k]
            half = j % (PAGE_SIZE // NSA_BLOCK)
            is_new = j >= n_past
            page = page_refs[g * topk + k]
            k_t = jnp.where(is_new, new_k, page[g * N:(g + 1) * N, :])
            v_t = jnp.where(is_new, new_v, page[NSA_KV + g * N:NSA_KV + (g + 1) * N, :])
            kpos = (j - half) * NSA_BLOCK + lane
            ok = (jnp.right_shift(lane, BLOCK_SHIFT) == half) & (kpos <= qpos)
            scores.append(jnp.where(ok, _dot(qs, k_t), -jnp.inf))
            vals.append(v_t)
        m = scores[0].max(axis=-1, keepdims=True)
        for s in scores[1:]:
            m = jnp.maximum(m, s.max(axis=-1, keepdims=True))
        m = jnp.where(m == -jnp.inf, 0.0, m)
        l = jnp.zeros((R, 1), F32)
        acc = jnp.zeros((R, N), F32)
        for s, v_t in zip(scores, vals):
            e = jnp.exp(s - m)
            l = l + jnp.sum(e, axis=-1, keepdims=True)
            acc = acc + _dot_nt(e, v_t)
        o_s = acc / jnp.maximum(l, 1e-30)
        for r in range(R):
            c0 = (g * R + r) * N
            gl = _gate_lane(g, r)
            heads.append(gt[:, gl:gl + 1] * o_c[:, c0:c0 + N] + gt[:, gl + 1:gl + 2] * o_s[r:r + 1]
                         + gt[:, gl + 2:gl + 3] * o_w[:, c0:c0 + N])
    o_ref[pl.ds(t, 1), :] = jnp.concatenate(heads, axis=1)


def _nsa_sample_sel(q, idx, page_table, pool_t, kv_s_new, o_c, o_w, gates, W, pos0):
    B, T, _ = q.shape
    G, N = NSA_KV_HEADS, HEAD_DIM
    topk = idx.shape[-1]
    n_pages = page_table.shape[1]
    bpp = PAGE_SIZE // NSA_BLOCK
    n_past = pos0 // NSA_BLOCK
    assert n_past % bpp == 0 and T <= NSA_BLOCK
    new_t = jnp.swapaxes(jnp.pad(kv_s_new, ((0, 0), (0, PAGE_SIZE - T), (0, 0))), 1, 2)

    def page_spec(g, k):
        def index(b, t, idx_ref, pt_ref):
            j = jnp.minimum(idx_ref[((b * T + t) * G + g) * topk + k], n_past - 1)
            return (pt_ref[b * n_pages + j // bpp], 0, 0)
        return pl.BlockSpec((None,) + pool_t.shape[1:], index)

    per_b = lambda a: pl.BlockSpec((None,) + a.shape[1:], lambda b, t, i, p: (b,) + (0,) * (a.ndim - 1))
    gb = _gate_bias_row(W['nsa_gate_b'])
    kv_specs = [page_spec(g, k) for g in range(G) for k in range(topk)]
    return pl.pallas_call(
        functools.partial(_nsa_sample_sel_body, n_past=n_past, pos0=pos0, topk=topk),
        grid_spec=pltpu.PrefetchScalarGridSpec(
            num_scalar_prefetch=2,
            grid=(B, T),
            in_specs=kv_specs + [per_b(q), per_b(new_t), per_b(o_c), per_b(o_w), per_b(gates),
                                 pl.BlockSpec((1, LANES), lambda b, t, i, p: (0, 0))],
            out_specs=pl.BlockSpec((None, T, NSA_WIDTH), lambda b, t, i, p: (b, 0, 0)),
        ),
        out_shape=jax.ShapeDtypeStruct((B, T, NSA_WIDTH), F32),
        compiler_params=pltpu.CompilerParams(dimension_semantics=("parallel", "arbitrary")),
        name="nsa_sample_sel",
    )(idx.reshape(-1), page_table.reshape(-1), *([pool_t] * len(kv_specs)), q, new_t, o_c, o_w, gates, gb)


def _nsa_sample_mixer(q, gates, kv_s, kv_w, page_table, pool_cmp_t, pool_sel_t, win_prev_t, W):
    B, T, _ = q.shape
    pos0 = page_table.shape[1] * PAGE_SIZE
    n_blocks = -(-(pos0 + T) // NSA_BLOCK)
    ck, cv_t = _split_cmp(_nsa_compress_paged(pool_cmp_t, page_table, W))
    o_c, o_w, idx = _nsa_sample_cmp(q, ck, cv_t, kv_w, win_prev_t, pos0, n_blocks)
    idx = jnp.swapaxes(idx[..., :min(NSA_TOPN, n_blocks)], 1, 2)
    return _nsa_sample_sel(q, idx, page_table, pool_sel_t, kv_s, o_c, o_w, gates, W, pos0)


def _mem_kv_body(x_ref, g_ref, w_ref, gk_ref, avg_ref, o_ref):
    h = _rms_rows(x_ref[...], g_ref[...]).astype(BF16)
    kv = lax.dot_general(w_ref[...], h, (((1,), (1,)), ((), ())), preferred_element_type=F32)
    k = kv[:MEM_WIDTH]
    o_ref[:MEM_WIDTH, :] = k * lax.rsqrt(_seg_mean_sq_rows(k, avg_ref[...]) + NORM_EPS) * gk_ref[...]
    o_ref[MEM_WIDTH:, :] = kv[MEM_WIDTH:]


def _mem_kv(mem, g, w_kv, k_g):
    B, m, d = mem.shape
    consts = [g.reshape(1, d), w_kv.T.astype(BF16), jnp.tile(k_g, MEM_HEADS).reshape(MEM_WIDTH, 1),
              _seg_avg_matrix(MEM_WIDTH, 1.0 / HEAD_DIM)]
    return pl.pallas_call(
        _mem_kv_body,
        grid=(B,),
        in_specs=[pl.BlockSpec((None, m, d), lambda b: (b, 0, 0))] + [_const_spec(c.shape) for c in consts],
        out_specs=pl.BlockSpec((None, 2 * MEM_WIDTH, m), lambda b: (b, 0, 0)),
        out_shape=jax.ShapeDtypeStruct((B, 2 * MEM_WIDTH, m), F32),
        compiler_params=pltpu.CompilerParams(dimension_semantics=("parallel",)),
        name="mem_kv",
    )(mem, *consts)


def _out_mem_body(x_ref, orw_ref, oml_ref, onsa_ref, kv_ref, w1_ref, w2_ref, w3_ref, g_ref, wq_ref, gq_ref,
                  avg_ref, wo_ref, o_ref):
    N = HEAD_DIM
    x = (x_ref[...] + _dot(orw_ref[...], w1_ref[...]) + _dot(oml_ref[...], w2_ref[...])
         + _dot(onsa_ref[...], w3_ref[...]))
    h = _rms_rows(x, g_ref[...]).astype(BF16)
    q = jnp.dot(h, wq_ref[...], preferred_element_type=F32)
    q = q * lax.rsqrt(_seg_mean_sq(q, avg_ref[...]) + NORM_EPS) * gq_ref[...] * (N ** -0.5)
    heads = []
    for hd in range(MEM_HEADS):
        s = _dot(q[:, hd * N:(hd + 1) * N], kv_ref[hd * N:(hd + 1) * N, :])
        e = jnp.exp(s - jnp.max(s, axis=-1, keepdims=True))
        p = e / jnp.sum(e, axis=-1, keepdims=True)
        heads.append(_dot_nt(p, kv_ref[MEM_WIDTH + hd * N:MEM_WIDTH + (hd + 1) * N, :]))
    o_ref[...] = x + _dot(jnp.concatenate(heads, axis=1), wo_ref[...])


def _out_mem(x, o_rw, o_ml, o_nsa, kv_t, w_out, g, w_q, q_g, w_o):
    B, T, d = x.shape
    tm = _row_tile(T, 512)
    w1 = w_out[:RW_WIDTH].astype(BF16)
    w2 = w_out[RW_WIDTH:RW_WIDTH + ML_WIDTH].astype(BF16)
    w3 = w_out[RW_WIDTH + ML_WIDTH:].astype(BF16)
    consts = [w1, w2, w3, g.reshape(1, d), w_q.astype(BF16), jnp.tile(q_g, MEM_HEADS).reshape(1, MEM_WIDTH),
              _seg_avg_matrix(MEM_WIDTH, 1.0 / HEAD_DIM), w_o.astype(BF16)]
    tile = lambda w: pl.BlockSpec((None, tm, w), lambda b, i: (b, i, 0))
    return pl.pallas_call(
        _out_mem_body,
        grid=(B, T // tm),
        in_specs=[tile(d), tile(RW_WIDTH), tile(ML_WIDTH), tile(NSA_WIDTH),
                  pl.BlockSpec((None,) + kv_t.shape[1:], lambda b, i: (b, 0, 0))]
                 + [_const_spec(c.shape) for c in consts],
        out_specs=tile(d),
        out_shape=jax.ShapeDtypeStruct((B, T, d), F32),
        compiler_params=pltpu.CompilerParams(dimension_semantics=("parallel", "parallel"),
                                             vmem_limit_bytes=VMEM_LIMIT),
        name="out_mem",
    )(x, o_rw, o_ml, o_nsa, kv_t, *consts)


def _layer(x, W, st, page_table, mem):
    B, T, d = x.shape
    is_prompt = st is None
    g = W['norm_g']
    G = NSA_KV_HEADS
    x1 = _ffn(x.reshape(B * T, d), g[0], W['ffa_up'], W['ffa_down']).reshape(B, T, d)
    proj = _in_proj(x1, g[1], W['w_in_perm'], W['nsa_qk_g'], channel_major=is_prompt)
    p_rw, p_ml, gates, q = proj[:4]
    if is_prompt:
        zeros = lambda *s: jnp.zeros(s, F32)
        st = {'rw_shift': zeros(B, 1, RW_COLS), 'rw_S': zeros(B, RW_HEADS, HEAD_DIM, HEAD_DIM),
              'ml_conv': zeros(B, ML_CONV - 1, ML_WIDTH), 'ml_C': zeros(B, ML_HEADS, HEAD_DIM, HEAD_DIM),
              'ml_n': zeros(B, ML_HEADS, HEAD_DIM), 'ml_m': zeros(B, ML_HEADS)}
        mem_kv_t = _mem_kv(mem, g[3], W['mem_w_kv'], W['mem_qk_g'][1])
    else:
        mem_kv_t = _channel_major(st['mem_kv'])
    o_rw, rw_S = _rwkv(p_rw, st['rw_shift'], st['rw_S'], W)
    o_ml, ml_C, ml_n, ml_m = _mlstm(p_ml, gates, st['ml_conv'], st['ml_C'], st['ml_n'], st['ml_m'], W)
    if is_prompt:
        ks, kw, kvc_t, kvs_t, kvw_t = proj[4:]
        ck, cv_t = _split_cmp(_nsa_compress_dense(kvc_t, W))
        o_nsa = _nsa_prompt(q, gates, ck, cv_t, ks, kvs_t, kw, kvw_t, W)
        new_kv = {'nsa_cmp': _token_major(kvc_t, 2, G), 'nsa_sel': _token_major(kvs_t, 2, G),
                  'nsa_win': _token_major(kvw_t[:, :, T - min(NSA_WINDOW, T):], 2, G)}
    else:
        kv_c, kv_s, kv_w = proj[4:]
        win_prev_t = _channel_major(st['nsa_win'])
        wb = win_prev_t.shape[2]
        o_nsa = _nsa_sample_mixer(q, gates, kv_s, kv_w, page_table, st['nsa_cmp_t'], st['nsa_sel_t'], win_prev_t, W)
        win_t = jnp.concatenate([win_prev_t, jnp.swapaxes(kv_w, 1, 2)], axis=2)
        kv5 = lambda z: z.reshape(B, T, 2, G, HEAD_DIM)
        new_kv = {'nsa_cmp': kv5(kv_c), 'nsa_sel': kv5(kv_s),
                  'nsa_win': _token_major(win_t[:, :, wb + T - min(NSA_WINDOW, wb + T):], 2, G)}
    x2 = _out_mem(x1, o_rw, o_ml, o_nsa, mem_kv_t, W['w_out'], g[2], W['mem_w_q'], W['mem_qk_g'][0], W['mem_w_o'])
    x3 = _ffn(x2.reshape(B * T, d), g[4], W['ffb_up'], W['ffb_down']).reshape(B, T, d)
    qk_in = p_ml[:, :, :ML_WIDTH]
    conv_all = jnp.concatenate([st['ml_conv'], qk_in], axis=1) if T < ML_CONV - 1 else qk_in
    new = dict(new_kv)
    new.update({'rw_shift': p_rw[:, T - 1:], 'rw_S': rw_S, 'ml_conv': conv_all[:, conv_all.shape[1] - (ML_CONV - 1):],
                'ml_C': ml_C, 'ml_n': ml_n, 'ml_m': ml_m})
    if is_prompt:
        new['mem_kv'] = _token_major(mem_kv_t, 2, MEM_HEADS)
    return x3, new


def kernel(x_prompt, x_sample, cache_nsa_cmp, cache_nsa_sel, cache_nsa_win, cache_mem_kv, state_rwkv_shift, state_rwkv_S, state_mlstm_conv, state_mlstm_C, state_mlstm_n, state_mlstm_m, page_table, mem_prompt, norm_g, ffa_up, ffa_down, ffb_up, ffb_down, w_in, w_out, rw_mu, rw_w0, rw_w2, rw_a0, rw_a2, rw_g2, rw_kk, rw_ka, rw_rk, rw_gn_g, rw_gn_b, ml_conv_w, ml_conv_b, ml_wq, ml_wk, ml_gate_b, ml_norm_g, ml_skip, nsa_qk_g, nsa_wpos, nsa_wc, nsa_gate_b, mem_w_q, mem_w_kv, mem_qk_g, mem_w_o):
    params = dict(norm_g=norm_g, ffa_up=ffa_up, ffa_down=ffa_down, ffb_up=ffb_up, ffb_down=ffb_down, w_out=w_out,
                  rw_mu=rw_mu, rw_w0=rw_w0, rw_w2=rw_w2, rw_a0=rw_a0, rw_a2=rw_a2, rw_g2=rw_g2, rw_kk=rw_kk,
                  rw_ka=rw_ka, rw_rk=rw_rk, rw_gn_g=rw_gn_g, rw_gn_b=rw_gn_b, ml_conv_w=ml_conv_w,
                  ml_conv_b=ml_conv_b, ml_wq=ml_wq, ml_wk=ml_wk, ml_gate_b=ml_gate_b, ml_norm_g=ml_norm_g,
                  ml_skip=ml_skip, nsa_qk_g=nsa_qk_g, nsa_wpos=nsa_wpos, nsa_wc=nsa_wc, nsa_gate_b=nsa_gate_b,
                  mem_w_q=mem_w_q, mem_w_kv=mem_w_kv, mem_qk_g=mem_qk_g, mem_w_o=mem_w_o)
    depth = norm_g.shape[0]
    y_p, y_s = x_prompt, x_sample
    new_p, new_s = [], []
    n_phys = cache_nsa_cmp.shape[1]
    all_pages = lambda c: _channel_major(c.reshape((depth * n_phys,) + c.shape[2:]))
    pool_cmp_t, pool_sel_t = all_pages(cache_nsa_cmp), all_pages(cache_nsa_sel)
    for l in range(depth):
        W = {name: v[l] for name, v in params.items()}
        W['w_in_perm'] = _permute_w_in(w_in[l])
        st = {'nsa_cmp_t': pool_cmp_t, 'nsa_sel_t': pool_sel_t, 'nsa_win': cache_nsa_win[l],
              'mem_kv': cache_mem_kv[l], 'rw_shift': state_rwkv_shift[l], 'rw_S': state_rwkv_S[l],
              'ml_conv': state_mlstm_conv[l], 'ml_C': state_mlstm_C[l], 'ml_n': state_mlstm_n[l],
              'ml_m': state_mlstm_m[l]}
        y_p, sp = _layer(y_p, W, None, None, mem_prompt)
        y_s, ss = _layer(y_s, W, st, page_table + l * n_phys, None)
        new_p.append(sp)
        new_s.append(ss)
    P = lambda name: jnp.stack([d[name] for d in new_p])
    S = lambda name: jnp.stack([d[name] for d in new_s])
    return (y_p, y_s,
            P('nsa_cmp'), S('nsa_cmp'), P('nsa_sel'), S('nsa_sel'), P('nsa_win'), S('nsa_win'),
            P('mem_kv'),
            P('rw_shift'), S('rw_shift'), P('rw_S'), S('rw_S'),
            P('ml_conv'), S('ml_conv'), P('ml_C'), S('ml_C'), P('ml_n'), S('ml_n'), P('ml_m'), S('ml_m'))
```

```python
import functools

import jax
import jax.numpy as jnp
from jax import lax
from jax.experimental import pallas as pl
from jax.experimental.pallas import tpu as pltpu

F32 = jnp.float32
BF16 = jnp.bfloat16

HEAD_DIM = 64
RW_HEADS = 4
RW_WIDTH = RW_HEADS * HEAD_DIM
RW_W_LORA = 64
RW_A_LORA = 64
RW_G_LORA = 128
RW_COLS = 3 * RW_WIDTH + RW_W_LORA + RW_A_LORA + RW_G_LORA
RW_GN_EPS = 64e-5
ML_HEADS = 4
ML_WIDTH = ML_HEADS * HEAD_DIM
ML_CONV = 4
RW_CHUNKS = (64,)
ML_CHUNKS = (128, 64)
ML_COLS = 3 * ML_WIDTH + 2 * ML_HEADS
NSA_HEADS = 8
NSA_KV_HEADS = 2
NSA_GROUP = NSA_HEADS // NSA_KV_HEADS
NSA_WIDTH = NSA_HEADS * HEAD_DIM
NSA_KV = NSA_KV_HEADS * HEAD_DIM
NSA_BLOCK = 64
BLOCK_SHIFT = 6
NSA_TOPN = 8
NSA_WINDOW = 512
NSA_COLS = NSA_WIDTH + 6 * NSA_KV + 3 * NSA_HEADS
MEM_HEADS = 4
MEM_WIDTH = MEM_HEADS * HEAD_DIM
PAGE_SIZE = 128
NORM_EPS = 1e-6
BIG = 1e9
NEG = -1e30
LOG2_E = 1.4426950408889634
RW_CHAIN_BATCH = 4
ML_CHAIN_BATCH = 4
LANES = 128
VMEM_LIMIT = 56 * 1024 * 1024

P_RW = 0
P_ML = P_RW + RW_COLS
P_Q = P_ML + 3 * ML_WIDTH
P_KVC = P_Q + NSA_WIDTH
P_KVS = P_KVC + 2 * NSA_KV
P_KVW = P_KVS + 2 * NSA_KV
P_GATES = P_KVW + 2 * NSA_KV
P_TOTAL = P_GATES + LANES


def _dot(a, b):
    return jnp.dot(a.astype(BF16), b.astype(BF16), preferred_element_type=F32)


def _dot_nt(a, b):
    return lax.dot_general(a.astype(BF16), b.astype(BF16), (((1,), (1,)), ((), ())),
                           preferred_element_type=F32)


def _dot_tn(a, b):
    return lax.dot_general(a.astype(BF16), b.astype(BF16), (((0,), (0,)), ((), ())),
                           preferred_element_type=F32)


def _split2(x):
    hi = x.astype(BF16)
    lo = (x - hi.astype(F32)).astype(BF16)
    return hi, lo


def _split3(x):
    hi = x.astype(BF16)
    r = x - hi.astype(F32)
    mid = r.astype(BF16)
    lo = (r - mid.astype(F32)).astype(BF16)
    return hi, mid, lo


def _dot_sel(sel, x):
    hi, mid, lo = _split3(x)
    s = sel.astype(BF16)
    d = lambda t: jnp.dot(s, t, preferred_element_type=F32)
    return d(hi) + d(mid) + d(lo)


def _dot_x_sel(x, sel):
    hi, mid, lo = _split3(x)
    s = sel.astype(BF16)
    d = lambda t: jnp.dot(t, s, preferred_element_type=F32)
    return d(hi) + d(mid) + d(lo)


def _dot_hi(a, b):
    ah, al = _split2(a)
    bh, bl = _split2(b)
    d = lambda u, v: jnp.dot(u, v, preferred_element_type=F32)
    return d(ah, bh) + d(al, bh) + d(ah, bl)


def _dot_nt_hi(a, b):
    ah, al = _split2(a)
    bh, bl = _split2(b)
    d = lambda u, v: lax.dot_general(u, v, (((1,), (1,)), ((), ())), preferred_element_type=F32)
    return d(ah, bh) + d(al, bh) + d(ah, bl)


def _seg_mean_sq(x, seg_avg):
    hi, lo = _split2(x * x)
    d = lambda t: jnp.dot(t, seg_avg, preferred_element_type=F32)
    return d(hi) + d(lo)


def _seg_mean_sq_rows(x, seg_avg):
    hi, lo = _split2(x * x)
    d = lambda t: jnp.dot(seg_avg, t, preferred_element_type=F32)
    return d(hi) + d(lo)


def _rms_rows(x, g):
    return x * lax.rsqrt(jnp.mean(x * x, axis=-1, keepdims=True) + NORM_EPS) * g


def _sigmoid(x):
    return 1.0 / (1.0 + jnp.exp(-x))


def _softplus(x):
    return jnp.maximum(x, 0.0) + jnp.log(1.0 + jnp.exp(-jnp.abs(x)))


def _row_tile(n, target):
    t = min(n, target)
    while n % t:
        t //= 2
    return t


def _scan_chunk(t, candidates):
    for c in candidates:
        if t % c == 0:
            return c
    return t


def _const_spec(shape):
    nd = len(shape)
    return pl.BlockSpec(shape, lambda *_: (0,) * nd, pipeline_mode=pl.Buffered(1))


def _seg_avg_matrix(width, scale):
    i = jnp.arange(width) // HEAD_DIM
    return (jnp.where(i[:, None] == i[None, :], scale, 0.0)).astype(BF16)


def _channel_major(x):
    b, t = x.shape[:2]
    return jnp.transpose(x, (0, 2, 3, 4, 1)).reshape(b, -1, t)


def _token_major(x, c, g):
    b, w, t = x.shape
    return jnp.transpose(x.reshape(b, c, g, w // (c * g), t), (0, 4, 1, 2, 3))


def _ffn_body(x_ref, g_ref, wg_ref, wu_ref, wd_ref, o_ref, *, f_chunk):
    x = x_ref[...]
    h = _rms_rows(x, g_ref[...]).astype(BF16)
    d_ff = wg_ref.shape[1]
    acc = jnp.zeros_like(x)
    for c in range(d_ff // f_chunk):
        sl = slice(c * f_chunk, (c + 1) * f_chunk)
        gate = jnp.dot(h, wg_ref[:, sl], preferred_element_type=F32)
        up = jnp.dot(h, wu_ref[:, sl], preferred_element_type=F32)
        act = (gate * _sigmoid(gate) * up).astype(BF16)
        acc = acc + jnp.dot(act, wd_ref[sl, :], preferred_element_type=F32)
    o_ref[...] = x + 0.5 * acc


def _ffn(x, g, w_up, w_down):
    n, d = x.shape
    d_ff = w_down.shape[0]
    tm = _row_tile(n, 512)
    f_chunk = d_ff
    wg = w_up[:, :d_ff].astype(BF16)
    wu = w_up[:, d_ff:].astype(BF16)
    wd = w_down.astype(BF16)
    row = pl.BlockSpec((tm, d), lambda i: (i, 0))
    return pl.pallas_call(
        functools.partial(_ffn_body, f_chunk=f_chunk),
        grid=(n // tm,),
        in_specs=[row, _const_spec((1, d)), _const_spec((d, d_ff)), _const_spec((d, d_ff)),
                  _const_spec((d_ff, d))],
        out_specs=row,
        out_shape=jax.ShapeDtypeStruct((n, d), F32),
        compiler_params=pltpu.CompilerParams(dimension_semantics=("parallel",),
                                             vmem_limit_bytes=VMEM_LIMIT),
        name="ffn",
    )(x, g.reshape(1, d), wg, wu, wd)


def _in_proj_body(x_ref, g_ref, w_ref, gq_ref, gks_ref, gkw_ref, avg_ref, rw_ref, ml_ref, gates_ref, q_ref,
                  *kv_refs, channel_major):
    h = _rms_rows(x_ref[...], g_ref[...]).astype(BF16)
    p = jnp.dot(h, w_ref[...], preferred_element_type=F32)
    rw_ref[...] = p[:, P_RW:P_ML]
    ml_ref[...] = p[:, P_ML:P_Q]
    gates_ref[...] = p[:, P_GATES:P_TOTAL]
    avg = avg_ref[...]

    def head_norm(z, g):
        w = z.shape[1]
        return z * lax.rsqrt(_seg_mean_sq(z, avg[:w, :w]) + NORM_EPS) * g

    q_ref[...] = head_norm(p[:, P_Q:P_KVC], gq_ref[...])
    kvc = p[:, P_KVC:P_KVS]
    ks = head_norm(p[:, P_KVS:P_KVS + NSA_KV], gks_ref[...])
    vs = p[:, P_KVS + NSA_KV:P_KVW]
    kw = head_norm(p[:, P_KVW:P_KVW + NSA_KV], gkw_ref[...])
    vw = p[:, P_KVW + NSA_KV:P_GATES]
    if channel_major:
        ks_ref, kw_ref, kvc_t_ref, kvs_t_ref, kvw_t_ref = kv_refs
        ks_ref[...] = ks.astype(BF16)
        kw_ref[...] = kw.astype(BF16)
        kvc_t_ref[...] = kvc.T
        kvs_t_ref[:NSA_KV, :] = ks.T
        kvs_t_ref[NSA_KV:, :] = vs.T
        kvw_t_ref[:NSA_KV, :] = kw.T
        kvw_t_ref[NSA_KV:, :] = vw.T
    else:
        kvc_ref, kvs_ref, kvw_ref = kv_refs
        kvc_ref[...] = kvc
        kvs_ref[:, :NSA_KV] = ks
        kvs_ref[:, NSA_KV:] = vs
        kvw_ref[:, :NSA_KV] = kw
        kvw_ref[:, NSA_KV:] = vw


def _permute_w_in(w_in):
    d = w_in.shape[0]
    o_ml = RW_COLS
    o_nsa = RW_COLS + ML_COLS
    ml_gates = w_in[:, o_ml + 3 * ML_WIDTH:o_nsa]
    nsa_gates = w_in[:, o_nsa + NSA_WIDTH + 6 * NSA_KV:]
    pad = jnp.zeros((d, LANES - 2 * ML_HEADS - 3 * NSA_HEADS), w_in.dtype)
    return jnp.concatenate([w_in[:, :o_ml + 3 * ML_WIDTH], w_in[:, o_nsa:o_nsa + NSA_WIDTH + 6 * NSA_KV],
                            ml_gates, nsa_gates, pad], axis=1).astype(BF16)


def _in_proj(x, g, w_perm, qk_g, channel_major):
    B, T, d = x.shape
    n = B * T
    tm = _row_tile(T if channel_major else n, 512)
    tpb = T // tm
    row = lambda w: pl.BlockSpec((tm, w), lambda i: (i, 0))
    widths = [RW_COLS, 3 * ML_WIDTH, LANES, NSA_WIDTH]
    out_specs = [row(w) for w in widths]
    out_shape = [jax.ShapeDtypeStruct((n, w), F32) for w in widths]
    w_kv = 2 * NSA_KV
    if channel_major:
        out_specs += [row(NSA_KV)] * 2 + [pl.BlockSpec((None, w_kv, tm), lambda i: (i // tpb, 0, i % tpb))] * 3
        out_shape += [jax.ShapeDtypeStruct((n, NSA_KV), BF16)] * 2 + [jax.ShapeDtypeStruct((B, w_kv, T), F32)] * 3
    else:
        out_specs += [row(w_kv)] * 3
        out_shape += [jax.ShapeDtypeStruct((n, w_kv), F32)] * 3
    gq = jnp.tile(qk_g[0], NSA_HEADS).reshape(1, NSA_WIDTH)
    gks = jnp.tile(qk_g[2], NSA_KV_HEADS).reshape(1, NSA_KV)
    gkw = jnp.tile(qk_g[3], NSA_KV_HEADS).reshape(1, NSA_KV)
    avg = _seg_avg_matrix(NSA_WIDTH, 1.0 / HEAD_DIM)
    outs = pl.pallas_call(
        functools.partial(_in_proj_body, channel_major=channel_major),
        grid=(n // tm,),
        in_specs=[row(d), _const_spec((1, d)), _const_spec((d, P_TOTAL)), _const_spec((1, NSA_WIDTH)),
                  _const_spec((1, NSA_KV)), _const_spec((1, NSA_KV)), _const_spec((NSA_WIDTH, NSA_WIDTH))],
        out_specs=out_specs,
        out_shape=out_shape,
        compiler_params=pltpu.CompilerParams(dimension_semantics=("parallel",),
                                             vmem_limit_bytes=VMEM_LIMIT),
        name="in_proj",
    )(x.reshape(n, d), g.reshape(1, d), w_perm, gq, gks, gkw, avg)
    return [o.reshape(B, T, -1) if o.shape[0] == n else o for o in outs]


def _rwkv_chunk_inputs(p, prev_row, mu_ref, w0_ref, lw_ref, a0_ref, g2_ref, kks_ref, ka_ref, rk_ref, tri_ref,
                       seg_ref):
    C = p.shape[0]
    rows = lax.broadcasted_iota(jnp.int32, (C, 1), 0)
    prev = jnp.where(rows == 0, prev_row, pltpu.roll(p, 1, axis=0))
    xm = p + (prev - p) * mu_ref[...]
    r = xm[:, 0:RW_WIDTH]
    k = xm[:, RW_WIDTH:2 * RW_WIDTH]
    v = xm[:, 2 * RW_WIDTH:3 * RW_WIDTH]
    lin = xm[:, 3 * RW_WIDTH:3 * RW_WIDTH + RW_W_LORA + RW_A_LORA]
    lane = lax.broadcasted_iota(jnp.int32, lin.shape, 1)
    lora = _dot_hi(jnp.where(lane < RW_W_LORA, jnp.tanh(lin), lin), lw_ref[...])
    w = -_softplus(-(w0_ref[...] + lora[:, :RW_WIDTH])) - 0.5
    a = _sigmoid(a0_ref[...] + lora[:, RW_WIDTH:])
    g = _dot(_sigmoid(xm[:, 3 * RW_WIDTH + RW_W_LORA + RW_A_LORA:]), g2_ref[...])
    kk = k * kks_ref[...]
    hi, lo = _split2(kk * kk)
    seg = seg_ref[...]
    ss = jnp.dot(hi, seg, preferred_element_type=F32) + jnp.dot(lo, seg, preferred_element_type=F32)
    kk = kk / jnp.maximum(jnp.sqrt(ss), 1e-12)
    k = k * (1.0 + (a - 1.0) * ka_ref[...])
    logdec = -jnp.exp(w)
    tri = tri_ref[...]
    G = _dot_sel(tri, logdec)
    g_end = G[C - 1:C, :]
    e_g = jnp.exp(G)
    e_gi = jnp.exp(-G)
    kkd = kk * jnp.exp(G - logdec)
    rd = r * e_g
    b = kk * a
    bi = b * e_gi
    ki = k * e_gi
    e_end = jnp.exp(g_end - G)
    bi2 = b * e_end
    ki2 = k * e_end
    dec_end = jnp.exp(g_end)
    rkk = r * k * rk_ref[...]
    return dict(v=v, g=g, kkd=kkd, rd=rd, bi=bi, ki=ki, bi2=bi2, ki2=ki2, dec_end=dec_end, rkk=rkk)


def _rwkv_body(p_ref, shift_ref, s0_ref, mu_ref, w0_ref, lw_ref, a0_ref, g2_ref, kks_ref, ka_ref,
               rk_ref, gng_ref, gnb_ref, tri_ref, seg_ref, o_ref, st_ref, carry_ref, s_ref, *, chunk, nb):
    C = chunk
    N = HEAD_DIM

    @pl.when(pl.program_id(1) == 0)
    def _():
        carry_ref[...] = shift_ref[...]
        s_ref[...] = s0_ref[...]

    pre = []
    for bi in range(nb):
        p = p_ref[bi]
        pre.append(_rwkv_chunk_inputs(p, carry_ref[bi], mu_ref, w0_ref, lw_ref, a0_ref, g2_ref, kks_ref, ka_ref,
                                      rk_ref, tri_ref, seg_ref))
        carry_ref[bi] = p[C - 1:C, :]

    ti = lax.broadcasted_iota(jnp.int32, (C, C), 0)
    si = lax.broadcasted_iota(jnp.int32, (C, C), 1)
    strict = si < ti
    eye = (si == ti).astype(F32)
    s2 = lax.broadcasted_iota(jnp.int32, (C, 2 * C), 1)
    incl2 = jnp.where(s2 < C, s2, s2 - C) <= lax.broadcasted_iota(jnp.int32, (C, 2 * C), 0)

    units = [(bi, h) for bi in range(nb) for h in range(RW_HEADS)]
    col = lambda name, u: pre[u[0]][name][:, u[1] * N:(u[1] + 1) * N]
    lhs = [jnp.concatenate([col('kkd', u), col('rd', u)], axis=0) for u in units]
    rhs = [jnp.concatenate([col('bi', u), col('ki', u)], axis=0) for u in units]
    vh = [col('v', u) for u in units]
    s0 = [s_ref[bi, h] for bi, h in units]
    m4 = [_dot_nt(a, b) for a, b in zip(lhs, rhs)]
    ks = [_dot_nt(a, s) for a, s in zip(lhs, s0)]
    pw = [jnp.where(strict, -m[:C, :C], 0.0) for m in m4]
    lk = [jnp.where(strict, m[:C, C:], 0.0) for m in m4]
    mbk = [jnp.where(incl2, m[C:, :], 0.0) for m in m4]
    lkv = [_dot(a, b) for a, b in zip(lk, vh)]
    t_inv = [eye + n for n in pw]
    span = 2
    while span < C:
        pw = [_dot(x, x) for x in pw]
        t_inv = [t + _dot(t, x) for t, x in zip(t_inv, pw)]
        span *= 2
    u_ = [_dot(t, -(k_[:C] + l)) for t, k_, l in zip(t_inv, ks, lkv)]
    uv = [jnp.concatenate([a, b], axis=0) for a, b in zip(u_, vh)]
    y = [k_[C:] + _dot(m, x) for k_, m, x in zip(ks, mbk, uv)]
    s_new = [s * col('dec_end', u) + _dot_tn(x, jnp.concatenate([col('bi2', u), col('ki2', u)], axis=0))
             for s, x, u in zip(s0, uv, units)]
    for (bi, h), s, yy, v_ in zip(units, s_new, y, vh):
        sl = slice(h * N, (h + 1) * N)
        s_ref[bi, h] = s
        mean = jnp.mean(yy, axis=-1, keepdims=True)
        var = jnp.mean(jnp.square(yy - mean), axis=-1, keepdims=True)
        yn = (yy - mean) * lax.rsqrt(var + RW_GN_EPS) * gng_ref[:, sl] + gnb_ref[:, sl]
        bonus = jnp.sum(pre[bi]['rkk'][:, sl], axis=-1, keepdims=True) * v_
        o_ref[bi, :, sl] = (yn + bonus) * pre[bi]['g'][:, sl]

    @pl.when(pl.program_id(1) == pl.num_programs(1) - 1)
    def _():
        st_ref[...] = s_ref[...]


def _rwkv(p_rw, shift_prev, s0, W):
    B, T, _ = p_rw.shape
    C = _scan_chunk(T, RW_CHUNKS)
    nb = _row_tile(B, RW_CHAIN_BATCH)
    row = lambda v: v.reshape(1, -1).astype(F32)
    z = jnp.zeros((RW_W_LORA, RW_WIDTH), F32)
    lw = jnp.concatenate([jnp.concatenate([W['rw_w2'], z], axis=1),
                          jnp.concatenate([z, W['rw_a2']], axis=1)], axis=0)
    tri = (jnp.arange(C)[:, None] >= jnp.arange(C)[None, :]).astype(BF16)
    seg = _seg_avg_matrix(RW_WIDTH, 1.0)
    consts = [row(W['rw_mu']), row(W['rw_w0']), lw, row(W['rw_a0']), W['rw_g2'].astype(BF16), row(W['rw_kk']),
              row(W['rw_ka']), row(W['rw_rk']), row(W['rw_gn_g']), row(W['rw_gn_b']), tri, seg]
    state = pl.BlockSpec((nb, RW_HEADS, HEAD_DIM, HEAD_DIM), lambda b, c: (b, 0, 0, 0))
    out, s_t = pl.pallas_call(
        functools.partial(_rwkv_body, chunk=C, nb=nb),
        grid=(B // nb, T // C),
        in_specs=[pl.BlockSpec((nb, C, RW_COLS), lambda b, c: (b, c, 0)),
                  pl.BlockSpec((nb, 1, RW_COLS), lambda b, c: (b, 0, 0)), state]
                 + [_const_spec(c.shape) for c in consts],
        out_specs=[pl.BlockSpec((nb, C, RW_WIDTH), lambda b, c: (b, c, 0)), state],
        out_shape=[jax.ShapeDtypeStruct((B, T, RW_WIDTH), F32),
                   jax.ShapeDtypeStruct((B, RW_HEADS, HEAD_DIM, HEAD_DIM), F32)],
        scratch_shapes=[pltpu.VMEM((nb, 1, RW_COLS), F32), pltpu.VMEM((nb, RW_HEADS, HEAD_DIM, HEAD_DIM), F32)],
        compiler_params=pltpu.CompilerParams(dimension_semantics=("parallel", "arbitrary"),
                                             vmem_limit_bytes=VMEM_LIMIT),
        name="rwkv7",
    )(p_rw, shift_prev, s0, *consts)
    return out, s_t


def _mlstm_body(p_ref, gc_ref, gt_ref, convp_ref, c0_ref, n0_ref, m0_ref, cw_ref, cb_ref, wq_ref, wk_ref,
                bias_r_ref, bias_c_ref, ng_ref, skip_ref, tril_ref, triu_ref,
                o_ref, ct_ref, nt_ref, mt_ref, ext_ref, c_ref, n_ref, m_ref, *, chunk, nb):
    L = chunk
    N = HEAD_DIM
    H = ML_HEADS

    @pl.when(pl.program_id(1) == 0)
    def _():
        ext_ref[:, 0:8, :] = convp_ref[...]
        c_ref[...] = c0_ref[...]
        n_ref[...] = n0_ref[...]
        m_ref[...] = m0_ref[...]

    cw = cw_ref[...]
    pre = []
    for bi in range(nb):
        p = p_ref[bi]
        x = p[:, 0:ML_WIDTH]
        ext_ref[bi, 8:8 + L, :] = x
        conv = (cb_ref[...] + cw[3:4, :] * x + cw[2:3, :] * ext_ref[bi, 7:7 + L, :]
                + cw[1:2, :] * ext_ref[bi, 6:6 + L, :] + cw[0:1, :] * ext_ref[bi, 5:5 + L, :])
        tail = ext_ref[bi, L:L + 8, :]
        ext_ref[bi, 0:8, :] = tail
        ca = conv * _sigmoid(conv)
        gcb = gc_ref[bi] + bias_r_ref[...]
        gtb = gt_ref[bi, 0] + bias_c_ref[...][:, 0:1]
        pre.append(dict(
            v=p[:, ML_WIDTH:2 * ML_WIDTH], o_pre=p[:, 2 * ML_WIDTH:3 * ML_WIDTH], ca=ca,
            q=_dot(ca, wq_ref[...]), k=_dot(ca, wk_ref[...]) * (N ** -0.5), gcb=gcb, gtb=gtb,
            bcum_c=_dot_sel(tril_ref[...], -_softplus(-gcb)),
            bcum_r=_dot_x_sel(-_softplus(-gtb), triu_ref[...])))

    ti = lax.broadcasted_iota(jnp.int32, (L, L), 0)
    si = lax.broadcasted_iota(jnp.int32, (L, L), 1)
    causal = si <= ti

    units = [(bi, h) for bi in range(nb) for h in range(H)]
    col = lambda name, u: pre[u[0]][name][:, u[1] * N:(u[1] + 1) * N]
    qh = [col('q', u) for u in units]
    kh = [col('k', u) for u in units]
    vh = [col('v', u) for u in units]
    cs = [c_ref[bi, h] for bi, h in units]
    ns = [n_ref[bi, h:h + 1, :] for bi, h in units]
    m_prev = [m_ref[bi, h:h + 1, 0:1] for bi, h in units]
    b_c = [pre[bi]['bcum_c'][:, H + h:H + h + 1] for bi, h in units]
    i_c = [pre[bi]['gcb'][:, h:h + 1] for bi, h in units]
    b_r = [pre[bi]['bcum_r'][H + h:H + h + 1, :] for bi, h in units]
    i_r = [pre[bi]['gtb'][h:h + 1, :] for bi, h in units]
    qk = [_dot_nt(a, b) for a, b in zip(qh, kh)]
    qc = [_dot_nt(a, c) for a, c in zip(qh, cs)]
    dmat = [jnp.where(causal, bc - br + ir, -jnp.inf) for bc, br, ir in zip(b_c, b_r, i_r)]
    inter = [bc + mp for bc, mp in zip(b_c, m_prev)]
    m_t = [jnp.maximum(it, jnp.max(dm, axis=-1, keepdims=True)) for it, dm in zip(inter, dmat)]
    a = [x * jnp.exp(dm - mt) for x, dm, mt in zip(qk, dmat, m_t)]
    av = [_dot(x, v_) for x, v_ in zip(a, vh)]
    b_end = [bc[L - 1:L, :] for bc in b_c]
    m_new = [jnp.maximum(be + mp, jnp.max(be - br + ir, axis=-1, keepdims=True))
             for be, mp, br, ir in zip(b_end, m_prev, b_r, i_r)]
    dec = [jnp.exp(be + mp - mn) for be, mp, mn in zip(b_end, m_prev, m_new)]
    wg = [jnp.exp(be - bc + ic - mn) for be, bc, ic, mn in zip(b_end, b_c, i_c, m_new)]
    c_new = [d * c + _dot_tn(w * v_, k_) for d, c, w, v_, k_ in zip(dec, cs, wg, vh, kh)]
    for idx, (bi, h) in enumerate(units):
        sl = slice(h * N, (h + 1) * N)
        sc = jnp.exp(inter[idx] - m_t[idx])
        num = sc * qc[idx] + av[idx]
        den = (sc * jnp.sum(qh[idx] * ns[idx], axis=-1, keepdims=True)
               + jnp.sum(a[idx], axis=-1, keepdims=True))
        hh = num / jnp.maximum(jnp.abs(den), jnp.exp(-m_t[idx]))
        c_ref[bi, h] = c_new[idx]
        n_ref[bi, h:h + 1, :] = dec[idx] * ns[idx] + jnp.sum(wg[idx] * kh[idx], axis=0, keepdims=True)
        m_ref[bi, h:h + 1, :] = jnp.broadcast_to(m_new[idx], (1, LANES))
        hn = hh * lax.rsqrt(jnp.mean(hh * hh, axis=-1, keepdims=True) + NORM_EPS)
        o_ref[bi, :, sl] = ((hn * ng_ref[:, sl] + skip_ref[:, sl] * pre[bi]['ca'][:, sl])
                            * _sigmoid(pre[bi]['o_pre'][:, sl]))

    @pl.when(pl.program_id(1) == pl.num_programs(1) - 1)
    def _():
        ct_ref[...] = c_ref[...]
        nt_ref[...] = n_ref[...]
        mt_ref[...] = m_ref[...]


def _block_diag(w):
    H, N, _ = w.shape
    eye = jnp.eye(H, dtype=w.dtype)
    return (eye[:, None, :, None] * w[:, :, None, :]).reshape(H * N, H * N)


def _mlstm(p_ml, gates, conv_prev, c0, n0, m0, W):
    B, T, _ = p_ml.shape
    H, N = ML_HEADS, HEAD_DIM
    L = _scan_chunk(T, ML_CHUNKS)
    nc = T // L
    gt = jnp.swapaxes(gates[:, :, :2 * H].reshape(B, nc, L, 2 * H), 2, 3)
    convp = jnp.concatenate([jnp.zeros((B, 8 - (ML_CONV - 1), ML_WIDTH), F32), conv_prev], axis=1)
    m0b = jnp.broadcast_to(jnp.pad(m0, ((0, 0), (0, 8 - H)))[:, :, None], (B, 8, LANES))
    bias = W['ml_gate_b'].reshape(2 * H)
    bias_r = jnp.pad(bias, (0, LANES - 2 * H)).reshape(1, LANES)
    bias_c = jnp.broadcast_to(bias[:, None], (2 * H, LANES))
    tril = (jnp.arange(L)[:, None] >= jnp.arange(L)[None, :]).astype(BF16)
    row = lambda z: z.reshape(1, -1).astype(F32)
    consts = [W['ml_conv_w'], row(W['ml_conv_b']), _block_diag(W['ml_wq']).astype(BF16),
              _block_diag(W['ml_wk']).astype(BF16), bias_r, bias_c, row(W['ml_norm_g']), row(W['ml_skip']),
              tril, tril.T]
    nb = _row_tile(B, ML_CHAIN_BATCH)
    per_b = lambda *shape: pl.BlockSpec((nb,) + shape, lambda b, c: (b,) + (0,) * len(shape))
    out, c_t, n_t, m_t = pl.pallas_call(
        functools.partial(_mlstm_body, chunk=L, nb=nb),
        grid=(B // nb, nc),
        in_specs=[pl.BlockSpec((nb, L, 3 * ML_WIDTH), lambda b, c: (b, c, 0)),
                  pl.BlockSpec((nb, L, LANES), lambda b, c: (b, c, 0)),
                  pl.BlockSpec((nb, 1, 2 * H, L), lambda b, c: (b, c, 0, 0)),
                  per_b(8, ML_WIDTH), per_b(H, N, N), per_b(H, N), per_b(8, LANES)]
                 + [_const_spec(c.shape) for c in consts],
        out_specs=[pl.BlockSpec((nb, L, ML_WIDTH), lambda b, c: (b, c, 0)),
                   per_b(H, N, N), per_b(H, N), per_b(8, LANES)],
        out_shape=[jax.ShapeDtypeStruct((B, T, ML_WIDTH), F32), jax.ShapeDtypeStruct((B, H, N, N), F32),
                   jax.ShapeDtypeStruct((B, H, N), F32), jax.ShapeDtypeStruct((B, 8, LANES), F32)],
        scratch_shapes=[pltpu.VMEM((nb, L + 8, ML_WIDTH), F32), pltpu.VMEM((nb, H, N, N), F32),
                        pltpu.VMEM((nb, H, N), F32), pltpu.VMEM((nb, 8, LANES), F32)],
        compiler_params=pltpu.CompilerParams(dimension_semantics=("parallel", "arbitrary"),
                                             vmem_limit_bytes=VMEM_LIMIT),
        name="mlstm",
    )(p_ml, gates, gt, convp, c0, n0, m0b, *consts)
    return out, c_t, n_t, m_t[:, :H, 0]


def _compress_body(*refs, n_parts, paged):
    if paged:
        refs = refs[1:]
    part_refs = refs[:n_parts]
    wpos_ref, sel_ref, wc_ref, gk_ref, avg_ref, o_ref = refs[n_parts:]
    wpos = wpos_ref[...]
    parts = []
    for r in part_refs:
        x = r[...]
        parts += [x[:, c * LANES:(c + 1) * LANES] * wpos for c in range(x.shape[1] // LANES)]
    c = _dot(jnp.concatenate(parts, axis=1), sel_ref[...])
    c = _dot_hi(wc_ref[...], c)
    ck = c[:NSA_KV]
    o_ref[:NSA_KV, :] = ck * lax.rsqrt(_seg_mean_sq_rows(ck, avg_ref[...]) + NORM_EPS) * gk_ref[...]
    o_ref[NSA_KV:, :] = c[NSA_KV:]


def _compress_consts(W, n_rows):
    w = 2 * NSA_KV
    wpos = W['nsa_wpos'].reshape(NSA_BLOCK, w).T
    wc_t = _block_diag(W['nsa_wc'].reshape(2 * NSA_KV_HEADS, HEAD_DIM, HEAD_DIM)).T
    gk = jnp.tile(W['nsa_qk_g'][1], NSA_KV_HEADS).reshape(NSA_KV, 1)
    sel = (jnp.arange(n_rows)[:, None] // NSA_BLOCK == jnp.arange(n_rows // NSA_BLOCK)[None, :]).astype(BF16)
    return [jnp.concatenate([wpos, wpos], axis=1), sel, wc_t, gk, _seg_avg_matrix(NSA_KV, 1.0 / HEAD_DIM)]


def _nsa_compress_dense(kvc_t, W):
    B, w, T = kvc_t.shape
    consts = _compress_consts(W, T)
    return pl.pallas_call(
        functools.partial(_compress_body, n_parts=1, paged=False),
        grid=(B,),
        in_specs=[pl.BlockSpec((None, w, T), lambda b: (b, 0, 0))] + [_const_spec(c.shape) for c in consts],
        out_specs=pl.BlockSpec((None, w, T // NSA_BLOCK), lambda b: (b, 0, 0)),
        out_shape=jax.ShapeDtypeStruct((B, w, T // NSA_BLOCK), F32),
        compiler_params=pltpu.CompilerParams(dimension_semantics=("parallel",), vmem_limit_bytes=VMEM_LIMIT),
        name="nsa_compress_dense",
    )(kvc_t, *consts)


def _nsa_compress_paged(pool_t, page_table, W):
    B, n_pages = page_table.shape
    w = pool_t.shape[1]
    pages = min(64, n_pages)
    bpp = PAGE_SIZE // NSA_BLOCK
    consts = _compress_consts(W, pages * PAGE_SIZE)

    def page_spec(k):
        return pl.BlockSpec((None, w, PAGE_SIZE), lambda b, i, pt: (pt[b * n_pages + i * pages + k], 0, 0))

    const = lambda c: pl.BlockSpec(c.shape, lambda b, i, pt: (0,) * c.ndim)
    return pl.pallas_call(
        functools.partial(_compress_body, n_parts=pages, paged=True),
        grid_spec=pltpu.PrefetchScalarGridSpec(
            num_scalar_prefetch=1,
            grid=(B, n_pages // pages),
            in_specs=[page_spec(k) for k in range(pages)] + [const(c) for c in consts],
            out_specs=pl.BlockSpec((None, w, pages * bpp), lambda b, i, pt: (b, 0, i)),
        ),
        out_shape=jax.ShapeDtypeStruct((B, w, n_pages * bpp), F32),
        compiler_params=pltpu.CompilerParams(dimension_semantics=("parallel", "parallel"),
                                             vmem_limit_bytes=VMEM_LIMIT),
        name="nsa_compress_paged",
    )(page_table.reshape(-1), *([pool_t] * pages), *consts)


def _split_cmp(cmp_t):
    return jnp.swapaxes(cmp_t[:, :NSA_KV], 1, 2), cmp_t[:, NSA_KV:]


def _gate_bias_row(gate_b):
    return jnp.pad(gate_b, (2 * ML_HEADS, LANES - 2 * ML_HEADS - 3 * NSA_HEADS)).reshape(1, LANES)


def _gate_lane(g, r):
    return 2 * ML_HEADS + (g * NSA_GROUP + r) * 3


def _attend_t(k_tile, v_t, q_all, bias, carry):
    if not isinstance(k_tile, (list, tuple)):
        k_tile = [k_tile] * len(q_all)
    tk = k_tile[0].shape[0]
    ones = jnp.ones((8, tk), BF16)
    s = [jnp.dot(k, q, preferred_element_type=F32) for k, q in zip(k_tile, q_all)]
    s = [x if b is None else x + jnp.concatenate([b] * NSA_GROUP, axis=1) for x, b in zip(s, bias)]
    m_new = [jnp.maximum(m, jnp.max(x, axis=0, keepdims=True)) for x, (m, _) in zip(s, carry)]
    p = [jnp.exp2(x - m).astype(BF16) for x, m in zip(s, m_new)]
    pv = [jnp.dot(jnp.concatenate([v.astype(BF16), ones], axis=0), x, preferred_element_type=F32)
          for v, x in zip(v_t, p)]
    return tuple((mn, jnp.exp2(m - mn) * acc + x) for mn, (m, acc), x in zip(m_new, carry, pv))


def _attend_t_init(groups, width):
    return tuple((jnp.full((1, width), NEG, F32), jnp.zeros((HEAD_DIM + 8, width), F32)) for _ in range(groups))


def _attend_t_finish(carry):
    return [acc[:HEAD_DIM] / jnp.maximum(acc[HEAD_DIM:HEAD_DIM + 1], 1e-30) for _, acc in carry]


def _nsa_prompt_body(q_ref, gates_ref, gb_ref, ck_ref, cv_ref, ks_ref, vs_ref, kw_ref, vw_ref, o_ref,
                     ot_ref, *, tq, tk, n_blocks):
    R, N, G = NSA_GROUP, HEAD_DIM, NSA_KV_HEADS
    W = R * tq
    i = pl.program_id(1)
    t0 = i * tq
    qpos = t0 + lax.broadcasted_iota(jnp.int32, (1, tq), 1)
    q_t = q_ref[...].T * (N ** -0.5)
    gt_t = _sigmoid((gates_ref[...] + gb_ref[...]).T)
    ck = ck_ref[...]
    nc = ck.shape[0]
    zeros = jnp.zeros((N, W), F32)
    q_f = []
    for g in range(G):
        hs = jnp.concatenate([q_t[(g * R + r) * N:(g * R + r + 1) * N] for r in range(R)], axis=1)
        q_f.append(jnp.concatenate([hs, zeros] if g == 0 else [zeros, hs], axis=0))
    q_b = [(x * LOG2_E).astype(BF16) for x in q_f]

    blk_w = lax.broadcasted_iota(jnp.int32, (nc, W), 0)
    vis = (blk_w + 1) * NSA_BLOCK - 1 <= jnp.concatenate([qpos] * R, axis=1)
    o_c, imp = [], []
    for g in range(G):
        s = jnp.where(vis, _dot_hi(ck, q_f[g]), -jnp.inf)
        m = jnp.max(s, axis=0, keepdims=True)
        e = jnp.exp(s - jnp.where(m == -jnp.inf, 0.0, m))
        p = e / jnp.maximum(jnp.sum(e, axis=0, keepdims=True), 1e-30)
        o_c.append(_dot(cv_ref[g * N:(g + 1) * N, :], p))
        pg = p[:, 0:tq]
        for r in range(1, R):
            pg = pg + p[:, r * tq:(r + 1) * tq]
        imp.append(pg)
    imp = jnp.concatenate(imp, axis=1)

    blk_r = lax.broadcasted_iota(jnp.int32, (nc, G * tq), 0)
    cur = jnp.right_shift(jnp.concatenate([qpos] * G, axis=1), BLOCK_SHIFT)
    forced = (blk_r == 0) | (blk_r == cur) | (blk_r == cur - 1)
    score = jnp.where(forced, BIG, jnp.where(blk_r <= cur, imp, -BIG))
    rowf = blk_r.astype(F32)
    sel_bias = jnp.full((nc, G * tq), NEG, F32)
    for _ in range(min(NSA_TOPN, n_blocks)):
        m = jnp.max(score, axis=0, keepdims=True)
        first = jnp.min(jnp.where(score == m, rowf, float(nc)), axis=0, keepdims=True)
        hit = rowf == first
        sel_bias = jnp.where(hit, 0.0, sel_bias)
        score = jnp.where(hit, -jnp.inf, score)
    if nc < N:
        sel_bias = jnp.concatenate([sel_bias, jnp.zeros((N - nc, G * tq), F32)], axis=0)
    q_sel = []
    for g in range(G):
        own = q_f[g][g * N:(g + 1) * N] * LOG2_E
        blocks = jnp.concatenate([sel_bias[:, g * tq:(g + 1) * tq]] * R, axis=1)
        q_sel.append(jnp.concatenate([own, blocks] if g == 0 else [blocks, own], axis=0).astype(BF16))

    kpos_col = lax.broadcasted_iota(jnp.int32, (tk, 1), 0)
    lane = lax.broadcasted_iota(jnp.int32, (tk, G * N), 1)
    row_blk = jnp.right_shift(lax.broadcasted_iota(jnp.int32, (tk, G * N), 0), BLOCK_SHIFT)
    bpt = tk // NSA_BLOCK

    def sel_tile(kt, carry, diagonal):
        k0 = pl.multiple_of(kt * tk, tk)
        keys = ks_ref[pl.ds(k0, tk), :].astype(F32)
        member = jnp.where((lane & (N - 1)) == row_blk + kt * bpt, 1.0, 0.0)
        k_aug = [jnp.where((lane >= N) if g == 0 else (lane < N), member, keys).astype(BF16) for g in range(G)]
        causal = jnp.where(k0 + kpos_col <= qpos, 0.0, NEG) if diagonal else None
        v_t = [vs_ref[g * N:(g + 1) * N, pl.ds(k0, tk)] for g in range(G)]
        return _attend_t(k_aug, v_t, q_sel, [causal] * G, carry)

    last = t0 // tk
    carry = lax.fori_loop(0, last, functools.partial(sel_tile, diagonal=False), _attend_t_init(G, W))
    o_s = _attend_t_finish(sel_tile(last, carry, diagonal=True))

    tw = min(NSA_WINDOW + tq, kw_ref.shape[0])
    w0 = pl.multiple_of(jnp.maximum(t0 + tq - tw, 0), tq)
    d = qpos - (w0 + lax.broadcasted_iota(jnp.int32, (tw, 1), 0))
    bias = jnp.where((d >= 0) & (d < NSA_WINDOW), 0.0, NEG)
    v_t = [vw_ref[g * N:(g + 1) * N, pl.ds(w0, tw)] for g in range(G)]
    o_w = _attend_t_finish(_attend_t(kw_ref[pl.ds(w0, tw), :], v_t, q_b, [bias] * G, _attend_t_init(G, W)))

    for g in range(G):
        for r in range(R):
            c0 = _gate_lane(g, r)
            h = g * R + r
            cols = slice(r * tq, (r + 1) * tq)
            ot_ref[h * N:(h + 1) * N, :] = (gt_t[c0:c0 + 1] * o_c[g][:, cols] + gt_t[c0 + 1:c0 + 2] * o_s[g][:, cols]
                                            + gt_t[c0 + 2:c0 + 3] * o_w[g][:, cols])
    o_ref[...] = ot_ref[...].T


def _nsa_prompt(q, gates, ck, cv_t, ks, kvs_t, kw, kvw_t, W):
    B, T, _ = q.shape
    tq = min(256, T)
    tk = min(512, T)
    assert tk % tq == 0 and T % tk == 0
    assert NSA_KV_HEADS == 2 and ck.shape[1] <= HEAD_DIM
    per_b = lambda a: pl.BlockSpec((None,) + a.shape[1:], lambda b, i: (b,) + (0,) * (a.ndim - 1))
    tile = lambda w: pl.BlockSpec((None, tq, w), lambda b, i: (b, i, 0))
    v_rows = pl.BlockSpec((None, NSA_KV, T), lambda b, i: (b, 1, 0))
    gb = _gate_bias_row(W['nsa_gate_b'])
    return pl.pallas_call(
        functools.partial(_nsa_prompt_body, tq=tq, tk=tk, n_blocks=T // NSA_BLOCK),
        grid=(B, T // tq),
        in_specs=[tile(NSA_WIDTH), tile(LANES), pl.BlockSpec((1, LANES), lambda b, i: (0, 0)),
                  per_b(ck), per_b(cv_t), per_b(ks), v_rows, per_b(kw), v_rows],
        out_specs=tile(NSA_WIDTH),
        out_shape=jax.ShapeDtypeStruct((B, T, NSA_WIDTH), F32),
        scratch_shapes=[pltpu.VMEM((NSA_WIDTH, tq), F32)],
        compiler_params=pltpu.CompilerParams(dimension_semantics=("parallel", "arbitrary"),
                                             vmem_limit_bytes=VMEM_LIMIT),
        name="nsa_prompt",
    )(q, gates, gb, ck, cv_t, ks, kvs_t, kw, kvw_t)


def _stack_heads(qg):
    return jnp.concatenate([qg[:, r * HEAD_DIM:(r + 1) * HEAD_DIM] for r in range(NSA_GROUP)],
                           axis=0) * (HEAD_DIM ** -0.5)


def _tile_rows(x, n):
    return jnp.concatenate([x] * n, axis=0)


def _masked_softmax(s, mask):
    s = jnp.where(mask, s, -jnp.inf)
    m = jnp.max(s, axis=-1, keepdims=True)
    m = jnp.where(m == -jnp.inf, 0.0, m)
    e = jnp.exp(s - m)
    return e / jnp.maximum(jnp.sum(e, axis=-1, keepdims=True), 1e-30)


def _select_blocks(imp, qpos, n_blocks, k):
    t, w = imp.shape
    jf = lax.broadcasted_iota(jnp.int32, (t, w), 1)
    cur = jnp.right_shift(qpos, BLOCK_SHIFT)
    forced = (jf == 0) | (jf == cur) | (jf == cur - 1)
    score = jnp.where(forced, BIG, jnp.where(jf <= cur, imp, -BIG))
    score = jnp.where(jf < n_blocks, score, -jnp.inf)
    lane = jf.astype(F32)
    idx = []
    for _ in range(k):
        m = jnp.max(score, axis=-1, keepdims=True)
        i = jnp.min(jnp.where(score == m, lane, float(w)), axis=-1, keepdims=True)
        score = jnp.where(lane == i, -jnp.inf, score)
        idx.append(i)
    return idx


def _online_step(carry, s, ok, pv):
    m, l, acc = carry
    s = jnp.where(ok, s, -jnp.inf)
    m_new = jnp.maximum(m, jnp.max(s, axis=-1, keepdims=True))
    m_safe = jnp.where(m_new == -jnp.inf, 0.0, m_new)
    p = jnp.exp(s - m_safe)
    alpha = jnp.exp(m - m_safe)
    return m_new, alpha * l + jnp.sum(p, axis=-1, keepdims=True), alpha * acc + pv(p)


def _online_init(rows):
    return (jnp.full((rows, 1), -jnp.inf, F32), jnp.zeros((rows, 1), F32), jnp.zeros((rows, HEAD_DIM), F32))


def _online_finish(carry):
    _, l, acc = carry
    return acc / jnp.maximum(l, 1e-30)


def _nsa_sample_cmp_body(q_ref, ck_ref, cv_ref, kvw_ref, winp_ref, oc_ref, ow_ref, idx_ref, *, pos0, n_blocks):
    R, N = NSA_GROUP, HEAD_DIM
    t = q_ref.shape[0]
    wb = winp_ref.shape[1]
    qpos = pos0 + lax.broadcasted_iota(jnp.int32, (t, 1), 0)
    qpos_s = _tile_rows(qpos, R)
    q = q_ref[...]
    ck = ck_ref[...]
    nc = ck.shape[0]
    w_sel = -(-n_blocks // LANES) * LANES
    new = kvw_ref[...]
    oc_heads, ow_heads = [], []
    for g in range(NSA_KV_HEADS):
        krows = slice(g * N, (g + 1) * N)
        vrows = slice(NSA_KV + g * N, NSA_KV + (g + 1) * N)
        qs = _stack_heads(q[:, g * R * N:(g + 1) * R * N])
        blk = lax.broadcasted_iota(jnp.int32, (1, nc), 1)
        p = _masked_softmax(_dot_nt_hi(qs, ck[:, krows]), (blk + 1) * NSA_BLOCK - 1 <= qpos_s)
        o_c = _dot_nt(p, cv_ref[krows, :])
        imp = p[0:t]
        for r in range(1, R):
            imp = imp + p[r * t:(r + 1) * t]
        if w_sel > nc:
            imp = jnp.concatenate([imp, jnp.zeros((t, w_sel - nc), F32)], axis=1)
        idx = _select_blocks(imp, qpos, n_blocks, min(NSA_TOPN, n_blocks))
        lane = lax.broadcasted_iota(jnp.int32, (t, LANES), 1)
        tile = jnp.zeros((t, LANES), F32)
        for kk, col in enumerate(idx):
            tile = jnp.where(lane == kk, col, tile)
        idx_ref[g] = tile.astype(jnp.int32)

        carry = _online_init(R * t)
        d = qpos_s - (pos0 - wb + lax.broadcasted_iota(jnp.int32, (1, wb), 1))
        carry = _online_step(carry, _dot(qs, winp_ref[krows, :]), (d >= 0) & (d < NSA_WINDOW),
                             lambda p: _dot_nt(p, winp_ref[vrows, :]))
        d = qpos_s - (pos0 + lax.broadcasted_iota(jnp.int32, (1, t), 1))
        carry = _online_step(carry, _dot_nt(qs, new[:, krows]), (d >= 0) & (d < NSA_WINDOW),
                             lambda p: _dot(p, new[:, vrows]))
        o_w = _online_finish(carry)
        oc_heads += [o_c[r * t:(r + 1) * t] for r in range(R)]
        ow_heads += [o_w[r * t:(r + 1) * t] for r in range(R)]
    oc_ref[...] = jnp.concatenate(oc_heads, axis=1)
    ow_ref[...] = jnp.concatenate(ow_heads, axis=1)


def _nsa_sample_cmp(q, ck, cv_t, kv_w, win_prev_t, pos0, n_blocks):
    B, T, _ = q.shape
    per_b = lambda a: pl.BlockSpec((None,) + a.shape[1:], lambda b: (b,) + (0,) * (a.ndim - 1))
    out_b = lambda *s: pl.BlockSpec((None,) + s, lambda b: (b,) + (0,) * len(s))
    return pl.pallas_call(
        functools.partial(_nsa_sample_cmp_body, pos0=pos0, n_blocks=n_blocks),
        grid=(B,),
        in_specs=[per_b(q), per_b(ck), per_b(cv_t), per_b(kv_w), per_b(win_prev_t)],
        out_specs=[out_b(T, NSA_WIDTH), out_b(T, NSA_WIDTH), out_b(NSA_KV_HEADS, T, LANES)],
        out_shape=[jax.ShapeDtypeStruct((B, T, NSA_WIDTH), F32), jax.ShapeDtypeStruct((B, T, NSA_WIDTH), F32),
                   jax.ShapeDtypeStruct((B, NSA_KV_HEADS, T, LANES), jnp.int32)],
        compiler_params=pltpu.CompilerParams(dimension_semantics=("parallel",)),
        name="nsa_sample_cmp",
    )(q, ck, cv_t, kv_w, win_prev_t)


def _nsa_sample_sel_body(idx_ref, pt_ref, *refs, n_past, pos0, topk):
    R, N, G = NSA_GROUP, HEAD_DIM, NSA_KV_HEADS
    page_refs = refs[:G * topk]
    q_ref, new_ref, oc_ref, ow_ref, gates_ref, gb_ref, o_ref = refs[G * topk:]
    b = pl.program_id(0)
    t = pl.program_id(1)
    n_t = pl.num_programs(1)
    qpos = pos0 + t
    gt = _sigmoid(gates_ref[pl.ds(t, 1), :] + gb_ref[...])
    q = q_ref[pl.ds(t, 1), :]
    o_c = oc_ref[pl.ds(t, 1), :]
    o_w = ow_ref[pl.ds(t, 1), :]
    lane = lax.broadcasted_iota(jnp.int32, (1, PAGE_SIZE), 1)
    heads = []
    for g in range(G):
        qs = _stack_heads(q[:, g * R * N:(g + 1) * R * N])
        new_k = new_ref[g * N:(g + 1) * N, :]
        new_v = new_ref[NSA_KV + g * N:NSA_KV + (g + 1) * N, :]
        scores, vals = [], []
        for k in range(topk):
            j = idx_ref[((b * n_t + t) * G + g) * topk + k]
            half = j % (PAGE_SIZE // NSA_BLOCK)
            is_new = j >= n_past
            page = page_refs[g * topk + k]
            k_t = jnp.where(is_new, new_k, page[g * N:(g + 1) * N, :])
            v_t = jnp.where(is_new, new_v, page[NSA_KV + g * N:NSA_KV + (g + 1) * N, :])
            kpos = (j - half) * NSA_BLOCK + lane
            ok = (jnp.right_shift(lane, BLOCK_SHIFT) == half) & (kpos <= qpos)
            scores.append(jnp.where(ok, _dot(qs, k_t), -jnp.inf))
            vals.append(v_t)
        m = scores[0].max(axis=-1, keepdims=True)
        for s in scores[1:]:
            m = jnp.maximum(m, s.max(axis=-1, keepdims=True))
        m = jnp.where(m == -jnp.inf, 0.0, m)
        l = jnp.zeros((R, 1), F32)
        acc = jnp.zeros((R, N), F32)
        for s, v_t in zip(scores, vals):
            e = jnp.exp(s - m)
            l = l + jnp.sum(e, axis=-1, keepdims=True)
            acc = acc + _dot_nt(e, v_t)
        o_s = acc / jnp.maximum(l, 1e-30)
        for r in range(R):
            c0 = (g * R + r) * N
            gl = _gate_lane(g, r)
            heads.append(gt[:, gl:gl + 1] * o_c[:, c0:c0 + N] + gt[:, gl + 1:gl + 2] * o_s[r:r + 1]
                         + gt[:, gl + 2:gl + 3] * o_w[:, c0:c0 + N])
    o_ref[pl.ds(t, 1), :] = jnp.concatenate(heads, axis=1)


def _nsa_sample_sel(q, idx, page_table, pool_t, kv_s_new, o_c, o_w, gates, W, pos0):
    B, T, _ = q.shape
    G, N = NSA_KV_HEADS, HEAD_DIM
    topk = idx.shape[-1]
    n_pages = page_table.shape[1]
    bpp = PAGE_SIZE // NSA_BLOCK
    n_past = pos0 // NSA_BLOCK
    assert n_past % bpp == 0 and T <= NSA_BLOCK
    new_t = jnp.swapaxes(jnp.pad(kv_s_new, ((0, 0), (0, PAGE_SIZE - T), (0, 0))), 1, 2)

    def page_spec(g, k):
        def index(b, t, idx_ref, pt_ref):
            j = jnp.minimum(idx_ref[((b * T + t) * G + g) * topk + k], n_past - 1)
            return (pt_ref[b * n_pages + j // bpp], 0, 0)
        return pl.BlockSpec((None,) + pool_t.shape[1:], index)

    per_b = lambda a: pl.BlockSpec((None,) + a.shape[1:], lambda b, t, i, p: (b,) + (0,) * (a.ndim - 1))
    gb = _gate_bias_row(W['nsa_gate_b'])
    kv_specs = [page_spec(g, k) for g in range(G) for k in range(topk)]
    return pl.pallas_call(
        functools.partial(_nsa_sample_sel_body, n_past=n_past, pos0=pos0, topk=topk),
        grid_spec=pltpu.PrefetchScalarGridSpec(
            num_scalar_prefetch=2,
            grid=(B, T),
            in_specs=kv_specs + [per_b(q), per_b(new_t), per_b(o_c), per_b(o_w), per_b(gates),
                                 pl.BlockSpec((1, LANES), lambda b, t, i, p: (0, 0))],
            out_specs=pl.BlockSpec((None, T, NSA_WIDTH), lambda b, t, i, p: (b, 0, 0)),
        ),
        out_shape=jax.ShapeDtypeStruct((B, T, NSA_WIDTH), F32),
        compiler_params=pltpu.CompilerParams(dimension_semantics=("parallel", "arbitrary")),
        name="nsa_sample_sel",
    )(idx.reshape(-1), page_table.reshape(-1), *([pool_t] * len(kv_specs)), q, new_t, o_c, o_w, gates, gb)


def _nsa_sample_mixer(q, gates, kv_s, kv_w, page_table, pool_cmp_t, pool_sel_t, win_prev_t, W):
    B, T, _ = q.shape
    pos0 = page_table.shape[1] * PAGE_SIZE
    n_blocks = -(-(pos0 + T) // NSA_BLOCK)
    ck, cv_t = _split_cmp(_nsa_compress_paged(pool_cmp_t, page_table, W))
    o_c, o_w, idx = _nsa_sample_cmp(q, ck, cv_t, kv_w, win_prev_t, pos0, n_blocks)
    idx = jnp.swapaxes(idx[..., :min(NSA_TOPN, n_blocks)], 1, 2)
    return _nsa_sample_sel(q, idx, page_table, pool_sel_t, kv_s, o_c, o_w, gates, W, pos0)


def _mem_kv_body(x_ref, g_ref, w_ref, gk_ref, avg_ref, o_ref):
    h = _rms_rows(x_ref[...], g_ref[...]).astype(BF16)
    kv = lax.dot_general(w_ref[...], h, (((1,), (1,)), ((), ())), preferred_element_type=F32)
    k = kv[:MEM_WIDTH]
    o_ref[:MEM_WIDTH, :] = k * lax.rsqrt(_seg_mean_sq_rows(k, avg_ref[...]) + NORM_EPS) * gk_ref[...]
    o_ref[MEM_WIDTH:, :] = kv[MEM_WIDTH:]


def _mem_kv(mem, g, w_kv, k_g):
    B, m, d = mem.shape
    consts = [g.reshape(1, d), w_kv.T.astype(BF16), jnp.tile(k_g, MEM_HEADS).reshape(MEM_WIDTH, 1),
              _seg_avg_matrix(MEM_WIDTH, 1.0 / HEAD_DIM)]
    return pl.pallas_call(
        _mem_kv_body,
        grid=(B,),
        in_specs=[pl.BlockSpec((None, m, d), lambda b: (b, 0, 0))] + [_const_spec(c.shape) for c in consts],
        out_specs=pl.BlockSpec((None, 2 * MEM_WIDTH, m), lambda b: (b, 0, 0)),
        out_shape=jax.ShapeDtypeStruct((B, 2 * MEM_WIDTH, m), F32),
        compiler_params=pltpu.CompilerParams(dimension_semantics=("parallel",)),
        name="mem_kv",
    )(mem, *consts)


def _out_mem_body(x_ref, orw_ref, oml_ref, onsa_ref, kv_ref, w1_ref, w2_ref, w3_ref, g_ref, wq_ref, gq_ref,
                  avg_ref, wo_ref, o_ref):
    N = HEAD_DIM
    x = (x_ref[...] + _dot(orw_ref[...], w1_ref[...]) + _dot(oml_ref[...], w2_ref[...])
         + _dot(onsa_ref[...], w3_ref[...]))
    h = _rms_rows(x, g_ref[...]).astype(BF16)
    q = jnp.dot(h, wq_ref[...], preferred_element_type=F32)
    q = q * lax.rsqrt(_seg_mean_sq(q, avg_ref[...]) + NORM_EPS) * gq_ref[...] * (N ** -0.5)
    heads = []
    for hd in range(MEM_HEADS):
        s = _dot(q[:, hd * N:(hd + 1) * N], kv_ref[hd * N:(hd + 1) * N, :])
        e = jnp.exp(s - jnp.max(s, axis=-1, keepdims=True))
        p = e / jnp.sum(e, axis=-1, keepdims=True)
        heads.append(_dot_nt(p, kv_ref[MEM_WIDTH + hd * N:MEM_WIDTH + (hd + 1) * N, :]))
    o_ref[...] = x + _dot(jnp.concatenate(heads, axis=1), wo_ref[...])


def _out_mem(x, o_rw, o_ml, o_nsa, kv_t, w_out, g, w_q, q_g, w_o):
    B, T, d = x.shape
    tm = _row_tile(T, 512)
    w1 = w_out[:RW_WIDTH].astype(BF16)
    w2 = w_out[RW_WIDTH:RW_WIDTH + ML_WIDTH].astype(BF16)
    w3 = w_out[RW_WIDTH + ML_WIDTH:].astype(BF16)
    consts = [w1, w2, w3, g.reshape(1, d), w_q.astype(BF16), jnp.tile(q_g, MEM_HEADS).reshape(1, MEM_WIDTH),
              _seg_avg_matrix(MEM_WIDTH, 1.0 / HEAD_DIM), w_o.astype(BF16)]
    tile = lambda w: pl.BlockSpec((None, tm, w), lambda b, i: (b, i, 0))
    return pl.pallas_call(
        _out_mem_body,
        grid=(B, T // tm),
        in_specs=[tile(d), tile(RW_WIDTH), tile(ML_WIDTH), tile(NSA_WIDTH),
                  pl.BlockSpec((None,) + kv_t.shape[1:], lambda b, i: (b, 0, 0))]
                 + [_const_spec(c.shape) for c in consts],
        out_specs=tile(d),
        out_shape=jax.ShapeDtypeStruct((B, T, d), F32),
        compiler_params=pltpu.CompilerParams(dimension_semantics=("parallel", "parallel"),
                                             vmem_limit_bytes=VMEM_LIMIT),
        name="out_mem",
    )(x, o_rw, o_ml, o_nsa, kv_t, *consts)


def _layer(x, W, st, page_table, mem):
    B, T, d = x.shape
    is_prompt = st is None
    g = W['norm_g']
    G = NSA_KV_HEADS
    x1 = _ffn(x.reshape(B * T, d), g[0], W['ffa_up'], W['ffa_down']).reshape(B, T, d)
    proj = _in_proj(x1, g[1], W['w_in_perm'], W['nsa_qk_g'], channel_major=is_prompt)
    p_rw, p_ml, gates, q = proj[:4]
    if is_prompt:
        zeros = lambda *s: jnp.zeros(s, F32)
        st = {'rw_shift': zeros(B, 1, RW_COLS), 'rw_S': zeros(B, RW_HEADS, HEAD_DIM, HEAD_DIM),
              'ml_conv': zeros(B, ML_CONV - 1, ML_WIDTH), 'ml_C': zeros(B, ML_HEADS, HEAD_DIM, HEAD_DIM),
              'ml_n': zeros(B, ML_HEADS, HEAD_DIM), 'ml_m': zeros(B, ML_HEADS)}
        mem_kv_t = _mem_kv(mem, g[3], W['mem_w_kv'], W['mem_qk_g'][1])
    else:
        mem_kv_t = _channel_major(st['mem_kv'])
    o_rw, rw_S = _rwkv(p_rw, st['rw_shift'], st['rw_S'], W)
    o_ml, ml_C, ml_n, ml_m = _mlstm(p_ml, gates, st['ml_conv'], st['ml_C'], st['ml_n'], st['ml_m'], W)
    if is_prompt:
        ks, kw, kvc_t, kvs_t, kvw_t = proj[4:]
        ck, cv_t = _split_cmp(_nsa_compress_dense(kvc_t, W))
        o_nsa = _nsa_prompt(q, gates, ck, cv_t, ks, kvs_t, kw, kvw_t, W)
        new_kv = {'nsa_cmp': _token_major(kvc_t, 2, G), 'nsa_sel': _token_major(kvs_t, 2, G),
                  'nsa_win': _token_major(kvw_t[:, :, T - min(NSA_WINDOW, T):], 2, G)}
    else:
        kv_c, kv_s, kv_w = proj[4:]
        win_prev_t = _channel_major(st['nsa_win'])
        wb = win_prev_t.shape[2]
        o_nsa = _nsa_sample_mixer(q, gates, kv_s, kv_w, page_table, st['nsa_cmp_t'], st['nsa_sel_t'], win_prev_t, W)
        win_t = jnp.concatenate([win_prev_t, jnp.swapaxes(kv_w, 1, 2)], axis=2)
        kv5 = lambda z: z.reshape(B, T, 2, G, HEAD_DIM)
        new_kv = {'nsa_cmp': kv5(kv_c), 'nsa_sel': kv5(kv_s),
                  'nsa_win': _token_major(win_t[:, :, wb + T - min(NSA_WINDOW, wb + T):], 2, G)}
    x2 = _out_mem(x1, o_rw, o_ml, o_nsa, mem_kv_t, W['w_out'], g[2], W['mem_w_q'], W['mem_qk_g'][0], W['mem_w_o'])
    x3 = _ffn(x2.reshape(B * T, d), g[4], W['ffb_up'], W['ffb_down']).reshape(B, T, d)
    qk_in = p_ml[:, :, :ML_WIDTH]
    conv_all = jnp.concatenate([st['ml_conv'], qk_in], axis=1) if T < ML_CONV - 1 else qk_in
    new = dict(new_kv)
    new.update({'rw_shift': p_rw[:, T - 1:], 'rw_S': rw_S, 'ml_conv': conv_all[:, conv_all.shape[1] - (ML_CONV - 1):],
                'ml_C': ml_C, 'ml_n': ml_n, 'ml_m': ml_m})
    if is_prompt:
        new['mem_kv'] = _token_major(mem_kv_t, 2, MEM_HEADS)
    return x3, new


def kernel(x_prompt, x_sample, cache_nsa_cmp, cache_nsa_sel, cache_nsa_win, cache_mem_kv, state_rwkv_shift, state_rwkv_S, state_mlstm_conv, state_mlstm_C, state_mlstm_n, state_mlstm_m, page_table, mem_prompt, norm_g, ffa_up, ffa_down, ffb_up, ffb_down, w_in, w_out, rw_mu, rw_w0, rw_w2, rw_a0, rw_a2, rw_g2, rw_kk, rw_ka, rw_rk, rw_gn_g, rw_gn_b, ml_conv_w, ml_conv_b, ml_wq, ml_wk, ml_gate_b, ml_norm_g, ml_skip, nsa_qk_g, nsa_wpos, nsa_wc, nsa_gate_b, mem_w_q, mem_w_kv, mem_qk_g, mem_w_o):
    params = dict(norm_g=norm_g, ffa_up=ffa_up, ffa_down=ffa_down, ffb_up=ffb_up, ffb_down=ffb_down, w_out=w_out,
                  rw_mu=rw_mu, rw_w0=rw_w0, rw_w2=rw_w2, rw_a0=rw_a0, rw_a2=rw_a2, rw_g2=rw_g2, rw_kk=rw_kk,
                  rw_ka=rw_ka, rw_rk=rw_rk, rw_gn_g=rw_gn_g, rw_gn_b=rw_gn_b, ml_conv_w=ml_conv_w,
                  ml_conv_b=ml_conv_b, ml_wq=ml_wq, ml_wk=ml_wk, ml_gate_b=ml_gate_b, ml_norm_g=ml_norm_g,
                  ml_skip=ml_skip, nsa_qk_g=nsa_qk_g, nsa_wpos=nsa_wpos, nsa_wc=nsa_wc, nsa_gate_b=nsa_gate_b,
                  mem_w_q=mem_w_q, mem_w_kv=mem_w_kv, mem_qk_g=mem_qk_g, mem_w_o=mem_w_o)
    depth = norm_g.shape[0]
    y_p, y_s = x_prompt, x_sample
    new_p, new_s = [], []
    n_phys = cache_nsa_cmp.shape[1]
    all_pages = lambda c: _channel_major(c.reshape((depth * n_phys,) + c.shape[2:]))
    pool_cmp_t, pool_sel_t = all_pages(cache_nsa_cmp), all_pages(cache_nsa_sel)
    for l in range(depth):
        W = {name: v[l] for name, v in params.items()}
        W['w_in_perm'] = _permute_w_in(w_in[l])
        st = {'nsa_cmp_t': pool_cmp_t, 'nsa_sel_t': pool_sel_t, 'nsa_win': cache_nsa_win[l],
              'mem_kv': cache_mem_kv[l], 'rw_shift': state_rwkv_shift[l], 'rw_S': state_rwkv_S[l],
              'ml_conv': state_mlstm_conv[l], 'ml_C': state_mlstm_C[l], 'ml_n': state_mlstm_n[l],
              'ml_m': state_mlstm_m[l]}
        y_p, sp = _layer(y_p, W, None, None, mem_prompt)
        y_s, ss = _layer(y_s, W, st, page_table + l * n_phys, None)
        new_p.append(sp)
        new_s.append(ss)
    P = lambda name: jnp.stack([d[name] for d in new_p])
    S = lambda name: jnp.stack([d[name] for d in new_s])
    return (y_p, y_s,
            P('nsa_cmp'), S('nsa_cmp'), P('nsa_sel'), S('nsa_sel'), P('nsa_win'), S('nsa_win'),
            P('mem_kv'),
            P('rw_shift'), S('rw_shift'), P('rw_S'), S('rw_S'),
            P('ml_conv'), S('ml_conv'), P('ml_C'), S('ml_C'), P('ml_n'), S('ml_n'), P('ml_m'), S('ml_m'))
```

```python
import functools

import jax
import jax.numpy as jnp
from jax import lax
from jax.experimental import pallas as pl
from jax.experimental.pallas import tpu as pltpu

F32 = jnp.float32
BF16 = jnp.bfloat16

HEAD_DIM = 64
RW_HEADS = 4
RW_WIDTH = RW_HEADS * HEAD_DIM
RW_W_LORA = 64
RW_A_LORA = 64
RW_G_LORA = 128
RW_COLS = 3 * RW_WIDTH + RW_W_LORA + RW_A_LORA + RW_G_LORA
RW_GN_EPS = 64e-5
ML_HEADS = 4
ML_WIDTH = ML_HEADS * HEAD_DIM
ML_CONV = 4
RW_CHUNKS = (64,)
ML_CHUNKS = (128, 64)
ML_COLS = 3 * ML_WIDTH + 2 * ML_HEADS
NSA_HEADS = 8
NSA_KV_HEADS = 2
NSA_GROUP = NSA_HEADS // NSA_KV_HEADS
NSA_WIDTH = NSA_HEADS * HEAD_DIM
NSA_KV = NSA_KV_HEADS * HEAD_DIM
NSA_BLOCK = 64
BLOCK_SHIFT = 6
NSA_TOPN = 8
NSA_WINDOW = 512
NSA_COLS = NSA_WIDTH + 6 * NSA_KV + 3 * NSA_HEADS
MEM_HEADS = 4
MEM_WIDTH = MEM_HEADS * HEAD_DIM
PAGE_SIZE = 128
NORM_EPS = 1e-6
BIG = 1e9
NEG = -1e30
LOG2_E = 1.4426950408889634
RW_CHAIN_BATCH = 4
ML_CHAIN_BATCH = 4
LANES = 128
VMEM_LIMIT = 56 * 1024 * 1024

P_RW = 0
P_ML = P_RW + RW_COLS
P_Q = P_ML + 3 * ML_WIDTH
P_KVC = P_Q + NSA_WIDTH
P_KVS = P_KVC + 2 * NSA_KV
P_KVW = P_KVS + 2 * NSA_KV
P_GATES = P_KVW + 2 * NSA_KV
P_TOTAL = P_GATES + LANES


def _dot(a, b):
    return jnp.dot(a.astype(BF16), b.astype(BF16), preferred_element_type=F32)


def _dot_nt(a, b):
    return lax.dot_general(a.astype(BF16), b.astype(BF16), (((1,), (1,)), ((), ())),
                           preferred_element_type=F32)


def _dot_tn(a, b):
    return lax.dot_general(a.astype(BF16), b.astype(BF16), (((0,), (0,)), ((), ())),
                           preferred_element_type=F32)


def _split2(x):
    hi = x.astype(BF16)
    lo = (x - hi.astype(F32)).astype(BF16)
    return hi, lo


def _split3(x):
    hi = x.astype(BF16)
    r = x - hi.astype(F32)
    mid = r.astype(BF16)
    lo = (r - mid.astype(F32)).astype(BF16)
    return hi, mid, lo


def _dot_sel(sel, x):
    hi, mid, lo = _split3(x)
    s = sel.astype(BF16)
    d = lambda t: jnp.dot(s, t, preferred_element_type=F32)
    return d(hi) + d(mid) + d(lo)


def _dot_x_sel(x, sel):
    hi, mid, lo = _split3(x)
    s = sel.astype(BF16)
    d = lambda t: jnp.dot(t, s, preferred_element_type=F32)
    return d(hi) + d(mid) + d(lo)


def _dot_hi(a, b):
    ah, al = _split2(a)
    bh, bl = _split2(b)
    d = lambda u, v: jnp.dot(u, v, preferred_element_type=F32)
    return d(ah, bh) + d(al, bh) + d(ah, bl)


def _dot_nt_hi(a, b):
    ah, al = _split2(a)
    bh, bl = _split2(b)
    d = lambda u, v: lax.dot_general(u, v, (((1,), (1,)), ((), ())), preferred_element_type=F32)
    return d(ah, bh) + d(al, bh) + d(ah, bl)


def _seg_mean_sq(x, seg_avg):
    hi, lo = _split2(x * x)
    d = lambda t: jnp.dot(t, seg_avg, preferred_element_type=F32)
    return d(hi) + d(lo)


def _seg_mean_sq_rows(x, seg_avg):
    hi, lo = _split2(x * x)
    d = lambda t: jnp.dot(seg_avg, t, preferred_element_type=F32)
    return d(hi) + d(lo)


def _rms_rows(x, g):
    return x * lax.rsqrt(jnp.mean(x * x, axis=-1, keepdims=True) + NORM_EPS) * g


def _sigmoid(x):
    return 1.0 / (1.0 + jnp.exp(-x))


def _softplus(x):
    return jnp.maximum(x, 0.0) + jnp.log(1.0 + jnp.exp(-jnp.abs(x)))


def _row_tile(n, target):
    t = min(n, target)
    while n % t:
        t //= 2
    return t


def _scan_chunk(t, candidates):
    for c in candidates:
        if t % c == 0:
            return c
    return t


def _const_spec(shape):
    nd = len(shape)
    return pl.BlockSpec(shape, lambda *_: (0,) * nd, pipeline_mode=pl.Buffered(1))


def _seg_avg_matrix(width, scale):
    i = jnp.arange(width) // HEAD_DIM
    return (jnp.where(i[:, None] == i[None, :], scale, 0.0)).astype(BF16)


def _channel_major(x):
    b, t = x.shape[:2]
    return jnp.transpose(x, (0, 2, 3, 4, 1)).reshape(b, -1, t)


def _token_major(x, c, g):
    b, w, t = x.shape
    return jnp.transpose(x.reshape(b, c, g, w // (c * g), t), (0, 4, 1, 2, 3))


def _ffn_body(x_ref, g_ref, wg_ref, wu_ref, wd_ref, o_ref, *, f_chunk):
    x = x_ref[...]
    h = _rms_rows(x, g_ref[...]).astype(BF16)
    d_ff = wg_ref.shape[1]
    acc = jnp.zeros_like(x)
    for c in range(d_ff // f_chunk):
        sl = slice(c * f_chunk, (c + 1) * f_chunk)
        gate = jnp.dot(h, wg_ref[:, sl], preferred_element_type=F32)
        up = jnp.dot(h, wu_ref[:, sl], preferred_element_type=F32)
        act = (gate * _sigmoid(gate) * up).astype(BF16)
        acc = acc + jnp.dot(act, wd_ref[sl, :], preferred_element_type=F32)
    o_ref[...] = x + 0.5 * acc


def _ffn(x, g, w_up, w_down):
    n, d = x.shape
    d_ff = w_down.shape[0]
    tm = _row_tile(n, 512)
    f_chunk = d_ff
    wg = w_up[:, :d_ff].astype(BF16)
    wu = w_up[:, d_ff:].astype(BF16)
    wd = w_down.astype(BF16)
    row = pl.BlockSpec((tm, d), lambda i: (i, 0))
    return pl.pallas_call(
        functools.partial(_ffn_body, f_chunk=f_chunk),
        grid=(n // tm,),
        in_specs=[row, _const_spec((1, d)), _const_spec((d, d_ff)), _const_spec((d, d_ff)),
                  _const_spec((d_ff, d))],
        out_specs=row,
        out_shape=jax.ShapeDtypeStruct((n, d), F32),
        compiler_params=pltpu.CompilerParams(dimension_semantics=("parallel",),
                                             vmem_limit_bytes=VMEM_LIMIT),
        name="ffn",
    )(x, g.reshape(1, d), wg, wu, wd)


def _in_proj_body(x_ref, g_ref, w_ref, gq_ref, gks_ref, gkw_ref, avg_ref, rw_ref, ml_ref, gates_ref, q_ref,
                  *kv_refs, channel_major):
    h = _rms_rows(x_ref[...], g_ref[...]).astype(BF16)
    p = jnp.dot(h, w_ref[...], preferred_element_type=F32)
    rw_ref[...] = p[:, P_RW:P_ML]
    ml_ref[...] = p[:, P_ML:P_Q]
    gates_ref[...] = p[:, P_GATES:P_TOTAL]
    avg = avg_ref[...]

    def head_norm(z, g):
        w = z.shape[1]
        return z * lax.rsqrt(_seg_mean_sq(z, avg[:w, :w]) + NORM_EPS) * g

    q_ref[...] = head_norm(p[:, P_Q:P_KVC], gq_ref[...])
    kvc = p[:, P_KVC:P_KVS]
    ks = head_norm(p[:, P_KVS:P_KVS + NSA_KV], gks_ref[...])
    vs = p[:, P_KVS + NSA_KV:P_KVW]
    kw = head_norm(p[:, P_KVW:P_KVW + NSA_KV], gkw_ref[...])
    vw = p[:, P_KVW + NSA_KV:P_GATES]
    if channel_major:
        ks_ref, kw_ref, kvc_t_ref, kvs_t_ref, kvw_t_ref = kv_refs
        ks_ref[...] = ks.astype(BF16)
        kw_ref[...] = kw.astype(BF16)
        kvc_t_ref[...] = kvc.T
        kvs_t_ref[:NSA_KV, :] = ks.T
        kvs_t_ref[NSA_KV:, :] = vs.T
        kvw_t_ref[:NSA_KV, :] = kw.T
        kvw_t_ref[NSA_KV:, :] = vw.T
    else:
        kvc_ref, kvs_ref, kvw_ref = kv_refs
        kvc_ref[...] = kvc
        kvs_ref[:, :NSA_KV] = ks
        kvs_ref[:, NSA_KV:] = vs
        kvw_ref[:, :NSA_KV] = kw
        kvw_ref[:, NSA_KV:] = vw


def _permute_w_in(w_in):
    d = w_in.shape[0]
    o_ml = RW_COLS
    o_nsa = RW_COLS + ML_COLS
    ml_gates = w_in[:, o_ml + 3 * ML_WIDTH:o_nsa]
    nsa_gates = w_in[:, o_nsa + NSA_WIDTH + 6 * NSA_KV:]
    pad = jnp.zeros((d, LANES - 2 * ML_HEADS - 3 * NSA_HEADS), w_in.dtype)
    return jnp.concatenate([w_in[:, :o_ml + 3 * ML_WIDTH], w_in[:, o_nsa:o_nsa + NSA_WIDTH + 6 * NSA_KV],
                            ml_gates, nsa_gates, pad], axis=1).astype(BF16)


def _in_proj(x, g, w_perm, qk_g, channel_major):
    B, T, d = x.shape
    n = B * T
    tm = _row_tile(T if channel_major else n, 512)
    tpb = T // tm
    row = lambda w: pl.BlockSpec((tm, w), lambda i: (i, 0))
    widths = [RW_COLS, 3 * ML_WIDTH, LANES, NSA_WIDTH]
    out_specs = [row(w) for w in widths]
    out_shape = [jax.ShapeDtypeStruct((n, w), F32) for w in widths]
    w_kv = 2 * NSA_KV
    if channel_major:
        out_specs += [row(NSA_KV)] * 2 + [pl.BlockSpec((None, w_kv, tm), lambda i: (i // tpb, 0, i % tpb))] * 3
        out_shape += [jax.ShapeDtypeStruct((n, NSA_KV), BF16)] * 2 + [jax.ShapeDtypeStruct((B, w_kv, T), F32)] * 3
    else:
        out_specs += [row(w_kv)] * 3
        out_shape += [jax.ShapeDtypeStruct((n, w_kv), F32)] * 3
    gq = jnp.tile(qk_g[0], NSA_HEADS).reshape(1, NSA_WIDTH)
    gks = jnp.tile(qk_g[2], NSA_KV_HEADS).reshape(1, NSA_KV)
    gkw = jnp.tile(qk_g[3], NSA_KV_HEADS).reshape(1, NSA_KV)
    avg = _seg_avg_matrix(NSA_WIDTH, 1.0 / HEAD_DIM)
    outs = pl.pallas_call(
        functools.partial(_in_proj_body, channel_major=channel_major),
        grid=(n // tm,),
        in_specs=[row(d), _const_spec((1, d)), _const_spec((d, P_TOTAL)), _const_spec((1, NSA_WIDTH)),
                  _const_spec((1, NSA_KV)), _const_spec((1, NSA_KV)), _const_spec((NSA_WIDTH, NSA_WIDTH))],
        out_specs=out_specs,
        out_shape=out_shape,
        compiler_params=pltpu.CompilerParams(dimension_semantics=("parallel",),
                                             vmem_limit_bytes=VMEM_LIMIT),
        name="in_proj",
    )(x.reshape(n, d), g.reshape(1, d), w_perm, gq, gks, gkw, avg)
    return [o.reshape(B, T, -1) if o.shape[0] == n else o for o in outs]


def _rwkv_chunk_inputs(p, prev_row, mu_ref, w0_ref, lw_ref, a0_ref, g2_ref, kks_ref, ka_ref, rk_ref, tri_ref,
                       seg_ref):
    C = p.shape[0]
    rows = lax.broadcasted_iota(jnp.int32, (C, 1), 0)
    prev = jnp.where(rows == 0, prev_row, pltpu.roll(p, 1, axis=0))
    xm = p + (prev - p) * mu_ref[...]
    r = xm[:, 0:RW_WIDTH]
    k = xm[:, RW_WIDTH:2 * RW_WIDTH]
    v = xm[:, 2 * RW_WIDTH:3 * RW_WIDTH]
    lin = xm[:, 3 * RW_WIDTH:3 * RW_WIDTH + RW_W_LORA + RW_A_LORA]
    lane = lax.broadcasted_iota(jnp.int32, lin.shape, 1)
    lora = _dot_hi(jnp.where(lane < RW_W_LORA, jnp.tanh(lin), lin), lw_ref[...])
    w = -_softplus(-(w0_ref[...] + lora[:, :RW_WIDTH])) - 0.5
    a = _sigmoid(a0_ref[...] + lora[:, RW_WIDTH:])
    g = _dot(_sigmoid(xm[:, 3 * RW_WIDTH + RW_W_LORA + RW_A_LORA:]), g2_ref[...])
    kk = k * kks_ref[...]
    hi, lo = _split2(kk * kk)
    seg = seg_ref[...]
    ss = jnp.dot(hi, seg, preferred_element_type=F32) + jnp.dot(lo, seg, preferred_element_type=F32)
    kk = kk / jnp.maximum(jnp.sqrt(ss), 1e-12)
    k = k * (1.0 + (a - 1.0) * ka_ref[...])
    logdec = -jnp.exp(w)
    tri = tri_ref[...]
    G = _dot_sel(tri, logdec)
    g_end = G[C - 1:C, :]
    e_g = jnp.exp(G)
    e_gi = jnp.exp(-G)
    kkd = kk * jnp.exp(G - logdec)
    rd = r * e_g
    b = kk * a
    bi = b * e_gi
    ki = k * e_gi
    e_end = jnp.exp(g_end - G)
    bi2 = b * e_end
    ki2 = k * e_end
    dec_end = jnp.exp(g_end)
    rkk = r * k * rk_ref[...]
    return dict(v=v, g=g, kkd=kkd, rd=rd, bi=bi, ki=ki, bi2=bi2, ki2=ki2, dec_end=dec_end, rkk=rkk)


def _rwkv_body(p_ref, shift_ref, s0_ref, mu_ref, w0_ref, lw_ref, a0_ref, g2_ref, kks_ref, ka_ref,
               rk_ref, gng_ref, gnb_ref, tri_ref, seg_ref, o_ref, st_ref, carry_ref, s_ref, *, chunk, nb):
    C = chunk
    N = HEAD_DIM

    @pl.when(pl.program_id(1) == 0)
    def _():
        carry_ref[...] = shift_ref[...]
        s_ref[...] = s0_ref[...]

    pre = []
    for bi in range(nb):
        p = p_ref[bi]
        pre.append(_rwkv_chunk_inputs(p, carry_ref[bi], mu_ref, w0_ref, lw_ref, a0_ref, g2_ref, kks_ref, ka_ref,
                                      rk_ref, tri_ref, seg_ref))
        carry_ref[bi] = p[C - 1:C, :]

    ti = lax.broadcasted_iota(jnp.int32, (C, C), 0)
    si = lax.broadcasted_iota(jnp.int32, (C, C), 1)
    strict = si < ti
    eye = (si == ti).astype(F32)
    s2 = lax.broadcasted_iota(jnp.int32, (C, 2 * C), 1)
    incl2 = jnp.where(s2 < C, s2, s2 - C) <= lax.broadcasted_iota(jnp.int32, (C, 2 * C), 0)

    units = [(bi, h) for bi in range(nb) for h in range(RW_HEADS)]
    col = lambda name, u: pre[u[0]][name][:, u[1] * N:(u[1] + 1) * N]
    lhs = [jnp.concatenate([col('kkd', u), col('rd', u)], axis=0) for u in units]
    rhs = [jnp.concatenate([col('bi', u), col('ki', u)], axis=0) for u in units]
    vh = [col('v', u) for u in units]
    s0 = [s_ref[bi, h] for bi, h in units]
    m4 = [_dot_nt(a, b) for a, b in zip(lhs, rhs)]
    ks = [_dot_nt(a, s) for a, s in zip(lhs, s0)]
    pw = [jnp.where(strict, -m[:C, :C], 0.0) for m in m4]
    lk = [jnp.where(strict, m[:C, C:], 0.0) for m in m4]
    mbk = [jnp.where(incl2, m[C:, :], 0.0) for m in m4]
    lkv = [_dot(a, b) for a, b in zip(lk, vh)]
    t_inv = [eye + n for n in pw]
    span = 2
    while span < C:
        pw = [_dot(x, x) for x in pw]
        t_inv = [t + _dot(t, x) for t, x in zip(t_inv, pw)]
        span *= 2
    u_ = [_dot(t, -(k_[:C] + l)) for t, k_, l in zip(t_inv, ks, lkv)]
    uv = [jnp.concatenate([a, b], axis=0) for a, b in zip(u_, vh)]
    y = [k_[C:] + _dot(m, x) for k_, m, x in zip(ks, mbk, uv)]
    s_new = [s * col('dec_end', u) + _dot_tn(x, jnp.concatenate([col('bi2', u), col('ki2', u)], axis=0))
             for s, x, u in zip(s0, uv, units)]
    for (bi, h), s, yy, v_ in zip(units, s_new, y, vh):
        sl = slice(h * N, (h + 1) * N)
        s_ref[bi, h] = s
        mean = jnp.mean(yy, axis=-1, keepdims=True)
        var = jnp.mean(jnp.square(yy - mean), axis=-1, keepdims=True)
        yn = (yy - mean) * lax.rsqrt(var + RW_GN_EPS) * gng_ref[:, sl] + gnb_ref[:, sl]
        bonus = jnp.sum(pre[bi]['rkk'][:, sl], axis=-1, keepdims=True) * v_
        o_ref[bi, :, sl] = (yn + bonus) * pre[bi]['g'][:, sl]

    @pl.when(pl.program_id(1) == pl.num_programs(1) - 1)
    def _():
        st_ref[...] = s_ref[...]


def _rwkv(p_rw, shift_prev, s0, W):
    B, T, _ = p_rw.shape
    C = _scan_chunk(T, RW_CHUNKS)
    nb = _row_tile(B, RW_CHAIN_BATCH)
    row = lambda v: v.reshape(1, -1).astype(F32)
    z = jnp.zeros((RW_W_LORA, RW_WIDTH), F32)
    lw = jnp.concatenate([jnp.concatenate([W['rw_w2'], z], axis=1),
                          jnp.concatenate([z, W['rw_a2']], axis=1)], axis=0)
    tri = (jnp.arange(C)[:, None] >= jnp.arange(C)[None, :]).astype(BF16)
    seg = _seg_avg_matrix(RW_WIDTH, 1.0)
    consts = [row(W['rw_mu']), row(W['rw_w0']), lw, row(W['rw_a0']), W['rw_g2'].astype(BF16), row(W['rw_kk']),
              row(W['rw_ka']), row(W['rw_rk']), row(W['rw_gn_g']), row(W['rw_gn_b']), tri, seg]
    state = pl.BlockSpec((nb, RW_HEADS, HEAD_DIM, HEAD_DIM), lambda b, c: (b, 0, 0, 0))
    out, s_t = pl.pallas_call(
        functools.partial(_rwkv_body, chunk=C, nb=nb),
        grid=(B // nb, T // C),
        in_specs=[pl.BlockSpec((nb, C, RW_COLS), lambda b, c: (b, c, 0)),
                  pl.BlockSpec((nb, 1, RW_COLS), lambda b, c: (b, 0, 0)), state]
                 + [_const_spec(c.shape) for c in consts],
        out_specs=[pl.BlockSpec((nb, C, RW_WIDTH), lambda b, c: (b, c, 0)), state],
        out_shape=[jax.ShapeDtypeStruct((B, T, RW_WIDTH), F32),
                   jax.ShapeDtypeStruct((B, RW_HEADS, HEAD_DIM, HEAD_DIM), F32)],
        scratch_shapes=[pltpu.VMEM((nb, 1, RW_COLS), F32), pltpu.VMEM((nb, RW_HEADS, HEAD_DIM, HEAD_DIM), F32)],
        compiler_params=pltpu.CompilerParams(dimension_semantics=("parallel", "arbitrary"),
                                             vmem_limit_bytes=VMEM_LIMIT),
        name="rwkv7",
    )(p_rw, shift_prev, s0, *consts)
    return out, s_t


def _mlstm_body(p_ref, gc_ref, gt_ref, convp_ref, c0_ref, n0_ref, m0_ref, cw_ref, cb_ref, wq_ref, wk_ref,
                bias_r_ref, bias_c_ref, ng_ref, skip_ref, tril_ref, triu_ref,
                o_ref, ct_ref, nt_ref, mt_ref, ext_ref, c_ref, n_ref, m_ref, *, chunk, nb):
    L = chunk
    N = HEAD_DIM
    H = ML_HEADS

    @pl.when(pl.program_id(1) == 0)
    def _():
        ext_ref[:, 0:8, :] = convp_ref[...]
        c_ref[...] = c0_ref[...]
        n_ref[...] = n0_ref[...]
        m_ref[...] = m0_ref[...]

    cw = cw_ref[...]
    pre = []
    for bi in range(nb):
        p = p_ref[bi]
        x = p[:, 0:ML_WIDTH]
        ext_ref[bi, 8:8 + L, :] = x
        conv = (cb_ref[...] + cw[3:4, :] * x + cw[2:3, :] * ext_ref[bi, 7:7 + L, :]
                + cw[1:2, :] * ext_ref[bi, 6:6 + L, :] + cw[0:1, :] * ext_ref[bi, 5:5 + L, :])
        tail = ext_ref[bi, L:L + 8, :]
        ext_ref[bi, 0:8, :] = tail
        ca = conv * _sigmoid(conv)
        gcb = gc_ref[bi] + bias_r_ref[...]
        gtb = gt_ref[bi, 0] + bias_c_ref[...][:, 0:1]
        pre.append(dict(
            v=p[:, ML_WIDTH:2 * ML_WIDTH], o_pre=p[:, 2 * ML_WIDTH:3 * ML_WIDTH], ca=ca,
            q=_dot(ca, wq_ref[...]), k=_dot(ca, wk_ref[...]) * (N ** -0.5), gcb=gcb, gtb=gtb,
            bcum_c=_dot_sel(tril_ref[...], -_softplus(-gcb)),
            bcum_r=_dot_x_sel(-_softplus(-gtb), triu_ref[...])))

    ti = lax.broadcasted_iota(jnp.int32, (L, L), 0)
    si = lax.broadcasted_iota(jnp.int32, (L, L), 1)
    causal = si <= ti

    units = [(bi, h) for bi in range(nb) for h in range(H)]
    col = lambda name, u: pre[u[0]][name][:, u[1] * N:(u[1] + 1) * N]
    qh = [col('q', u) for u in units]
    kh = [col('k', u) for u in units]
    vh = [col('v', u) for u in units]
    cs = [c_ref[bi, h] for bi, h in units]
    ns = [n_ref[bi, h:h + 1, :] for bi, h in units]
    m_prev = [m_ref[bi, h:h + 1, 0:1] for bi, h in units]
    b_c = [pre[bi]['bcum_c'][:, H + h:H + h + 1] for bi, h in units]
    i_c = [pre[bi]['gcb'][:, h:h + 1] for bi, h in units]
    b_r = [pre[bi]['bcum_r'][H + h:H + h + 1, :] for bi, h in units]
    i_r = [pre[bi]['gtb'][h:h + 1, :] for bi, h in units]
    qk = [_dot_nt(a, b) for a, b in zip(qh, kh)]
    qc = [_dot_nt(a, c) for a, c in zip(qh, cs)]
    dmat = [jnp.where(causal, bc - br + ir, -jnp.inf) for bc, br, ir in zip(b_c, b_r, i_r)]
    inter = [bc + mp for bc, mp in zip(b_c, m_prev)]
    m_t = [jnp.maximum(it, jnp.max(dm, axis=-1, keepdims=True)) for it, dm in zip(inter, dmat)]
    a = [x * jnp.exp(dm - mt) for x, dm, mt in zip(qk, dmat, m_t)]
    av = [_dot(x, v_) for x, v_ in zip(a, vh)]
    b_end = [bc[L - 1:L, :] for bc in b_c]
    m_new = [jnp.maximum(be + mp, jnp.max(be - br + ir, axis=-1, keepdims=True))
             for be, mp, br, ir in zip(b_end, m_prev, b_r, i_r)]
    dec = [jnp.exp(be + mp - mn) for be, mp, mn in zip(b_end, m_prev, m_new)]
    wg = [jnp.exp(be - bc + ic - mn) for be, bc, ic, mn in zip(b_end, b_c, i_c, m_new)]
    c_new = [d * c + _dot_tn(w * v_, k_) for d, c, w, v_, k_ in zip(dec, cs, wg, vh, kh)]
    for idx, (bi, h) in enumerate(units):
        sl = slice(h * N, (h + 1) * N)
        sc = jnp.exp(inter[idx] - m_t[idx])
        num = sc * qc[idx] + av[idx]
        den = (sc * jnp.sum(qh[idx] * ns[idx], axis=-1, keepdims=True)
               + jnp.sum(a[idx], axis=-1, keepdims=True))
        hh = num / jnp.maximum(jnp.abs(den), jnp.exp(-m_t[idx]))
        c_ref[bi, h] = c_new[idx]
        n_ref[bi, h:h + 1, :] = dec[idx] * ns[idx] + jnp.sum(wg[idx] * kh[idx], axis=0, keepdims=True)
        m_ref[bi, h:h + 1, :] = jnp.broadcast_to(m_new[idx], (1, LANES))
        hn = hh * lax.rsqrt(jnp.mean(hh * hh, axis=-1, keepdims=True) + NORM_EPS)
        o_ref[bi, :, sl] = ((hn * ng_ref[:, sl] + skip_ref[:, sl] * pre[bi]['ca'][:, sl])
                            * _sigmoid(pre[bi]['o_pre'][:, sl]))

    @pl.when(pl.program_id(1) == pl.num_programs(1) - 1)
    def _():
        ct_ref[...] = c_ref[...]
        nt_ref[...] = n_ref[...]
        mt_ref[...] = m_ref[...]


def _block_diag(w):
    H, N, _ = w.shape
    eye = jnp.eye(H, dtype=w.dtype)
    return (eye[:, None, :, None] * w[:, :, None, :]).reshape(H * N, H * N)


def _mlstm(p_ml, gates, conv_prev, c0, n0, m0, W):
    B, T, _ = p_ml.shape
    H, N = ML_HEADS, HEAD_DIM
    L = _scan_chunk(T, ML_CHUNKS)
    nc = T // L
    gt = jnp.swapaxes(gates[:, :, :2 * H].reshape(B, nc, L, 2 * H), 2, 3)
    convp = jnp.concatenate([jnp.zeros((B, 8 - (ML_CONV - 1), ML_WIDTH), F32), conv_prev], axis=1)
    m0b = jnp.broadcast_to(jnp.pad(m0, ((0, 0), (0, 8 - H)))[:, :, None], (B, 8, LANES))
    bias = W['ml_gate_b'].reshape(2 * H)
    bias_r = jnp.pad(bias, (0, LANES - 2 * H)).reshape(1, LANES)
    bias_c = jnp.broadcast_to(bias[:, None], (2 * H, LANES))
    tril = (jnp.arange(L)[:, None] >= jnp.arange(L)[None, :]).astype(BF16)
    row = lambda z: z.reshape(1, -1).astype(F32)
    consts = [W['ml_conv_w'], row(W['ml_conv_b']), _block_diag(W['ml_wq']).astype(BF16),
              _block_diag(W['ml_wk']).astype(BF16), bias_r, bias_c, row(W['ml_norm_g']), row(W['ml_skip']),
              tril, tril.T]
    nb = _row_tile(B, ML_CHAIN_BATCH)
    per_b = lambda *shape: pl.BlockSpec((nb,) + shape, lambda b, c: (b,) + (0,) * len(shape))
    out, c_t, n_t, m_t = pl.pallas_call(
        functools.partial(_mlstm_body, chunk=L, nb=nb),
        grid=(B // nb, nc),
        in_specs=[pl.BlockSpec((nb, L, 3 * ML_WIDTH), lambda b, c: (b, c, 0)),
                  pl.BlockSpec((nb, L, LANES), lambda b, c: (b, c, 0)),
                  pl.BlockSpec((nb, 1, 2 * H, L), lambda b, c: (b, c, 0, 0)),
                  per_b(8, ML_WIDTH), per_b(H, N, N), per_b(H, N), per_b(8, LANES)]
                 + [_const_spec(c.shape) for c in consts],
        out_specs=[pl.BlockSpec((nb, L, ML_WIDTH), lambda b, c: (b, c, 0)),
                   per_b(H, N, N), per_b(H, N), per_b(8, LANES)],
        out_shape=[jax.ShapeDtypeStruct((B, T, ML_WIDTH), F32), jax.ShapeDtypeStruct((B, H, N, N), F32),
                   jax.ShapeDtypeStruct((B, H, N), F32), jax.ShapeDtypeStruct((B, 8, LANES), F32)],
        scratch_shapes=[pltpu.VMEM((nb, L + 8, ML_WIDTH), F32), pltpu.VMEM((nb, H, N, N), F32),
                        pltpu.VMEM((nb, H, N), F32), pltpu.VMEM((nb, 8, LANES), F32)],
        compiler_params=pltpu.CompilerParams(dimension_semantics=("parallel", "arbitrary"),
                                             vmem_limit_bytes=VMEM_LIMIT),
        name="mlstm",
    )(p_ml, gates, gt, convp, c0, n0, m0b, *consts)
    return out, c_t, n_t, m_t[:, :H, 0]


def _compress_body(*refs, n_parts, paged):
    if paged:
        refs = refs[1:]
    part_refs = refs[:n_parts]
    wpos_ref, sel_ref, wc_ref, gk_ref, avg_ref, o_ref = refs[n_parts:]
    wpos = wpos_ref[...]
    parts = []
    for r in part_refs:
        x = r[...]
        parts += [x[:, c * LANES:(c + 1) * LANES] * wpos for c in range(x.shape[1] // LANES)]
    c = _dot(jnp.concatenate(parts, axis=1), sel_ref[...])
    c = _dot_hi(wc_ref[...], c)
    ck = c[:NSA_KV]
    o_ref[:NSA_KV, :] = ck * lax.rsqrt(_seg_mean_sq_rows(ck, avg_ref[...]) + NORM_EPS) * gk_ref[...]
    o_ref[NSA_KV:, :] = c[NSA_KV:]


def _compress_consts(W, n_rows):
    w = 2 * NSA_KV
    wpos = W['nsa_wpos'].reshape(NSA_BLOCK, w).T
    wc_t = _block_diag(W['nsa_wc'].reshape(2 * NSA_KV_HEADS, HEAD_DIM, HEAD_DIM)).T
    gk = jnp.tile(W['nsa_qk_g'][1], NSA_KV_HEADS).reshape(NSA_KV, 1)
    sel = (jnp.arange(n_rows)[:, None] // NSA_BLOCK == jnp.arange(n_rows // NSA_BLOCK)[None, :]).astype(BF16)
    return [jnp.concatenate([wpos, wpos], axis=1), sel, wc_t, gk, _seg_avg_matrix(NSA_KV, 1.0 / HEAD_DIM)]


def _nsa_compress_dense(kvc_t, W):
    B, w, T = kvc_t.shape
    consts = _compress_consts(W, T)
    return pl.pallas_call(
        functools.partial(_compress_body, n_parts=1, paged=False),
        grid=(B,),
        in_specs=[pl.BlockSpec((None, w, T), lambda b: (b, 0, 0))] + [_const_spec(c.shape) for c in consts],
        out_specs=pl.BlockSpec((None, w, T // NSA_BLOCK), lambda b: (b, 0, 0)),
        out_shape=jax.ShapeDtypeStruct((B, w, T // NSA_BLOCK), F32),
        compiler_params=pltpu.CompilerParams(dimension_semantics=("parallel",), vmem_limit_bytes=VMEM_LIMIT),
        name="nsa_compress_dense",
    )(kvc_t, *consts)


def _nsa_compress_paged(pool_t, page_table, W):
    B, n_pages = page_table.shape
    w = pool_t.shape[1]
    pages = min(64, n_pages)
    bpp = PAGE_SIZE // NSA_BLOCK
    consts = _compress_consts(W, pages * PAGE_SIZE)

    def page_spec(k):
        return pl.BlockSpec((None, w, PAGE_SIZE), lambda b, i, pt: (pt[b * n_pages + i * pages + k], 0, 0))

    const = lambda c: pl.BlockSpec(c.shape, lambda b, i, pt: (0,) * c.ndim)
    return pl.pallas_call(
        functools.partial(_compress_body, n_parts=pages, paged=True),
        grid_spec=pltpu.PrefetchScalarGridSpec(
            num_scalar_prefetch=1,
            grid=(B, n_pages // pages),
            in_specs=[page_spec(k) for k in range(pages)] + [const(c) for c in consts],
            out_specs=pl.BlockSpec((None, w, pages * bpp), lambda b, i, pt: (b, 0, i)),
        ),
        out_shape=jax.ShapeDtypeStruct((B, w, n_pages * bpp), F32),
        compiler_params=pltpu.CompilerParams(dimension_semantics=("parallel", "parallel"),
                                             vmem_limit_bytes=VMEM_LIMIT),
        name="nsa_compress_paged",
    )(page_table.reshape(-1), *([pool_t] * pages), *consts)


def _split_cmp(cmp_t):
    return jnp.swapaxes(cmp_t[:, :NSA_KV], 1, 2), cmp_t[:, NSA_KV:]


def _gate_bias_row(gate_b):
    return jnp.pad(gate_b, (2 * ML_HEADS, LANES - 2 * ML_HEADS - 3 * NSA_HEADS)).reshape(1, LANES)


def _gate_lane(g, r):
    return 2 * ML_HEADS + (g * NSA_GROUP + r) * 3


def _attend_t(k_tile, v_t, q_all, bias, carry):
    if not isinstance(k_tile, (list, tuple)):
        k_tile = [k_tile] * len(q_all)
    tk = k_tile[0].shape[0]
    ones = jnp.ones((8, tk), BF16)
    s = [jnp.dot(k, q, preferred_element_type=F32) for k, q in zip(k_tile, q_all)]
    s = [x if b is None else x + jnp.concatenate([b] * NSA_GROUP, axis=1) for x, b in zip(s, bias)]
    m_new = [jnp.maximum(m, jnp.max(x, axis=0, keepdims=True)) for x, (m, _) in zip(s, carry)]
    p = [jnp.exp2(x - m).astype(BF16) for x, m in zip(s, m_new)]
    pv = [jnp.dot(jnp.concatenate([v.astype(BF16), ones], axis=0), x, preferred_element_type=F32)
          for v, x in zip(v_t, p)]
    return tuple((mn, jnp.exp2(m - mn) * acc + x) for mn, (m, acc), x in zip(m_new, carry, pv))


def _attend_t_init(groups, width):
    return tuple((jnp.full((1, width), NEG, F32), jnp.zeros((HEAD_DIM + 8, width), F32)) for _ in range(groups))


def _attend_t_finish(carry):
    return [acc[:HEAD_DIM] / jnp.maximum(acc[HEAD_DIM:HEAD_DIM + 1], 1e-30) for _, acc in carry]


def _nsa_prompt_body(q_ref, gates_ref, gb_ref, ck_ref, cv_ref, ks_ref, vs_ref, kw_ref, vw_ref, o_ref,
                     ot_ref, *, tq, tk, n_blocks):
    R, N, G = NSA_GROUP, HEAD_DIM, NSA_KV_HEADS
    W = R * tq
    i = pl.program_id(1)
    t0 = i * tq
    qpos = t0 + lax.broadcasted_iota(jnp.int32, (1, tq), 1)
    q_t = q_ref[...].T * (N ** -0.5)
    gt_t = _sigmoid((gates_ref[...] + gb_ref[...]).T)
    ck = ck_ref[...]
    nc = ck.shape[0]
    zeros = jnp.zeros((N, W), F32)
    q_f = []
    for g in range(G):
        hs = jnp.concatenate([q_t[(g * R + r) * N:(g * R + r + 1) * N] for r in range(R)], axis=1)
        q_f.append(jnp.concatenate([hs, zeros] if g == 0 else [zeros, hs], axis=0))
    q_b = [(x * LOG2_E).astype(BF16) for x in q_f]

    blk_w = lax.broadcasted_iota(jnp.int32, (nc, W), 0)
    vis = (blk_w + 1) * NSA_BLOCK - 1 <= jnp.concatenate([qpos] * R, axis=1)
    o_c, imp = [], []
    for g in range(G):
        s = jnp.where(vis, _dot_hi(ck, q_f[g]), -jnp.inf)
        m = jnp.max(s, axis=0, keepdims=True)
        e = jnp.exp(s - jnp.where(m == -jnp.inf, 0.0, m))
        p = e / jnp.maximum(jnp.sum(e, axis=0, keepdims=True), 1e-30)
        o_c.append(_dot(cv_ref[g * N:(g + 1) * N, :], p))
        pg = p[:, 0:tq]
        for r in range(1, R):
            pg = pg + p[:, r * tq:(r + 1) * tq]
        imp.append(pg)
    imp = jnp.concatenate(imp, axis=1)

    blk_r = lax.broadcasted_iota(jnp.int32, (nc, G * tq), 0)
    cur = jnp.right_shift(jnp.concatenate([qpos] * G, axis=1), BLOCK_SHIFT)
    forced = (blk_r == 0) | (blk_r == cur) | (blk_r == cur - 1)
    score = jnp.where(forced, BIG, jnp.where(blk_r <= cur, imp, -BIG))
    rowf = blk_r.astype(F32)
    sel_bias = jnp.full((nc, G * tq), NEG, F32)
    for _ in range(min(NSA_TOPN, n_blocks)):
        m = jnp.max(score, axis=0, keepdims=True)
        first = jnp.min(jnp.where(score == m, rowf, float(nc)), axis=0, keepdims=True)
        hit = rowf == first
        sel_bias = jnp.where(hit, 0.0, sel_bias)
        score = jnp.where(hit, -jnp.inf, score)
    if nc < N:
        sel_bias = jnp.concatenate([sel_bias, jnp.zeros((N - nc, G * tq), F32)], axis=0)
    q_sel = []
    for g in range(G):
        own = q_f[g][g * N:(g + 1) * N] * LOG2_E
        blocks = jnp.concatenate([sel_bias[:, g * tq:(g + 1) * tq]] * R, axis=1)
        q_sel.append(jnp.concatenate([own, blocks] if g == 0 else [blocks, own], axis=0).astype(BF16))

    kpos_col = lax.broadcasted_iota(jnp.int32, (tk, 1), 0)
    lane = lax.broadcasted_iota(jnp.int32, (tk, G * N), 1)
    row_blk = jnp.right_shift(lax.broadcasted_iota(jnp.int32, (tk, G * N), 0), BLOCK_SHIFT)
    bpt = tk // NSA_BLOCK

    def sel_tile(kt, carry, diagonal):
        k0 = pl.multiple_of(kt * tk, tk)
        keys = ks_ref[pl.ds(k0, tk), :].astype(F32)
        member = jnp.where((lane & (N - 1)) == row_blk + kt * bpt, 1.0, 0.0)
        k_aug = [jnp.where((lane >= N) if g == 0 else (lane < N), member, keys).astype(BF16) for g in range(G)]
        causal = jnp.where(k0 + kpos_col <= qpos, 0.0, NEG) if diagonal else None
        v_t = [vs_ref[g * N:(g + 1) * N, pl.ds(k0, tk)] for g in range(G)]
        return _attend_t(k_aug, v_t, q_sel, [causal] * G, carry)

    last = t0 // tk
    carry = lax.fori_loop(0, last, functools.partial(sel_tile, diagonal=False), _attend_t_init(G, W))
    o_s = _attend_t_finish(sel_tile(last, carry, diagonal=True))

    tw = min(NSA_WINDOW + tq, kw_ref.shape[0])
    w0 = pl.multiple_of(jnp.maximum(t0 + tq - tw, 0), tq)
    d = qpos - (w0 + lax.broadcasted_iota(jnp.int32, (tw, 1), 0))
    bias = jnp.where((d >= 0) & (d < NSA_WINDOW), 0.0, NEG)
    v_t = [vw_ref[g * N:(g + 1) * N, pl.ds(w0, tw)] for g in range(G)]
    o_w = _attend_t_finish(_attend_t(kw_ref[pl.ds(w0, tw), :], v_t, q_b, [bias] * G, _attend_t_init(G, W)))

    for g in range(G):
        for r in range(R):
            c0 = _gate_lane(g, r)
            h = g * R + r
            cols = slice(r * tq, (r + 1) * tq)
            ot_ref[h * N:(h + 1) * N, :] = (gt_t[c0:c0 + 1] * o_c[g][:, cols] + gt_t[c0 + 1:c0 + 2] * o_s[g][:, cols]
                                            + gt_t[c0 + 2:c0 + 3] * o_w[g][:, cols])
    o_ref[...] = ot_ref[...].T


def _nsa_prompt(q, gates, ck, cv_t, ks, kvs_t, kw, kvw_t, W):
    B, T, _ = q.shape
    tq = min(256, T)
    tk = min(512, T)
    assert tk % tq == 0 and T % tk == 0
    assert NSA_KV_HEADS == 2 and ck.shape[1] <= HEAD_DIM
    per_b = lambda a: pl.BlockSpec((None,) + a.shape[1:], lambda b, i: (b,) + (0,) * (a.ndim - 1))
    tile = lambda w: pl.BlockSpec((None, tq, w), lambda b, i: (b, i, 0))
    v_rows = pl.BlockSpec((None, NSA_KV, T), lambda b, i: (b, 1, 0))
    gb = _gate_bias_row(W['nsa_gate_b'])
    return pl.pallas_call(
        functools.partial(_nsa_prompt_body, tq=tq, tk=tk, n_blocks=T // NSA_BLOCK),
        grid=(B, T // tq),
        in_specs=[tile(NSA_WIDTH), tile(LANES), pl.BlockSpec((1, LANES), lambda b, i: (0, 0)),
                  per_b(ck), per_b(cv_t), per_b(ks), v_rows, per_b(kw), v_rows],
        out_specs=tile(NSA_WIDTH),
        out_shape=jax.ShapeDtypeStruct((B, T, NSA_WIDTH), F32),
        scratch_shapes=[pltpu.VMEM((NSA_WIDTH, tq), F32)],
        compiler_params=pltpu.CompilerParams(dimension_semantics=("parallel", "arbitrary"),
                                             vmem_limit_bytes=VMEM_LIMIT),
        name="nsa_prompt",
    )(q, gates, gb, ck, cv_t, ks, kvs_t, kw, kvw_t)


def _stack_heads(qg):
    return jnp.concatenate([qg[:, r * HEAD_DIM:(r + 1) * HEAD_DIM] for r in range(NSA_GROUP)],
                           axis=0) * (HEAD_DIM ** -0.5)


def _tile_rows(x, n):
    return jnp.concatenate([x] * n, axis=0)


def _masked_softmax(s, mask):
    s = jnp.where(mask, s, -jnp.inf)
    m = jnp.max(s, axis=-1, keepdims=True)
    m = jnp.where(m == -jnp.inf, 0.0, m)
    e = jnp.exp(s - m)
    return e / jnp.maximum(jnp.sum(e, axis=-1, keepdims=True), 1e-30)


def _select_blocks(imp, qpos, n_blocks, k):
    t, w = imp.shape
    jf = lax.broadcasted_iota(jnp.int32, (t, w), 1)
    cur = jnp.right_shift(qpos, BLOCK_SHIFT)
    forced = (jf == 0) | (jf == cur) | (jf == cur - 1)
    score = jnp.where(forced, BIG, jnp.where(jf <= cur, imp, -BIG))
    score = jnp.where(jf < n_blocks, score, -jnp.inf)
    lane = jf.astype(F32)
    idx = []
    for _ in range(k):
        m = jnp.max(score, axis=-1, keepdims=True)
        i = jnp.min(jnp.where(score == m, lane, float(w)), axis=-1, keepdims=True)
        score = jnp.where(lane == i, -jnp.inf, score)
        idx.append(i)
    return idx


def _online_step(carry, s, ok, pv):
    m, l, acc = carry
    s = jnp.where(ok, s, -jnp.inf)
    m_new = jnp.maximum(m, jnp.max(s, axis=-1, keepdims=True))
    m_safe = jnp.where(m_new == -jnp.inf, 0.0, m_new)
    p = jnp.exp(s - m_safe)
    alpha = jnp.exp(m - m_safe)
    return m_new, alpha * l + jnp.sum(p, axis=-1, keepdims=True), alpha * acc + pv(p)


def _online_init(rows):
    return (jnp.full((rows, 1), -jnp.inf, F32), jnp.zeros((rows, 1), F32), jnp.zeros((rows, HEAD_DIM), F32))


def _online_finish(carry):
    _, l, acc = carry
    return acc / jnp.maximum(l, 1e-30)


def _nsa_sample_cmp_body(q_ref, ck_ref, cv_ref, kvw_ref, winp_ref, oc_ref, ow_ref, idx_ref, *, pos0, n_blocks):
    R, N = NSA_GROUP, HEAD_DIM
    t = q_ref.shape[0]
    wb = winp_ref.shape[1]
    qpos = pos0 + lax.broadcasted_iota(jnp.int32, (t, 1), 0)
    qpos_s = _tile_rows(qpos, R)
    q = q_ref[...]
    ck = ck_ref[...]
    nc = ck.shape[0]
    w_sel = -(-n_blocks // LANES) * LANES
    new = kvw_ref[...]
    oc_heads, ow_heads = [], []
    for g in range(NSA_KV_HEADS):
        krows = slice(g * N, (g + 1) * N)
        vrows = slice(NSA_KV + g * N, NSA_KV + (g + 1) * N)
        qs = _stack_heads(q[:, g * R * N:(g + 1) * R * N])
        blk = lax.broadcasted_iota(jnp.int32, (1, nc), 1)
        p = _masked_softmax(_dot_nt_hi(qs, ck[:, krows]), (blk + 1) * NSA_BLOCK - 1 <= qpos_s)
        o_c = _dot_nt(p, cv_ref[krows, :])
        imp = p[0:t]
        for r in range(1, R):
            imp = imp + p[r * t:(r + 1) * t]
        if w_sel > nc:
            imp = jnp.concatenate([imp, jnp.zeros((t, w_sel - nc), F32)], axis=1)
        idx = _select_blocks(imp, qpos, n_blocks, min(NSA_TOPN, n_blocks))
        lane = lax.broadcasted_iota(jnp.int32, (t, LANES), 1)
        tile = jnp.zeros((t, LANES), F32)
        for kk, col in enumerate(idx):
            tile = jnp.where(lane == kk, col, tile)
        idx_ref[g] = tile.astype(jnp.int32)

        carry = _online_init(R * t)
        d = qpos_s - (pos0 - wb + lax.broadcasted_iota(jnp.int32, (1, wb), 1))
        carry = _online_step(carry, _dot(qs, winp_ref[krows, :]), (d >= 0) & (d < NSA_WINDOW),
                             lambda p: _dot_nt(p, winp_ref[vrows, :]))
        d = qpos_s - (pos0 + lax.broadcasted_iota(jnp.int32, (1, t), 1))
        carry = _online_step(carry, _dot_nt(qs, new[:, krows]), (d >= 0) & (d < NSA_WINDOW),
                             lambda p: _dot(p, new[:, vrows]))
        o_w = _online_finish(carry)
        oc_heads += [o_c[r * t:(r + 1) * t] for r in range(R)]
        ow_heads += [o_w[r * t:(r + 1) * t] for r in range(R)]
    oc_ref[...] = jnp.concatenate(oc_heads, axis=1)
    ow_ref[...] = jnp.concatenate(ow_heads, axis=1)


def _nsa_sample_cmp(q, ck, cv_t, kv_w, win_prev_t, pos0, n_blocks):
    B, T, _ = q.shape
    per_b = lambda a: pl.BlockSpec((None,) + a.shape[1:], lambda b: (b,) + (0,) * (a.ndim - 1))
    out_b = lambda *s: pl.BlockSpec((None,) + s, lambda b: (b,) + (0,) * len(s))
    return pl.pallas_call(
        functools.partial(_nsa_sample_cmp_body, pos0=pos0, n_blocks=n_blocks),
        grid=(B,),
        in_specs=[per_b(q), per_b(ck), per_b(cv_t), per_b(kv_w), per_b(win_prev_t)],
        out_specs=[out_b(T, NSA_WIDTH), out_b(T, NSA_WIDTH), out_b(NSA_KV_HEADS, T, LANES)],
        out_shape=[jax.ShapeDtypeStruct((B, T, NSA_WIDTH), F32), jax.ShapeDtypeStruct((B, T, NSA_WIDTH), F32),
                   jax.ShapeDtypeStruct((B, NSA_KV_HEADS, T, LANES), jnp.int32)],
        compiler_params=pltpu.CompilerParams(dimension_semantics=("parallel",)),
        name="nsa_sample_cmp",
    )(q, ck, cv_t, kv_w, win_prev_t)


def _nsa_sample_sel_body(idx_ref, pt_ref, *refs, n_past, pos0, topk, tokens):
    R, N, G = NSA_GROUP, HEAD_DIM, NSA_KV_HEADS
    page_refs = refs[:tokens * G * topk]
    q_ref, new_ref, oc_ref, ow_ref, gates_ref, gb_ref, o_ref = refs[tokens * G * topk:]
    b = pl.program_id(0)
    n_t = pl.num_programs(1) * tokens
    lane = lax.broadcasted_iota(jnp.int32, (1, PAGE_SIZE), 1)
    units = [(pl.program_id(1) * tokens + tt, tt, g) for tt in range(tokens) for g in range(G)]
    q_rows = {tt: q_ref[pl.ds(t, 1), :] for t, tt, g in units}
    qs = [_stack_heads(q_rows[tt][:, g * R * N:(g + 1) * R * N]) for t, tt, g in units]
    scores, vals = [], []
    for (t, tt, g), q_g in zip(units, qs):
        new_k = new_ref[g * N:(g + 1) * N, :]
        new_v = new_ref[NSA_KV + g * N:NSA_KV + (g + 1) * N, :]
        sc, vs = [], []
        for k in range(topk):
            j = idx_ref[((b * n_t + t) * G + g) * topk + k]
            half = j % (PAGE_SIZE // NSA_BLOCK)
            is_new = j >= n_past
            page = page_refs[(tt * G + g) * topk + k]
            k_t = jnp.where(is_new, new_k, page[g * N:(g + 1) * N, :])
            kpos = (j - half) * NSA_BLOCK + lane
            ok = (jnp.right_shift(lane, BLOCK_SHIFT) == half) & (kpos <= pos0 + t)
            sc.append(jnp.where(ok, _dot(q_g, k_t), -jnp.inf))
            vs.append(jnp.where(is_new, new_v, page[NSA_KV + g * N:NSA_KV + (g + 1) * N, :]))
        scores.append(sc)
        vals.append(vs)
    o_s = []
    for sc, vs in zip(scores, vals):
        m = sc[0].max(axis=-1, keepdims=True)
        for s in sc[1:]:
            m = jnp.maximum(m, s.max(axis=-1, keepdims=True))
        m = jnp.where(m == -jnp.inf, 0.0, m)
        l = jnp.zeros((R, 1), F32)
        acc = jnp.zeros((R, N), F32)
        for s, v_t in zip(sc, vs):
            e = jnp.exp(s - m)
            l = l + jnp.sum(e, axis=-1, keepdims=True)
            acc = acc + _dot_nt(e, v_t)
        o_s.append(acc / jnp.maximum(l, 1e-30))
    for tt in range(tokens):
        t = pl.program_id(1) * tokens + tt
        gt = _sigmoid(gates_ref[pl.ds(t, 1), :] + gb_ref[...])
        o_c = oc_ref[pl.ds(t, 1), :]
        o_w = ow_ref[pl.ds(t, 1), :]
        heads = []
        for g in range(G):
            for r in range(R):
                c0 = (g * R + r) * N
                gl = _gate_lane(g, r)
                heads.append(gt[:, gl:gl + 1] * o_c[:, c0:c0 + N] + gt[:, gl + 1:gl + 2] * o_s[tt * G + g][r:r + 1]
                             + gt[:, gl + 2:gl + 3] * o_w[:, c0:c0 + N])
        o_ref[pl.ds(t, 1), :] = jnp.concatenate(heads, axis=1)


def _nsa_sample_sel(q, idx, page_table, pool_t, kv_s_new, o_c, o_w, gates, W, pos0):
    B, T, _ = q.shape
    G, N = NSA_KV_HEADS, HEAD_DIM
    topk = idx.shape[-1]
    n_pages = page_table.shape[1]
    bpp = PAGE_SIZE // NSA_BLOCK
    n_past = pos0 // NSA_BLOCK
    assert n_past % bpp == 0 and T <= NSA_BLOCK
    new_t = jnp.swapaxes(jnp.pad(kv_s_new, ((0, 0), (0, PAGE_SIZE - T), (0, 0))), 1, 2)

    tokens = _row_tile(T, 4)

    def page_spec(tt, g, k):
        def index(b, t, idx_ref, pt_ref):
            j = jnp.minimum(idx_ref[((b * T + t * tokens + tt) * G + g) * topk + k], n_past - 1)
            return (pt_ref[b * n_pages + j // bpp], 0, 0)
        return pl.BlockSpec((None,) + pool_t.shape[1:], index)

    per_b = lambda a: pl.BlockSpec((None,) + a.shape[1:], lambda b, t, i, p: (b,) + (0,) * (a.ndim - 1))
    gb = _gate_bias_row(W['nsa_gate_b'])
    kv_specs = [page_spec(tt, g, k) for tt in range(tokens) for g in range(G) for k in range(topk)]
    return pl.pallas_call(
        functools.partial(_nsa_sample_sel_body, n_past=n_past, pos0=pos0, topk=topk, tokens=tokens),
        grid_spec=pltpu.PrefetchScalarGridSpec(
            num_scalar_prefetch=2,
            grid=(B, T // tokens),
            in_specs=kv_specs + [per_b(q), per_b(new_t), per_b(o_c), per_b(o_w), per_b(gates),
                                 pl.BlockSpec((1, LANES), lambda b, t, i, p: (0, 0))],
            out_specs=pl.BlockSpec((None, T, NSA_WIDTH), lambda b, t, i, p: (b, 0, 0)),
        ),
        out_shape=jax.ShapeDtypeStruct((B, T, NSA_WIDTH), F32),
        compiler_params=pltpu.CompilerParams(dimension_semantics=("parallel", "arbitrary")),
        name="nsa_sample_sel",
    )(idx.reshape(-1), page_table.reshape(-1), *([pool_t] * len(kv_specs)), q, new_t, o_c, o_w, gates, gb)


def _nsa_sample_mixer(q, gates, kv_s, kv_w, page_table, pool_cmp_t, pool_sel_t, win_prev_t, W):
    B, T, _ = q.shape
    pos0 = page_table.shape[1] * PAGE_SIZE
    n_blocks = -(-(pos0 + T) // NSA_BLOCK)
    ck, cv_t = _split_cmp(_nsa_compress_paged(pool_cmp_t, page_table, W))
    o_c, o_w, idx = _nsa_sample_cmp(q, ck, cv_t, kv_w, win_prev_t, pos0, n_blocks)
    idx = jnp.swapaxes(idx[..., :min(NSA_TOPN, n_blocks)], 1, 2)
    return _nsa_sample_sel(q, idx, page_table, pool_sel_t, kv_s, o_c, o_w, gates, W, pos0)


def _mem_kv_body(x_ref, g_ref, w_ref, gk_ref, avg_ref, o_ref):
    h = _rms_rows(x_ref[...], g_ref[...]).astype(BF16)
    kv = lax.dot_general(w_ref[...], h, (((1,), (1,)), ((), ())), preferred_element_type=F32)
    k = kv[:MEM_WIDTH]
    o_ref[:MEM_WIDTH, :] = k * lax.rsqrt(_seg_mean_sq_rows(k, avg_ref[...]) + NORM_EPS) * gk_ref[...]
    o_ref[MEM_WIDTH:, :] = kv[MEM_WIDTH:]


def _mem_kv(mem, g, w_kv, k_g):
    B, m, d = mem.shape
    consts = [g.reshape(1, d), w_kv.T.astype(BF16), jnp.tile(k_g, MEM_HEADS).reshape(MEM_WIDTH, 1),
              _seg_avg_matrix(MEM_WIDTH, 1.0 / HEAD_DIM)]
    return pl.pallas_call(
        _mem_kv_body,
        grid=(B,),
        in_specs=[pl.BlockSpec((None, m, d), lambda b: (b, 0, 0))] + [_const_spec(c.shape) for c in consts],
        out_specs=pl.BlockSpec((None, 2 * MEM_WIDTH, m), lambda b: (b, 0, 0)),
        out_shape=jax.ShapeDtypeStruct((B, 2 * MEM_WIDTH, m), F32),
        compiler_params=pltpu.CompilerParams(dimension_semantics=("parallel",)),
        name="mem_kv",
    )(mem, *consts)


def _out_mem_body(x_ref, orw_ref, oml_ref, onsa_ref, kv_ref, w1_ref, w2_ref, w3_ref, g_ref, wq_ref, gq_ref,
                  avg_ref, wo_ref, o_ref):
    N = HEAD_DIM
    x = (x_ref[...] + _dot(orw_ref[...], w1_ref[...]) + _dot(oml_ref[...], w2_ref[...])
         + _dot(onsa_ref[...], w3_ref[...]))
    h = _rms_rows(x, g_ref[...]).astype(BF16)
    q = jnp.dot(h, wq_ref[...], preferred_element_type=F32)
    q = q * lax.rsqrt(_seg_mean_sq(q, avg_ref[...]) + NORM_EPS) * gq_ref[...] * (N ** -0.5)
    heads = []
    for hd in range(MEM_HEADS):
        s = _dot(q[:, hd * N:(hd + 1) * N], kv_ref[hd * N:(hd + 1) * N, :])
        e = jnp.exp(s - jnp.max(s, axis=-1, keepdims=True))
        p = e / jnp.sum(e, axis=-1, keepdims=True)
        heads.append(_dot_nt(p, kv_ref[MEM_WIDTH + hd * N:MEM_WIDTH + (hd + 1) * N, :]))
    o_ref[...] = x + _dot(jnp.concatenate(heads, axis=1), wo_ref[...])


def _out_mem(x, o_rw, o_ml, o_nsa, kv_t, w_out, g, w_q, q_g, w_o):
    B, T, d = x.shape
    tm = _row_tile(T, 512)
    w1 = w_out[:RW_WIDTH].astype(BF16)
    w2 = w_out[RW_WIDTH:RW_WIDTH + ML_WIDTH].astype(BF16)
    w3 = w_out[RW_WIDTH + ML_WIDTH:].astype(BF16)
    consts = [w1, w2, w3, g.reshape(1, d), w_q.astype(BF16), jnp.tile(q_g, MEM_HEADS).reshape(1, MEM_WIDTH),
              _seg_avg_matrix(MEM_WIDTH, 1.0 / HEAD_DIM), w_o.astype(BF16)]
    tile = lambda w: pl.BlockSpec((None, tm, w), lambda b, i: (b, i, 0))
    return pl.pallas_call(
        _out_mem_body,
        grid=(B, T // tm),
        in_specs=[tile(d), tile(RW_WIDTH), tile(ML_WIDTH), tile(NSA_WIDTH),
                  pl.BlockSpec((None,) + kv_t.shape[1:], lambda b, i: (b, 0, 0))]
                 + [_const_spec(c.shape) for c in consts],
        out_specs=tile(d),
        out_shape=jax.ShapeDtypeStruct((B, T, d), F32),
        compiler_params=pltpu.CompilerParams(dimension_semantics=("parallel", "parallel"),
                                             vmem_limit_bytes=VMEM_LIMIT),
        name="out_mem",
    )(x, o_rw, o_ml, o_nsa, kv_t, *consts)


def _layer(x, W, st, page_table, mem):
    B, T, d = x.shape
    is_prompt = st is None
    g = W['norm_g']
    G = NSA_KV_HEADS
    x1 = _ffn(x.reshape(B * T, d), g[0], W['ffa_up'], W['ffa_down']).reshape(B, T, d)
    proj = _in_proj(x1, g[1], W['w_in_perm'], W['nsa_qk_g'], channel_major=is_prompt)
    p_rw, p_ml, gates, q = proj[:4]
    if is_prompt:
        zeros = lambda *s: jnp.zeros(s, F32)
        st = {'rw_shift': zeros(B, 1, RW_COLS), 'rw_S': zeros(B, RW_HEADS, HEAD_DIM, HEAD_DIM),
              'ml_conv': zeros(B, ML_CONV - 1, ML_WIDTH), 'ml_C': zeros(B, ML_HEADS, HEAD_DIM, HEAD_DIM),
              'ml_n': zeros(B, ML_HEADS, HEAD_DIM), 'ml_m': zeros(B, ML_HEADS)}
        mem_kv_t = _mem_kv(mem, g[3], W['mem_w_kv'], W['mem_qk_g'][1])
    else:
        mem_kv_t = _channel_major(st['mem_kv'])
    o_rw, rw_S = _rwkv(p_rw, st['rw_shift'], st['rw_S'], W)
    o_ml, ml_C, ml_n, ml_m = _mlstm(p_ml, gates, st['ml_conv'], st['ml_C'], st['ml_n'], st['ml_m'], W)
    if is_prompt:
        ks, kw, kvc_t, kvs_t, kvw_t = proj[4:]
        ck, cv_t = _split_cmp(_nsa_compress_dense(kvc_t, W))
        o_nsa = _nsa_prompt(q, gates, ck, cv_t, ks, kvs_t, kw, kvw_t, W)
        new_kv = {'nsa_cmp': _token_major(kvc_t, 2, G), 'nsa_sel': _token_major(kvs_t, 2, G),
                  'nsa_win': _token_major(kvw_t[:, :, T - min(NSA_WINDOW, T):], 2, G)}
    else:
        kv_c, kv_s, kv_w = proj[4:]
        win_prev_t = _channel_major(st['nsa_win'])
        wb = win_prev_t.shape[2]
        o_nsa = _nsa_sample_mixer(q, gates, kv_s, kv_w, page_table, st['nsa_cmp_t'], st['nsa_sel_t'], win_prev_t, W)
        win_t = jnp.concatenate([win_prev_t, jnp.swapaxes(kv_w, 1, 2)], axis=2)
        kv5 = lambda z: z.reshape(B, T, 2, G, HEAD_DIM)
        new_kv = {'nsa_cmp': kv5(kv_c), 'nsa_sel': kv5(kv_s),
                  'nsa_win': _token_major(win_t[:, :, wb + T - min(NSA_WINDOW, wb + T):], 2, G)}
    x2 = _out_mem(x1, o_rw, o_ml, o_nsa, mem_kv_t, W['w_out'], g[2], W['mem_w_q'], W['mem_qk_g'][0], W['mem_w_o'])
    x3 = _ffn(x2.reshape(B * T, d), g[4], W['ffb_up'], W['ffb_down']).reshape(B, T, d)
    qk_in = p_ml[:, :, :ML_WIDTH]
    conv_all = jnp.concatenate([st['ml_conv'], qk_in], axis=1) if T < ML_CONV - 1 else qk_in
    new = dict(new_kv)
    new.update({'rw_shift': p_rw[:, T - 1:], 'rw_S': rw_S, 'ml_conv': conv_all[:, conv_all.shape[1] - (ML_CONV - 1):],
                'ml_C': ml_C, 'ml_n': ml_n, 'ml_m': ml_m})
    if is_prompt:
        new['mem_kv'] = _token_major(mem_kv_t, 2, MEM_HEADS)
    return x3, new


def kernel(x_prompt, x_sample, cache_nsa_cmp, cache_nsa_sel, cache_nsa_win, cache_mem_kv, state_rwkv_shift, state_rwkv_S, state_mlstm_conv, state_mlstm_C, state_mlstm_n, state_mlstm_m, page_table, mem_prompt, norm_g, ffa_up, ffa_down, ffb_up, ffb_down, w_in, w_out, rw_mu, rw_w0, rw_w2, rw_a0, rw_a2, rw_g2, rw_kk, rw_ka, rw_rk, rw_gn_g, rw_gn_b, ml_conv_w, ml_conv_b, ml_wq, ml_wk, ml_gate_b, ml_norm_g, ml_skip, nsa_qk_g, nsa_wpos, nsa_wc, nsa_gate_b, mem_w_q, mem_w_kv, mem_qk_g, mem_w_o):
    params = dict(norm_g=norm_g, ffa_up=ffa_up, ffa_down=ffa_down, ffb_up=ffb_up, ffb_down=ffb_down, w_out=w_out,
                  rw_mu=rw_mu, rw_w0=rw_w0, rw_w2=rw_w2, rw_a0=rw_a0, rw_a2=rw_a2, rw_g2=rw_g2, rw_kk=rw_kk,
                  rw_ka=rw_ka, rw_rk=rw_rk, rw_gn_g=rw_gn_g, rw_gn_b=rw_gn_b, ml_conv_w=ml_conv_w,
                  ml_conv_b=ml_conv_b, ml_wq=ml_wq, ml_wk=ml_wk, ml_gate_b=ml_gate_b, ml_norm_g=ml_norm_g,
                  ml_skip=ml_skip, nsa_qk_g=nsa_qk_g, nsa_wpos=nsa_wpos, nsa_wc=nsa_wc, nsa_gate_b=nsa_gate_b,
                  mem_w_q=mem_w_q, mem_w_kv=mem_w_kv, mem_qk_g=mem_qk_g, mem_w_o=mem_w_o)
    depth = norm_g.shape[0]
    y_p, y_s = x_prompt, x_sample
    new_p, new_s = [], []
    n_phys = cache_nsa_cmp.shape[1]
    all_pages = lambda c: _channel_major(c.reshape((depth * n_phys,) + c.shape[2:]))
    pool_cmp_t, pool_sel_t = all_pages(cache_nsa_cmp), all_pages(cache_nsa_sel)
    for l in range(depth):
        W = {name: v[l] for name, v in params.items()}
        W['w_in_perm'] = _permute_w_in(w_in[l])
        st = {'nsa_cmp_t': pool_cmp_t, 'nsa_sel_t': pool_sel_t, 'nsa_win': cache_nsa_win[l],
              'mem_kv': cache_mem_kv[l], 'rw_shift': state_rwkv_shift[l], 'rw_S': state_rwkv_S[l],
              'ml_conv': state_mlstm_conv[l], 'ml_C': state_mlstm_C[l], 'ml_n': state_mlstm_n[l],
              'ml_m': state_mlstm_m[l]}
        y_p, sp = _layer(y_p, W, None, None, mem_prompt)
        y_s, ss = _layer(y_s, W, st, page_table + l * n_phys, None)
        new_p.append(sp)
        new_s.append(ss)
    P = lambda name: jnp.stack([d[name] for d in new_p])
    S = lambda name: jnp.stack([d[name] for d in new_s])
    return (y_p, y_s,
            P('nsa_cmp'), S('nsa_cmp'), P('nsa_sel'), S('nsa_sel'), P('nsa_win'), S('nsa_win'),
            P('mem_kv'),
            P('rw_shift'), S('rw_shift'), P('rw_S'), S('rw_S'),
            P('ml_conv'), S('ml_conv'), P('ml_C'), S('ml_C'), P('ml_n'), S('ml_n'), P('ml_m'), S('ml_m'))
```

```python
import functools

import jax
import jax.numpy as jnp
from jax import lax
from jax.experimental import pallas as pl
from jax.experimental.pallas import tpu as pltpu

F32 = jnp.float32
BF16 = jnp.bfloat16

HEAD_DIM = 64
RW_HEADS = 4
RW_WIDTH = RW_HEADS * HEAD_DIM
RW_W_LORA = 64
RW_A_LORA = 64
RW_G_LORA = 128
RW_COLS = 3 * RW_WIDTH + RW_W_LORA + RW_A_LORA + RW_G_LORA
RW_GN_EPS = 64e-5
ML_HEADS = 4
ML_WIDTH = ML_HEADS * HEAD_DIM
ML_CONV = 4
RW_CHUNKS = (64,)
ML_CHUNKS = (128, 64)
ML_COLS = 3 * ML_WIDTH + 2 * ML_HEADS
NSA_HEADS = 8
NSA_KV_HEADS = 2
NSA_GROUP = NSA_HEADS // NSA_KV_HEADS
NSA_WIDTH = NSA_HEADS * HEAD_DIM
NSA_KV = NSA_KV_HEADS * HEAD_DIM
NSA_BLOCK = 64
BLOCK_SHIFT = 6
NSA_TOPN = 8
NSA_WINDOW = 512
NSA_COLS = NSA_WIDTH + 6 * NSA_KV + 3 * NSA_HEADS
MEM_HEADS = 4
MEM_WIDTH = MEM_HEADS * HEAD_DIM
PAGE_SIZE = 128
NORM_EPS = 1e-6
BIG = 1e9
NEG = -1e30
LOG2_E = 1.4426950408889634
RW_CHAIN_BATCH = 4
ML_CHAIN_BATCH = 4
OUT_MEM_ROWS = 512
LANES = 128
VMEM_LIMIT = 56 * 1024 * 1024

P_RW = 0
P_ML = P_RW + RW_COLS
P_Q = P_ML + 3 * ML_WIDTH
P_KVC = P_Q + NSA_WIDTH
P_KVS = P_KVC + 2 * NSA_KV
P_KVW = P_KVS + 2 * NSA_KV
P_GATES = P_KVW + 2 * NSA_KV
P_TOTAL = P_GATES + LANES


def _dot(a, b):
    return jnp.dot(a.astype(BF16), b.astype(BF16), preferred_element_type=F32)


def _dot_nt(a, b):
    return lax.dot_general(a.astype(BF16), b.astype(BF16), (((1,), (1,)), ((), ())),
                           preferred_element_type=F32)


def _dot_tn(a, b):
    return lax.dot_general(a.astype(BF16), b.astype(BF16), (((0,), (0,)), ((), ())),
                           preferred_element_type=F32)


def _split2(x):
    hi = x.astype(BF16)
    lo = (x - hi.astype(F32)).astype(BF16)
    return hi, lo


def _split3(x):
    hi = x.astype(BF16)
    r = x - hi.astype(F32)
    mid = r.astype(BF16)
    lo = (r - mid.astype(F32)).astype(BF16)
    return hi, mid, lo


def _dot_sel(sel, x):
    hi, mid, lo = _split3(x)
    s = sel.astype(BF16)
    d = lambda t: jnp.dot(s, t, preferred_element_type=F32)
    return d(hi) + d(mid) + d(lo)


def _dot_x_sel(x, sel):
    hi, mid, lo = _split3(x)
    s = sel.astype(BF16)
    d = lambda t: jnp.dot(t, s, preferred_element_type=F32)
    return d(hi) + d(mid) + d(lo)


def _dot_hi(a, b):
    ah, al = _split2(a)
    bh, bl = _split2(b)
    d = lambda u, v: jnp.dot(u, v, preferred_element_type=F32)
    return d(ah, bh) + d(al, bh) + d(ah, bl)


def _dot_nt_hi(a, b):
    ah, al = _split2(a)
    bh, bl = _split2(b)
    d = lambda u, v: lax.dot_general(u, v, (((1,), (1,)), ((), ())), preferred_element_type=F32)
    return d(ah, bh) + d(al, bh) + d(ah, bl)


def _seg_mean_sq(x, seg_avg):
    hi, lo = _split2(x * x)
    d = lambda t: jnp.dot(t, seg_avg, preferred_element_type=F32)
    return d(hi) + d(lo)


def _seg_mean_sq_rows(x, seg_avg):
    hi, lo = _split2(x * x)
    d = lambda t: jnp.dot(seg_avg, t, preferred_element_type=F32)
    return d(hi) + d(lo)


def _rms_rows(x, g):
    return x * lax.rsqrt(jnp.mean(x * x, axis=-1, keepdims=True) + NORM_EPS) * g


def _sigmoid(x):
    return 1.0 / (1.0 + jnp.exp(-x))


def _softplus(x):
    return jnp.maximum(x, 0.0) + jnp.log(1.0 + jnp.exp(-jnp.abs(x)))


def _row_tile(n, target):
    t = min(n, target)
    while n % t:
        t //= 2
    return t


def _scan_chunk(t, candidates):
    for c in candidates:
        if t % c == 0:
            return c
    return t


def _const_spec(shape):
    nd = len(shape)
    return pl.BlockSpec(shape, lambda *_: (0,) * nd, pipeline_mode=pl.Buffered(1))


def _seg_avg_matrix(width, scale):
    i = jnp.arange(width) // HEAD_DIM
    return (jnp.where(i[:, None] == i[None, :], scale, 0.0)).astype(BF16)


def _channel_major(x):
    b, t = x.shape[:2]
    return jnp.transpose(x, (0, 2, 3, 4, 1)).reshape(b, -1, t)


def _token_major(x, c, g):
    b, w, t = x.shape
    return jnp.transpose(x.reshape(b, c, g, w // (c * g), t), (0, 4, 1, 2, 3))


def _ffn_body(x_ref, g_ref, wg_ref, wu_ref, wd_ref, o_ref, *, f_chunk):
    x = x_ref[...]
    h = _rms_rows(x, g_ref[...]).astype(BF16)
    d_ff = wg_ref.shape[1]
    acc = jnp.zeros_like(x)
    for c in range(d_ff // f_chunk):
        sl = slice(c * f_chunk, (c + 1) * f_chunk)
        gate = jnp.dot(h, wg_ref[:, sl], preferred_element_type=F32)
        up = jnp.dot(h, wu_ref[:, sl], preferred_element_type=F32)
        act = (gate * _sigmoid(gate) * up).astype(BF16)
        acc = acc + jnp.dot(act, wd_ref[sl, :], preferred_element_type=F32)
    o_ref[...] = x + 0.5 * acc


def _ffn(x, g, w_up, w_down):
    n, d = x.shape
    d_ff = w_down.shape[0]
    tm = _row_tile(n, 512)
    f_chunk = d_ff
    wg = w_up[:, :d_ff].astype(BF16)
    wu = w_up[:, d_ff:].astype(BF16)
    wd = w_down.astype(BF16)
    row = pl.BlockSpec((tm, d), lambda i: (i, 0))
    return pl.pallas_call(
        functools.partial(_ffn_body, f_chunk=f_chunk),
        grid=(n // tm,),
        in_specs=[row, _const_spec((1, d)), _const_spec((d, d_ff)), _const_spec((d, d_ff)),
                  _const_spec((d_ff, d))],
        out_specs=row,
        out_shape=jax.ShapeDtypeStruct((n, d), F32),
        compiler_params=pltpu.CompilerParams(dimension_semantics=("parallel",),
                                             vmem_limit_bytes=VMEM_LIMIT),
        name="ffn",
    )(x, g.reshape(1, d), wg, wu, wd)


def _in_proj_body(x_ref, g_ref, w_ref, gq_ref, gks_ref, gkw_ref, avg_ref, rw_ref, ml_ref, gates_ref, q_ref,
                  *kv_refs, channel_major):
    h = _rms_rows(x_ref[...], g_ref[...]).astype(BF16)
    p = jnp.dot(h, w_ref[...], preferred_element_type=F32)
    rw_ref[...] = p[:, P_RW:P_ML]
    ml_ref[...] = p[:, P_ML:P_Q]
    gates_ref[...] = p[:, P_GATES:P_TOTAL]
    avg = avg_ref[...]

    def head_norm(z, g):
        w = z.shape[1]
        return z * lax.rsqrt(_seg_mean_sq(z, avg[:w, :w]) + NORM_EPS) * g

    q_ref[...] = head_norm(p[:, P_Q:P_KVC], gq_ref[...])
    kvc = p[:, P_KVC:P_KVS]
    ks = head_norm(p[:, P_KVS:P_KVS + NSA_KV], gks_ref[...])
    vs = p[:, P_KVS + NSA_KV:P_KVW]
    kw = head_norm(p[:, P_KVW:P_KVW + NSA_KV], gkw_ref[...])
    vw = p[:, P_KVW + NSA_KV:P_GATES]
    if channel_major:
        ks_ref, kw_ref, kvc_t_ref, kvs_t_ref, kvw_t_ref = kv_refs
        ks_ref[...] = ks.astype(BF16)
        kw_ref[...] = kw.astype(BF16)
        kvc_t_ref[...] = kvc.T
        kvs_t_ref[:NSA_KV, :] = ks.T
        kvs_t_ref[NSA_KV:, :] = vs.T
        kvw_t_ref[:NSA_KV, :] = kw.T
        kvw_t_ref[NSA_KV:, :] = vw.T
    else:
        kvc_ref, kvs_ref, kvw_ref = kv_refs
        kvc_ref[...] = kvc
        kvs_ref[:, :NSA_KV] = ks
        kvs_ref[:, NSA_KV:] = vs
        kvw_ref[:, :NSA_KV] = kw
        kvw_ref[:, NSA_KV:] = vw


def _permute_w_in(w_in):
    d = w_in.shape[0]
    o_ml = RW_COLS
    o_nsa = RW_COLS + ML_COLS
    ml_gates = w_in[:, o_ml + 3 * ML_WIDTH:o_nsa]
    nsa_gates = w_in[:, o_nsa + NSA_WIDTH + 6 * NSA_KV:]
    pad = jnp.zeros((d, LANES - 2 * ML_HEADS - 3 * NSA_HEADS), w_in.dtype)
    return jnp.concatenate([w_in[:, :o_ml + 3 * ML_WIDTH], w_in[:, o_nsa:o_nsa + NSA_WIDTH + 6 * NSA_KV],
                            ml_gates, nsa_gates, pad], axis=1).astype(BF16)


def _in_proj(x, g, w_perm, qk_g, channel_major):
    B, T, d = x.shape
    n = B * T
    tm = _row_tile(T if channel_major else n, 512)
    tpb = T // tm
    row = lambda w: pl.BlockSpec((tm, w), lambda i: (i, 0))
    widths = [RW_COLS, 3 * ML_WIDTH, LANES, NSA_WIDTH]
    out_specs = [row(w) for w in widths]
    out_shape = [jax.ShapeDtypeStruct((n, w), F32) for w in widths]
    w_kv = 2 * NSA_KV
    if channel_major:
        out_specs += [row(NSA_KV)] * 2 + [pl.BlockSpec((None, w_kv, tm), lambda i: (i // tpb, 0, i % tpb))] * 3
        out_shape += [jax.ShapeDtypeStruct((n, NSA_KV), BF16)] * 2 + [jax.ShapeDtypeStruct((B, w_kv, T), F32)] * 3
    else:
        out_specs += [row(w_kv)] * 3
        out_shape += [jax.ShapeDtypeStruct((n, w_kv), F32)] * 3
    gq = jnp.tile(qk_g[0], NSA_HEADS).reshape(1, NSA_WIDTH)
    gks = jnp.tile(qk_g[2], NSA_KV_HEADS).reshape(1, NSA_KV)
    gkw = jnp.tile(qk_g[3], NSA_KV_HEADS).reshape(1, NSA_KV)
    avg = _seg_avg_matrix(NSA_WIDTH, 1.0 / HEAD_DIM)
    outs = pl.pallas_call(
        functools.partial(_in_proj_body, channel_major=channel_major),
        grid=(n // tm,),
        in_specs=[row(d), _const_spec((1, d)), _const_spec((d, P_TOTAL)), _const_spec((1, NSA_WIDTH)),
                  _const_spec((1, NSA_KV)), _const_spec((1, NSA_KV)), _const_spec((NSA_WIDTH, NSA_WIDTH))],
        out_specs=out_specs,
        out_shape=out_shape,
        compiler_params=pltpu.CompilerParams(dimension_semantics=("parallel",),
                                             vmem_limit_bytes=VMEM_LIMIT),
        name="in_proj",
    )(x.reshape(n, d), g.reshape(1, d), w_perm, gq, gks, gkw, avg)
    return [o.reshape(B, T, -1) if o.shape[0] == n else o for o in outs]


def _rwkv_chunk_inputs(p, prev_row, mu_ref, w0_ref, lw_ref, a0_ref, g2_ref, kks_ref, ka_ref, rk_ref, tri_ref,
                       seg_ref):
    C = p.shape[0]
    rows = lax.broadcasted_iota(jnp.int32, (C, 1), 0)
    prev = jnp.where(rows == 0, prev_row, pltpu.roll(p, 1, axis=0))
    xm = p + (prev - p) * mu_ref[...]
    r = xm[:, 0:RW_WIDTH]
    k = xm[:, RW_WIDTH:2 * RW_WIDTH]
    v = xm[:, 2 * RW_WIDTH:3 * RW_WIDTH]
    lin = xm[:, 3 * RW_WIDTH:3 * RW_WIDTH + RW_W_LORA + RW_A_LORA]
    lane = lax.broadcasted_iota(jnp.int32, lin.shape, 1)
    lora = _dot_hi(jnp.where(lane < RW_W_LORA, jnp.tanh(lin), lin), lw_ref[...])
    w = -_softplus(-(w0_ref[...] + lora[:, :RW_WIDTH])) - 0.5
    a = _sigmoid(a0_ref[...] + lora[:, RW_WIDTH:])
    g = _dot(_sigmoid(xm[:, 3 * RW_WIDTH + RW_W_LORA + RW_A_LORA:]), g2_ref[...])
    kk = k * kks_ref[...]
    hi, lo = _split2(kk * kk)
    seg = seg_ref[...]
    ss = jnp.dot(hi, seg, preferred_element_type=F32) + jnp.dot(lo, seg, preferred_element_type=F32)
    kk = kk / jnp.maximum(jnp.sqrt(ss), 1e-12)
    k = k * (1.0 + (a - 1.0) * ka_ref[...])
    logdec = -jnp.exp(w)
    tri = tri_ref[...]
    G = _dot_sel(tri, logdec)
    g_end = G[C - 1:C, :]
    e_g = jnp.exp(G)
    e_gi = jnp.exp(-G)
    kkd = kk * jnp.exp(G - logdec)
    rd = r * e_g
    b = kk * a
    bi = b * e_gi
    ki = k * e_gi
    e_end = jnp.exp(g_end - G)
    bi2 = b * e_end
    ki2 = k * e_end
    dec_end = jnp.exp(g_end)
    rkk = r * k * rk_ref[...]
    return dict(v=v, g=g, kkd=kkd, rd=rd, bi=bi, ki=ki, bi2=bi2, ki2=ki2, dec_end=dec_end, rkk=rkk)


def _rwkv_body(p_ref, shift_ref, s0_ref, mu_ref, w0_ref, lw_ref, a0_ref, g2_ref, kks_ref, ka_ref,
               rk_ref, gng_ref, gnb_ref, tri_ref, seg_ref, o_ref, st_ref, carry_ref, s_ref, *, chunk, nb):
    C = chunk
    N = HEAD_DIM

    @pl.when(pl.program_id(1) == 0)
    def _():
        carry_ref[...] = shift_ref[...]
        s_ref[...] = s0_ref[...]

    pre = []
    for bi in range(nb):
        p = p_ref[bi]
        pre.append(_rwkv_chunk_inputs(p, carry_ref[bi], mu_ref, w0_ref, lw_ref, a0_ref, g2_ref, kks_ref, ka_ref,
                                      rk_ref, tri_ref, seg_ref))
        carry_ref[bi] = p[C - 1:C, :]

    ti = lax.broadcasted_iota(jnp.int32, (C, C), 0)
    si = lax.broadcasted_iota(jnp.int32, (C, C), 1)
    strict = si < ti
    eye = (si == ti).astype(F32)
    s2 = lax.broadcasted_iota(jnp.int32, (C, 2 * C), 1)
    incl2 = jnp.where(s2 < C, s2, s2 - C) <= lax.broadcasted_iota(jnp.int32, (C, 2 * C), 0)

    units = [(bi, h) for bi in range(nb) for h in range(RW_HEADS)]
    col = lambda name, u: pre[u[0]][name][:, u[1] * N:(u[1] + 1) * N]
    lhs = [jnp.concatenate([col('kkd', u), col('rd', u)], axis=0) for u in units]
    rhs = [jnp.concatenate([col('bi', u), col('ki', u)], axis=0) for u in units]
    vh = [col('v', u) for u in units]
    s0 = [s_ref[bi, h] for bi, h in units]
    m4 = [_dot_nt(a, b) for a, b in zip(lhs, rhs)]
    ks = [_dot_nt(a, s) for a, s in zip(lhs, s0)]
    pw = [jnp.where(strict, -m[:C, :C], 0.0) for m in m4]
    lk = [jnp.where(strict, m[:C, C:], 0.0) for m in m4]
    mbk = [jnp.where(incl2, m[C:, :], 0.0) for m in m4]
    lkv = [_dot(a, b) for a, b in zip(lk, vh)]
    t_inv = [eye + n for n in pw]
    span = 2
    while span < C:
        pw = [_dot(x, x) for x in pw]
        t_inv = [t + _dot(t, x) for t, x in zip(t_inv, pw)]
        span *= 2
    u_ = [_dot(t, -(k_[:C] + l)) for t, k_, l in zip(t_inv, ks, lkv)]
    uv = [jnp.concatenate([a, b], axis=0) for a, b in zip(u_, vh)]
    y = [k_[C:] + _dot(m, x) for k_, m, x in zip(ks, mbk, uv)]
    s_new = [s * col('dec_end', u) + _dot_tn(x, jnp.concatenate([col('bi2', u), col('ki2', u)], axis=0))
             for s, x, u in zip(s0, uv, units)]
    for (bi, h), s, yy, v_ in zip(units, s_new, y, vh):
        sl = slice(h * N, (h + 1) * N)
        s_ref[bi, h] = s
        mean = jnp.mean(yy, axis=-1, keepdims=True)
        var = jnp.mean(jnp.square(yy - mean), axis=-1, keepdims=True)
        yn = (yy - mean) * lax.rsqrt(var + RW_GN_EPS) * gng_ref[:, sl] + gnb_ref[:, sl]
        bonus = jnp.sum(pre[bi]['rkk'][:, sl], axis=-1, keepdims=True) * v_
        o_ref[bi, :, sl] = (yn + bonus) * pre[bi]['g'][:, sl]

    @pl.when(pl.program_id(1) == pl.num_programs(1) - 1)
    def _():
        st_ref[...] = s_ref[...]


def _rwkv(p_rw, shift_prev, s0, W):
    B, T, _ = p_rw.shape
    C = _scan_chunk(T, RW_CHUNKS)
    nb = _row_tile(B, RW_CHAIN_BATCH)
    row = lambda v: v.reshape(1, -1).astype(F32)
    z = jnp.zeros((RW_W_LORA, RW_WIDTH), F32)
    lw = jnp.concatenate([jnp.concatenate([W['rw_w2'], z], axis=1),
                          jnp.concatenate([z, W['rw_a2']], axis=1)], axis=0)
    tri = (jnp.arange(C)[:, None] >= jnp.arange(C)[None, :]).astype(BF16)
    seg = _seg_avg_matrix(RW_WIDTH, 1.0)
    consts = [row(W['rw_mu']), row(W['rw_w0']), lw, row(W['rw_a0']), W['rw_g2'].astype(BF16), row(W['rw_kk']),
              row(W['rw_ka']), row(W['rw_rk']), row(W['rw_gn_g']), row(W['rw_gn_b']), tri, seg]
    state = pl.BlockSpec((nb, RW_HEADS, HEAD_DIM, HEAD_DIM), lambda b, c: (b, 0, 0, 0))
    out, s_t = pl.pallas_call(
        functools.partial(_rwkv_body, chunk=C, nb=nb),
        grid=(B // nb, T // C),
        in_specs=[pl.BlockSpec((nb, C, RW_COLS), lambda b, c: (b, c, 0)),
                  pl.BlockSpec((nb, 1, RW_COLS), lambda b, c: (b, 0, 0)), state]
                 + [_const_spec(c.shape) for c in consts],
        out_specs=[pl.BlockSpec((nb, C, RW_WIDTH), lambda b, c: (b, c, 0)), state],
        out_shape=[jax.ShapeDtypeStruct((B, T, RW_WIDTH), F32),
                   jax.ShapeDtypeStruct((B, RW_HEADS, HEAD_DIM, HEAD_DIM), F32)],
        scratch_shapes=[pltpu.VMEM((nb, 1, RW_COLS), F32), pltpu.VMEM((nb, RW_HEADS, HEAD_DIM, HEAD_DIM), F32)],
        compiler_params=pltpu.CompilerParams(dimension_semantics=("parallel", "arbitrary"),
                                             vmem_limit_bytes=VMEM_LIMIT),
        name="rwkv7",
    )(p_rw, shift_prev, s0, *consts)
    return out, s_t


def _mlstm_body(p_ref, gc_ref, gt_ref, convp_ref, c0_ref, n0_ref, m0_ref, cw_ref, cb_ref, wq_ref, wk_ref,
                bias_r_ref, bias_c_ref, ng_ref, skip_ref, tril_ref, triu_ref,
                o_ref, ct_ref, nt_ref, mt_ref, ext_ref, c_ref, n_ref, m_ref, *, chunk, nb):
    L = chunk
    N = HEAD_DIM
    H = ML_HEADS

    @pl.when(pl.program_id(1) == 0)
    def _():
        ext_ref[:, 0:8, :] = convp_ref[...]
        c_ref[...] = c0_ref[...]
        n_ref[...] = n0_ref[...]
        m_ref[...] = m0_ref[...]

    cw = cw_ref[...]
    pre = []
    for bi in range(nb):
        p = p_ref[bi]
        x = p[:, 0:ML_WIDTH]
        ext_ref[bi, 8:8 + L, :] = x
        conv = (cb_ref[...] + cw[3:4, :] * x + cw[2:3, :] * ext_ref[bi, 7:7 + L, :]
                + cw[1:2, :] * ext_ref[bi, 6:6 + L, :] + cw[0:1, :] * ext_ref[bi, 5:5 + L, :])
        tail = ext_ref[bi, L:L + 8, :]
        ext_ref[bi, 0:8, :] = tail
        ca = conv * _sigmoid(conv)
        gcb = gc_ref[bi] + bias_r_ref[...]
        gtb = gt_ref[bi, 0] + bias_c_ref[...][:, 0:1]
        pre.append(dict(
            v=p[:, ML_WIDTH:2 * ML_WIDTH], o_pre=p[:, 2 * ML_WIDTH:3 * ML_WIDTH], ca=ca,
            q=_dot(ca, wq_ref[...]), k=_dot(ca, wk_ref[...]) * (N ** -0.5), gcb=gcb, gtb=gtb,
            bcum_c=_dot_sel(tril_ref[...], -_softplus(-gcb)),
            bcum_r=_dot_x_sel(-_softplus(-gtb), triu_ref[...])))

    ti = lax.broadcasted_iota(jnp.int32, (L, L), 0)
    si = lax.broadcasted_iota(jnp.int32, (L, L), 1)
    causal = si <= ti

    units = [(bi, h) for bi in range(nb) for h in range(H)]
    col = lambda name, u: pre[u[0]][name][:, u[1] * N:(u[1] + 1) * N]
    qh = [col('q', u) for u in units]
    kh = [col('k', u) for u in units]
    vh = [col('v', u) for u in units]
    cs = [c_ref[bi, h] for bi, h in units]
    ns = [n_ref[bi, h:h + 1, :] for bi, h in units]
    m_prev = [m_ref[bi, h:h + 1, 0:1] for bi, h in units]
    b_c = [pre[bi]['bcum_c'][:, H + h:H + h + 1] for bi, h in units]
    i_c = [pre[bi]['gcb'][:, h:h + 1] for bi, h in units]
    b_r = [pre[bi]['bcum_r'][H + h:H + h + 1, :] for bi, h in units]
    i_r = [pre[bi]['gtb'][h:h + 1, :] for bi, h in units]
    qk = [_dot_nt(a, b) for a, b in zip(qh, kh)]
    qc = [_dot_nt(a, c) for a, c in zip(qh, cs)]
    dmat = [jnp.where(causal, bc - br + ir, -jnp.inf) for bc, br, ir in zip(b_c, b_r, i_r)]
    inter = [bc + mp for bc, mp in zip(b_c, m_prev)]
    m_t = [jnp.maximum(it, jnp.max(dm, axis=-1, keepdims=True)) for it, dm in zip(inter, dmat)]
    a = [x * jnp.exp(dm - mt) for x, dm, mt in zip(qk, dmat, m_t)]
    av = [_dot(x, v_) for x, v_ in zip(a, vh)]
    b_end = [bc[L - 1:L, :] for bc in b_c]
    m_new = [jnp.maximum(be + mp, jnp.max(be - br + ir, axis=-1, keepdims=True))
             for be, mp, br, ir in zip(b_end, m_prev, b_r, i_r)]
    dec = [jnp.exp(be + mp - mn) for be, mp, mn in zip(b_end, m_prev, m_new)]
    wg = [jnp.exp(be - bc + ic - mn) for be, bc, ic, mn in zip(b_end, b_c, i_c, m_new)]
    c_new = [d * c + _dot_tn(w * v_, k_) for d, c, w, v_, k_ in zip(dec, cs, wg, vh, kh)]
    for idx, (bi, h) in enumerate(units):
        sl = slice(h * N, (h + 1) * N)
        sc = jnp.exp(inter[idx] - m_t[idx])
        num = sc * qc[idx] + av[idx]
        den = (sc * jnp.sum(qh[idx] * ns[idx], axis=-1, keepdims=True)
               + jnp.sum(a[idx], axis=-1, keepdims=True))
        hh = num / jnp.maximum(jnp.abs(den), jnp.exp(-m_t[idx]))
        c_ref[bi, h] = c_new[idx]
        n_ref[bi, h:h + 1, :] = dec[idx] * ns[idx] + jnp.sum(wg[idx] * kh[idx], axis=0, keepdims=True)
        m_ref[bi, h:h + 1, :] = jnp.broadcast_to(m_new[idx], (1, LANES))
        hn = hh * lax.rsqrt(jnp.mean(hh * hh, axis=-1, keepdims=True) + NORM_EPS)
        o_ref[bi, :, sl] = ((hn * ng_ref[:, sl] + skip_ref[:, sl] * pre[bi]['ca'][:, sl])
                            * _sigmoid(pre[bi]['o_pre'][:, sl]))

    @pl.when(pl.program_id(1) == pl.num_programs(1) - 1)
    def _():
        ct_ref[...] = c_ref[...]
        nt_ref[...] = n_ref[...]
        mt_ref[...] = m_ref[...]


def _block_diag(w):
    H, N, _ = w.shape
    eye = jnp.eye(H, dtype=w.dtype)
    return (eye[:, None, :, None] * w[:, :, None, :]).reshape(H * N, H * N)


def _mlstm(p_ml, gates, conv_prev, c0, n0, m0, W):
    B, T, _ = p_ml.shape
    H, N = ML_HEADS, HEAD_DIM
    L = _scan_chunk(T, ML_CHUNKS)
    nc = T // L
    gt = jnp.swapaxes(gates[:, :, :2 * H].reshape(B, nc, L, 2 * H), 2, 3)
    convp = jnp.concatenate([jnp.zeros((B, 8 - (ML_CONV - 1), ML_WIDTH), F32), conv_prev], axis=1)
    m0b = jnp.broadcast_to(jnp.pad(m0, ((0, 0), (0, 8 - H)))[:, :, None], (B, 8, LANES))
    bias = W['ml_gate_b'].reshape(2 * H)
    bias_r = jnp.pad(bias, (0, LANES - 2 * H)).reshape(1, LANES)
    bias_c = jnp.broadcast_to(bias[:, None], (2 * H, LANES))
    tril = (jnp.arange(L)[:, None] >= jnp.arange(L)[None, :]).astype(BF16)
    row = lambda z: z.reshape(1, -1).astype(F32)
    consts = [W['ml_conv_w'], row(W['ml_conv_b']), _block_diag(W['ml_wq']).astype(BF16),
              _block_diag(W['ml_wk']).astype(BF16), bias_r, bias_c, row(W['ml_norm_g']), row(W['ml_skip']),
              tril, tril.T]
    nb = _row_tile(B, ML_CHAIN_BATCH)
    per_b = lambda *shape: pl.BlockSpec((nb,) + shape, lambda b, c: (b,) + (0,) * len(shape))
    out, c_t, n_t, m_t = pl.pallas_call(
        functools.partial(_mlstm_body, chunk=L, nb=nb),
        grid=(B // nb, nc),
        in_specs=[pl.BlockSpec((nb, L, 3 * ML_WIDTH), lambda b, c: (b, c, 0)),
                  pl.BlockSpec((nb, L, LANES), lambda b, c: (b, c, 0)),
                  pl.BlockSpec((nb, 1, 2 * H, L), lambda b, c: (b, c, 0, 0)),
                  per_b(8, ML_WIDTH), per_b(H, N, N), per_b(H, N), per_b(8, LANES)]
                 + [_const_spec(c.shape) for c in consts],
        out_specs=[pl.BlockSpec((nb, L, ML_WIDTH), lambda b, c: (b, c, 0)),
                   per_b(H, N, N), per_b(H, N), per_b(8, LANES)],
        out_shape=[jax.ShapeDtypeStruct((B, T, ML_WIDTH), F32), jax.ShapeDtypeStruct((B, H, N, N), F32),
                   jax.ShapeDtypeStruct((B, H, N), F32), jax.ShapeDtypeStruct((B, 8, LANES), F32)],
        scratch_shapes=[pltpu.VMEM((nb, L + 8, ML_WIDTH), F32), pltpu.VMEM((nb, H, N, N), F32),
                        pltpu.VMEM((nb, H, N), F32), pltpu.VMEM((nb, 8, LANES), F32)],
        compiler_params=pltpu.CompilerParams(dimension_semantics=("parallel", "arbitrary"),
                                             vmem_limit_bytes=VMEM_LIMIT),
        name="mlstm",
    )(p_ml, gates, gt, convp, c0, n0, m0b, *consts)
    return out, c_t, n_t, m_t[:, :H, 0]


def _compress_body(*refs, n_parts, paged):
    if paged:
        refs = refs[1:]
    part_refs = refs[:n_parts]
    wpos_ref, sel_ref, wc_ref, gk_ref, avg_ref, o_ref = refs[n_parts:]
    wpos = wpos_ref[...]
    parts = []
    for r in part_refs:
        x = r[...]
        parts += [x[:, c * LANES:(c + 1) * LANES] * wpos for c in range(x.shape[1] // LANES)]
    c = _dot(jnp.concatenate(parts, axis=1), sel_ref[...])
    c = _dot_hi(wc_ref[...], c)
    ck = c[:NSA_KV]
    o_ref[:NSA_KV, :] = ck * lax.rsqrt(_seg_mean_sq_rows(ck, avg_ref[...]) + NORM_EPS) * gk_ref[...]
    o_ref[NSA_KV:, :] = c[NSA_KV:]


def _compress_consts(W, n_rows):
    w = 2 * NSA_KV
    wpos = W['nsa_wpos'].reshape(NSA_BLOCK, w).T
    wc_t = _block_diag(W['nsa_wc'].reshape(2 * NSA_KV_HEADS, HEAD_DIM, HEAD_DIM)).T
    gk = jnp.tile(W['nsa_qk_g'][1], NSA_KV_HEADS).reshape(NSA_KV, 1)
    sel = (jnp.arange(n_rows)[:, None] // NSA_BLOCK == jnp.arange(n_rows // NSA_BLOCK)[None, :]).astype(BF16)
    return [jnp.concatenate([wpos, wpos], axis=1), sel, wc_t, gk, _seg_avg_matrix(NSA_KV, 1.0 / HEAD_DIM)]


def _nsa_compress_dense(kvc_t, W):
    B, w, T = kvc_t.shape
    consts = _compress_consts(W, T)
    return pl.pallas_call(
        functools.partial(_compress_body, n_parts=1, paged=False),
        grid=(B,),
        in_specs=[pl.BlockSpec((None, w, T), lambda b: (b, 0, 0))] + [_const_spec(c.shape) for c in consts],
        out_specs=pl.BlockSpec((None, w, T // NSA_BLOCK), lambda b: (b, 0, 0)),
        out_shape=jax.ShapeDtypeStruct((B, w, T // NSA_BLOCK), F32),
        compiler_params=pltpu.CompilerParams(dimension_semantics=("parallel",), vmem_limit_bytes=VMEM_LIMIT),
        name="nsa_compress_dense",
    )(kvc_t, *consts)


def _nsa_compress_paged(pool_t, page_table, W):
    B, n_pages = page_table.shape
    w = pool_t.shape[1]
    pages = min(64, n_pages)
    bpp = PAGE_SIZE // NSA_BLOCK
    consts = _compress_consts(W, pages * PAGE_SIZE)

    def page_spec(k):
        return pl.BlockSpec((None, w, PAGE_SIZE), lambda b, i, pt: (pt[b * n_pages + i * pages + k], 0, 0))

    const = lambda c: pl.BlockSpec(c.shape, lambda b, i, pt: (0,) * c.ndim)
    return pl.pallas_call(
        functools.partial(_compress_body, n_parts=pages, paged=True),
        grid_spec=pltpu.PrefetchScalarGridSpec(
            num_scalar_prefetch=1,
            grid=(B, n_pages // pages),
            in_specs=[page_spec(k) for k in range(pages)] + [const(c) for c in consts],
            out_specs=pl.BlockSpec((None, w, pages * bpp), lambda b, i, pt: (b, 0, i)),
        ),
        out_shape=jax.ShapeDtypeStruct((B, w, n_pages * bpp), F32),
        compiler_params=pltpu.CompilerParams(dimension_semantics=("parallel", "parallel"),
                                             vmem_limit_bytes=VMEM_LIMIT),
        name="nsa_compress_paged",
    )(page_table.reshape(-1), *([pool_t] * pages), *consts)


def _split_cmp(cmp_t):
    return jnp.swapaxes(cmp_t[:, :NSA_KV], 1, 2), cmp_t[:, NSA_KV:]


def _gate_bias_row(gate_b):
    return jnp.pad(gate_b, (2 * ML_HEADS, LANES - 2 * ML_HEADS - 3 * NSA_HEADS)).reshape(1, LANES)


def _gate_lane(g, r):
    return 2 * ML_HEADS + (g * NSA_GROUP + r) * 3


def _attend_t(k_tile, v_t, q_all, bias, carry):
    if not isinstance(k_tile, (list, tuple)):
        k_tile = [k_tile] * len(q_all)
    tk = k_tile[0].shape[0]
    ones = jnp.ones((8, tk), BF16)
    s = [jnp.dot(k, q, preferred_element_type=F32) for k, q in zip(k_tile, q_all)]
    s = [x if b is None else x + jnp.concatenate([b] * NSA_GROUP, axis=1) for x, b in zip(s, bias)]
    m_new = [jnp.maximum(m, jnp.max(x, axis=0, keepdims=True)) for x, (m, _) in zip(s, carry)]
    p = [jnp.exp2(x - m).astype(BF16) for x, m in zip(s, m_new)]
    pv = [jnp.dot(jnp.concatenate([v.astype(BF16), ones], axis=0), x, preferred_element_type=F32)
          for v, x in zip(v_t, p)]
    return tuple((mn, jnp.exp2(m - mn) * acc + x) for mn, (m, acc), x in zip(m_new, carry, pv))


def _attend_t_init(groups, width):
    return tuple((jnp.full((1, width), NEG, F32), jnp.zeros((HEAD_DIM + 8, width), F32)) for _ in range(groups))


def _attend_t_finish(carry):
    return [acc[:HEAD_DIM] / jnp.maximum(acc[HEAD_DIM:HEAD_DIM + 1], 1e-30) for _, acc in carry]


def _nsa_prompt_body(q_ref, gates_ref, gb_ref, ck_ref, cv_ref, ks_ref, vs_ref, kw_ref, vw_ref, o_ref,
                     ot_ref, *, tq, tk, n_blocks):
    R, N, G = NSA_GROUP, HEAD_DIM, NSA_KV_HEADS
    W = R * tq
    i = pl.program_id(1)
    t0 = i * tq
    qpos = t0 + lax.broadcasted_iota(jnp.int32, (1, tq), 1)
    q_t = q_ref[...].T * (N ** -0.5)
    gt_t = _sigmoid((gates_ref[...] + gb_ref[...]).T)
    ck = ck_ref[...]
    nc = ck.shape[0]
    zeros = jnp.zeros((N, W), F32)
    q_f = []
    for g in range(G):
        hs = jnp.concatenate([q_t[(g * R + r) * N:(g * R + r + 1) * N] for r in range(R)], axis=1)
        q_f.append(jnp.concatenate([hs, zeros] if g == 0 else [zeros, hs], axis=0))
    q_b = [(x * LOG2_E).astype(BF16) for x in q_f]

    blk_w = lax.broadcasted_iota(jnp.int32, (nc, W), 0)
    vis = (blk_w + 1) * NSA_BLOCK - 1 <= jnp.concatenate([qpos] * R, axis=1)
    o_c, imp = [], []
    for g in range(G):
        s = jnp.where(vis, _dot_hi(ck, q_f[g]), -jnp.inf)
        m = jnp.max(s, axis=0, keepdims=True)
        e = jnp.exp(s - jnp.where(m == -jnp.inf, 0.0, m))
        p = e / jnp.maximum(jnp.sum(e, axis=0, keepdims=True), 1e-30)
        o_c.append(_dot(cv_ref[g * N:(g + 1) * N, :], p))
        pg = p[:, 0:tq]
        for r in range(1, R):
            pg = pg + p[:, r * tq:(r + 1) * tq]
        imp.append(pg)
    imp = jnp.concatenate(imp, axis=1)

    blk_r = lax.broadcasted_iota(jnp.int32, (nc, G * tq), 0)
    cur = jnp.right_shift(jnp.concatenate([qpos] * G, axis=1), BLOCK_SHIFT)
    forced = (blk_r == 0) | (blk_r == cur) | (blk_r == cur - 1)
    score = jnp.where(forced, BIG, jnp.where(blk_r <= cur, imp, -BIG))
    rowf = blk_r.astype(F32)
    sel_bias = jnp.full((nc, G * tq), NEG, F32)
    for _ in range(min(NSA_TOPN, n_blocks)):
        m = jnp.max(score, axis=0, keepdims=True)
        first = jnp.min(jnp.where(score == m, rowf, float(nc)), axis=0, keepdims=True)
        hit = rowf == first
        sel_bias = jnp.where(hit, 0.0, sel_bias)
        score = jnp.where(hit, -jnp.inf, score)
    if nc < N:
        sel_bias = jnp.concatenate([sel_bias, jnp.zeros((N - nc, G * tq), F32)], axis=0)
    q_sel = []
    for g in range(G):
        own = q_f[g][g * N:(g + 1) * N] * LOG2_E
        blocks = jnp.concatenate([sel_bias[:, g * tq:(g + 1) * tq]] * R, axis=1)
        q_sel.append(jnp.concatenate([own, blocks] if g == 0 else [blocks, own], axis=0).astype(BF16))

    kpos_col = lax.broadcasted_iota(jnp.int32, (tk, 1), 0)
    lane = lax.broadcasted_iota(jnp.int32, (tk, G * N), 1)
    row_blk = jnp.right_shift(lax.broadcasted_iota(jnp.int32, (tk, G * N), 0), BLOCK_SHIFT)
    bpt = tk // NSA_BLOCK

    def sel_tile(kt, carry, diagonal):
        k0 = pl.multiple_of(kt * tk, tk)
        keys = ks_ref[pl.ds(k0, tk), :].astype(F32)
        member = jnp.where((lane & (N - 1)) == row_blk + kt * bpt, 1.0, 0.0)
        k_aug = [jnp.where((lane >= N) if g == 0 else (lane < N), member, keys).astype(BF16) for g in range(G)]
        causal = jnp.where(k0 + kpos_col <= qpos, 0.0, NEG) if diagonal else None
        v_t = [vs_ref[g * N:(g + 1) * N, pl.ds(k0, tk)] for g in range(G)]
        return _attend_t(k_aug, v_t, q_sel, [causal] * G, carry)

    last = t0 // tk
    carry = lax.fori_loop(0, last, functools.partial(sel_tile, diagonal=False), _attend_t_init(G, W))
    o_s = _attend_t_finish(sel_tile(last, carry, diagonal=True))

    tw = min(NSA_WINDOW + tq, kw_ref.shape[0])
    w0 = pl.multiple_of(jnp.maximum(t0 + tq - tw, 0), tq)
    d = qpos - (w0 + lax.broadcasted_iota(jnp.int32, (tw, 1), 0))
    bias = jnp.where((d >= 0) & (d < NSA_WINDOW), 0.0, NEG)
    v_t = [vw_ref[g * N:(g + 1) * N, pl.ds(w0, tw)] for g in range(G)]
    o_w = _attend_t_finish(_attend_t(kw_ref[pl.ds(w0, tw), :], v_t, q_b, [bias] * G, _attend_t_init(G, W)))

    for g in range(G):
        for r in range(R):
            c0 = _gate_lane(g, r)
            h = g * R + r
            cols = slice(r * tq, (r + 1) * tq)
            ot_ref[h * N:(h + 1) * N, :] = (gt_t[c0:c0 + 1] * o_c[g][:, cols] + gt_t[c0 + 1:c0 + 2] * o_s[g][:, cols]
                                            + gt_t[c0 + 2:c0 + 3] * o_w[g][:, cols])
    o_ref[...] = ot_ref[...].T


def _nsa_prompt(q, gates, ck, cv_t, ks, kvs_t, kw, kvw_t, W):
    B, T, _ = q.shape
    tq = min(256, T)
    tk = min(512, T)
    assert tk % tq == 0 and T % tk == 0
    assert NSA_KV_HEADS == 2 and ck.shape[1] <= HEAD_DIM
    per_b = lambda a: pl.BlockSpec((None,) + a.shape[1:], lambda b, i: (b,) + (0,) * (a.ndim - 1))
    tile = lambda w: pl.BlockSpec((None, tq, w), lambda b, i: (b, i, 0))
    v_rows = pl.BlockSpec((None, NSA_KV, T), lambda b, i: (b, 1, 0))
    gb = _gate_bias_row(W['nsa_gate_b'])
    return pl.pallas_call(
        functools.partial(_nsa_prompt_body, tq=tq, tk=tk, n_blocks=T // NSA_BLOCK),
        grid=(B, T // tq),
        in_specs=[tile(NSA_WIDTH), tile(LANES), pl.BlockSpec((1, LANES), lambda b, i: (0, 0)),
                  per_b(ck), per_b(cv_t), per_b(ks), v_rows, per_b(kw), v_rows],
        out_specs=tile(NSA_WIDTH),
        out_shape=jax.ShapeDtypeStruct((B, T, NSA_WIDTH), F32),
        scratch_shapes=[pltpu.VMEM((NSA_WIDTH, tq), F32)],
        compiler_params=pltpu.CompilerParams(dimension_semantics=("parallel", "arbitrary"),
                                             vmem_limit_bytes=VMEM_LIMIT),
        name="nsa_prompt",
    )(q, gates, gb, ck, cv_t, ks, kvs_t, kw, kvw_t)


def _stack_heads(qg):
    return jnp.concatenate([qg[:, r * HEAD_DIM:(r + 1) * HEAD_DIM] for r in range(NSA_GROUP)],
                           axis=0) * (HEAD_DIM ** -0.5)


def _tile_rows(x, n):
    return jnp.concatenate([x] * n, axis=0)


def _masked_softmax(s, mask):
    s = jnp.where(mask, s, -jnp.inf)
    m = jnp.max(s, axis=-1, keepdims=True)
    m = jnp.where(m == -jnp.inf, 0.0, m)
    e = jnp.exp(s - m)
    return e / jnp.maximum(jnp.sum(e, axis=-1, keepdims=True), 1e-30)


def _select_blocks(imp, qpos, n_blocks, k):
    t, w = imp.shape
    jf = lax.broadcasted_iota(jnp.int32, (t, w), 1)
    cur = jnp.right_shift(qpos, BLOCK_SHIFT)
    forced = (jf == 0) | (jf == cur) | (jf == cur - 1)
    score = jnp.where(forced, BIG, jnp.where(jf <= cur, imp, -BIG))
    score = jnp.where(jf < n_blocks, score, -jnp.inf)
    lane = jf.astype(F32)
    idx = []
    for _ in range(k):
        m = jnp.max(score, axis=-1, keepdims=True)
        i = jnp.min(jnp.where(score == m, lane, float(w)), axis=-1, keepdims=True)
        score = jnp.where(lane == i, -jnp.inf, score)
        idx.append(i)
    return idx


def _online_step(carry, s, ok, pv):
    m, l, acc = carry
    s = jnp.where(ok, s, -jnp.inf)
    m_new = jnp.maximum(m, jnp.max(s, axis=-1, keepdims=True))
    m_safe = jnp.where(m_new == -jnp.inf, 0.0, m_new)
    p = jnp.exp(s - m_safe)
    alpha = jnp.exp(m - m_safe)
    return m_new, alpha * l + jnp.sum(p, axis=-1, keepdims=True), alpha * acc + pv(p)


def _online_init(rows):
    return (jnp.full((rows, 1), -jnp.inf, F32), jnp.zeros((rows, 1), F32), jnp.zeros((rows, HEAD_DIM), F32))


def _online_finish(carry):
    _, l, acc = carry
    return acc / jnp.maximum(l, 1e-30)


def _nsa_sample_cmp_body(q_ref, ck_ref, cv_ref, kvw_ref, winp_ref, oc_ref, ow_ref, idx_ref, *, pos0, n_blocks):
    R, N = NSA_GROUP, HEAD_DIM
    t = q_ref.shape[0]
    wb = winp_ref.shape[1]
    qpos = pos0 + lax.broadcasted_iota(jnp.int32, (t, 1), 0)
    qpos_s = _tile_rows(qpos, R)
    q = q_ref[...]
    ck = ck_ref[...]
    nc = ck.shape[0]
    w_sel = -(-n_blocks // LANES) * LANES
    new = kvw_ref[...]
    oc_heads, ow_heads = [], []
    for g in range(NSA_KV_HEADS):
        krows = slice(g * N, (g + 1) * N)
        vrows = slice(NSA_KV + g * N, NSA_KV + (g + 1) * N)
        qs = _stack_heads(q[:, g * R * N:(g + 1) * R * N])
        blk = lax.broadcasted_iota(jnp.int32, (1, nc), 1)
        p = _masked_softmax(_dot_nt_hi(qs, ck[:, krows]), (blk + 1) * NSA_BLOCK - 1 <= qpos_s)
        o_c = _dot_nt(p, cv_ref[krows, :])
        imp = p[0:t]
        for r in range(1, R):
            imp = imp + p[r * t:(r + 1) * t]
        if w_sel > nc:
            imp = jnp.concatenate([imp, jnp.zeros((t, w_sel - nc), F32)], axis=1)
        idx = _select_blocks(imp, qpos, n_blocks, min(NSA_TOPN, n_blocks))
        lane = lax.broadcasted_iota(jnp.int32, (t, LANES), 1)
        tile = jnp.zeros((t, LANES), F32)
        for kk, col in enumerate(idx):
            tile = jnp.where(lane == kk, col, tile)
        idx_ref[g] = tile.astype(jnp.int32)

        carry = _online_init(R * t)
        d = qpos_s - (pos0 - wb + lax.broadcasted_iota(jnp.int32, (1, wb), 1))
        carry = _online_step(carry, _dot(qs, winp_ref[krows, :]), (d >= 0) & (d < NSA_WINDOW),
                             lambda p: _dot_nt(p, winp_ref[vrows, :]))
        d = qpos_s - (pos0 + lax.broadcasted_iota(jnp.int32, (1, t), 1))
        carry = _online_step(carry, _dot_nt(qs, new[:, krows]), (d >= 0) & (d < NSA_WINDOW),
                             lambda p: _dot(p, new[:, vrows]))
        o_w = _online_finish(carry)
        oc_heads += [o_c[r * t:(r + 1) * t] for r in range(R)]
        ow_heads += [o_w[r * t:(r + 1) * t] for r in range(R)]
    oc_ref[...] = jnp.concatenate(oc_heads, axis=1)
    ow_ref[...] = jnp.concatenate(ow_heads, axis=1)


def _nsa_sample_cmp(q, ck, cv_t, kv_w, win_prev_t, pos0, n_blocks):
    B, T, _ = q.shape
    per_b = lambda a: pl.BlockSpec((None,) + a.shape[1:], lambda b: (b,) + (0,) * (a.ndim - 1))
    out_b = lambda *s: pl.BlockSpec((None,) + s, lambda b: (b,) + (0,) * len(s))
    return pl.pallas_call(
        functools.partial(_nsa_sample_cmp_body, pos0=pos0, n_blocks=n_blocks),
        grid=(B,),
        in_specs=[per_b(q), per_b(ck), per_b(cv_t), per_b(kv_w), per_b(win_prev_t)],
        out_specs=[out_b(T, NSA_WIDTH), out_b(T, NSA_WIDTH), out_b(NSA_KV_HEADS, T, LANES)],
        out_shape=[jax.ShapeDtypeStruct((B, T, NSA_WIDTH), F32), jax.ShapeDtypeStruct((B, T, NSA_WIDTH), F32),
                   jax.ShapeDtypeStruct((B, NSA_KV_HEADS, T, LANES), jnp.int32)],
        compiler_params=pltpu.CompilerParams(dimension_semantics=("parallel",)),
        name="nsa_sample_cmp",
    )(q, ck, cv_t, kv_w, win_prev_t)


def _nsa_sample_sel_body(idx_ref, pt_ref, *refs, n_past, pos0, topk, tokens):
    R, N, G = NSA_GROUP, HEAD_DIM, NSA_KV_HEADS
    page_refs = refs[:tokens * G * topk]
    q_ref, new_ref, oc_ref, ow_ref, gates_ref, gb_ref, o_ref = refs[tokens * G * topk:]
    b = pl.program_id(0)
    n_t = pl.num_programs(1) * tokens
    lane = lax.broadcasted_iota(jnp.int32, (1, PAGE_SIZE), 1)
    units = [(pl.program_id(1) * tokens + tt, tt, g) for tt in range(tokens) for g in range(G)]
    q_rows = {tt: q_ref[pl.ds(t, 1), :] for t, tt, g in units}
    qs = [_stack_heads(q_rows[tt][:, g * R * N:(g + 1) * R * N]) for t, tt, g in units]
    scores, vals = [], []
    for (t, tt, g), q_g in zip(units, qs):
        new_k = new_ref[g * N:(g + 1) * N, :]
        new_v = new_ref[NSA_KV + g * N:NSA_KV + (g + 1) * N, :]
        sc, vs = [], []
        for k in range(topk):
            j = idx_ref[((b * n_t + t) * G + g) * topk + k]
            half = j % (PAGE_SIZE // NSA_BLOCK)
            is_new = j >= n_past
            page = page_refs[(tt * G + g) * topk + k]
            k_t = jnp.where(is_new, new_k, page[g * N:(g + 1) * N, :])
            kpos = (j - half) * NSA_BLOCK + lane
            ok = (jnp.right_shift(lane, BLOCK_SHIFT) == half) & (kpos <= pos0 + t)
            sc.append(jnp.where(ok, _dot(q_g, k_t), -jnp.inf))
            vs.append(jnp.where(is_new, new_v, page[NSA_KV + g * N:NSA_KV + (g + 1) * N, :]))
        scores.append(sc)
        vals.append(vs)
    o_s = []
    for sc, vs in zip(scores, vals):
        m = sc[0].max(axis=-1, keepdims=True)
        for s in sc[1:]:
            m = jnp.maximum(m, s.max(axis=-1, keepdims=True))
        m = jnp.where(m == -jnp.inf, 0.0, m)
        l = jnp.zeros((R, 1), F32)
        acc = jnp.zeros((R, N), F32)
        for s, v_t in zip(sc, vs):
            e = jnp.exp(s - m)
            l = l + jnp.sum(e, axis=-1, keepdims=True)
            acc = acc + _dot_nt(e, v_t)
        o_s.append(acc / jnp.maximum(l, 1e-30))
    for tt in range(tokens):
        t = pl.program_id(1) * tokens + tt
        gt = _sigmoid(gates_ref[pl.ds(t, 1), :] + gb_ref[...])
        o_c = oc_ref[pl.ds(t, 1), :]
        o_w = ow_ref[pl.ds(t, 1), :]
        heads = []
        for g in range(G):
            for r in range(R):
                c0 = (g * R + r) * N
                gl = _gate_lane(g, r)
                heads.append(gt[:, gl:gl + 1] * o_c[:, c0:c0 + N] + gt[:, gl + 1:gl + 2] * o_s[tt * G + g][r:r + 1]
                             + gt[:, gl + 2:gl + 3] * o_w[:, c0:c0 + N])
        o_ref[pl.ds(t, 1), :] = jnp.concatenate(heads, axis=1)


def _nsa_sample_sel(q, idx, page_table, pool_t, kv_s_new, o_c, o_w, gates, W, pos0):
    B, T, _ = q.shape
    G, N = NSA_KV_HEADS, HEAD_DIM
    topk = idx.shape[-1]
    n_pages = page_table.shape[1]
    bpp = PAGE_SIZE // NSA_BLOCK
    n_past = pos0 // NSA_BLOCK
    assert n_past % bpp == 0 and T <= NSA_BLOCK
    new_t = jnp.swapaxes(jnp.pad(kv_s_new, ((0, 0), (0, PAGE_SIZE - T), (0, 0))), 1, 2)

    tokens = _row_tile(T, 4)

    def page_spec(tt, g, k):
        def index(b, t, idx_ref, pt_ref):
            j = jnp.minimum(idx_ref[((b * T + t * tokens + tt) * G + g) * topk + k], n_past - 1)
            return (pt_ref[b * n_pages + j // bpp], 0, 0)
        return pl.BlockSpec((None,) + pool_t.shape[1:], index)

    per_b = lambda a: pl.BlockSpec((None,) + a.shape[1:], lambda b, t, i, p: (b,) + (0,) * (a.ndim - 1))
    gb = _gate_bias_row(W['nsa_gate_b'])
    kv_specs = [page_spec(tt, g, k) for tt in range(tokens) for g in range(G) for k in range(topk)]
    return pl.pallas_call(
        functools.partial(_nsa_sample_sel_body, n_past=n_past, pos0=pos0, topk=topk, tokens=tokens),
        grid_spec=pltpu.PrefetchScalarGridSpec(
            num_scalar_prefetch=2,
            grid=(B, T // tokens),
            in_specs=kv_specs + [per_b(q), per_b(new_t), per_b(o_c), per_b(o_w), per_b(gates),
                                 pl.BlockSpec((1, LANES), lambda b, t, i, p: (0, 0))],
            out_specs=pl.BlockSpec((None, T, NSA_WIDTH), lambda b, t, i, p: (b, 0, 0)),
        ),
        out_shape=jax.ShapeDtypeStruct((B, T, NSA_WIDTH), F32),
        compiler_params=pltpu.CompilerParams(dimension_semantics=("parallel", "arbitrary")),
        name="nsa_sample_sel",
    )(idx.reshape(-1), page_table.reshape(-1), *([pool_t] * len(kv_specs)), q, new_t, o_c, o_w, gates, gb)


def _nsa_sample_mixer(q, gates, kv_s, kv_w, page_table, pool_cmp_t, pool_sel_t, win_prev_t, W):
    B, T, _ = q.shape
    pos0 = page_table.shape[1] * PAGE_SIZE
    n_blocks = -(-(pos0 + T) // NSA_BLOCK)
    ck, cv_t = _split_cmp(_nsa_compress_paged(pool_cmp_t, page_table, W))
    o_c, o_w, idx = _nsa_sample_cmp(q, ck, cv_t, kv_w, win_prev_t, pos0, n_blocks)
    idx = jnp.swapaxes(idx[..., :min(NSA_TOPN, n_blocks)], 1, 2)
    return _nsa_sample_sel(q, idx, page_table, pool_sel_t, kv_s, o_c, o_w, gates, W, pos0)


def _mem_kv_body(x_ref, g_ref, w_ref, gk_ref, avg_ref, o_ref):
    h = _rms_rows(x_ref[...], g_ref[...]).astype(BF16)
    kv = lax.dot_general(w_ref[...], h, (((1,), (1,)), ((), ())), preferred_element_type=F32)
    k = kv[:MEM_WIDTH]
    o_ref[:MEM_WIDTH, :] = k * lax.rsqrt(_seg_mean_sq_rows(k, avg_ref[...]) + NORM_EPS) * gk_ref[...]
    o_ref[MEM_WIDTH:, :] = kv[MEM_WIDTH:]


def _mem_kv(mem, g, w_kv, k_g):
    B, m, d = mem.shape
    consts = [g.reshape(1, d), w_kv.T.astype(BF16), jnp.tile(k_g, MEM_HEADS).reshape(MEM_WIDTH, 1),
              _seg_avg_matrix(MEM_WIDTH, 1.0 / HEAD_DIM)]
    return pl.pallas_call(
        _mem_kv_body,
        grid=(B,),
        in_specs=[pl.BlockSpec((None, m, d), lambda b: (b, 0, 0))] + [_const_spec(c.shape) for c in consts],
        out_specs=pl.BlockSpec((None, 2 * MEM_WIDTH, m), lambda b: (b, 0, 0)),
        out_shape=jax.ShapeDtypeStruct((B, 2 * MEM_WIDTH, m), F32),
        compiler_params=pltpu.CompilerParams(dimension_semantics=("parallel",)),
        name="mem_kv",
    )(mem, *consts)


def _out_mem_body(x_ref, orw_ref, oml_ref, onsa_ref, kv_ref, w1_ref, w2_ref, w3_ref, g_ref, wq_ref, gq_ref,
                  avg_ref, wo_ref, o_ref):
    N = HEAD_DIM
    nb, tm, d = x_ref.shape
    rows = lambda r: r[...].reshape(nb * tm, r.shape[-1])
    x = rows(x_ref) + _dot(rows(orw_ref), w1_ref[...]) + _dot(rows(oml_ref), w2_ref[...]) + _dot(rows(onsa_ref),
                                                                                                 w3_ref[...])
    h = _rms_rows(x, g_ref[...]).astype(BF16)
    q = jnp.dot(h, wq_ref[...], preferred_element_type=F32)
    q = q * lax.rsqrt(_seg_mean_sq(q, avg_ref[...]) + NORM_EPS) * gq_ref[...] * (N ** -0.5)
    attended = []
    for bi in range(nb):
        heads = []
        for hd in range(MEM_HEADS):
            s = _dot(q[bi * tm:(bi + 1) * tm, hd * N:(hd + 1) * N], kv_ref[bi, hd * N:(hd + 1) * N, :])
            e = jnp.exp(s - jnp.max(s, axis=-1, keepdims=True))
            p = e / jnp.sum(e, axis=-1, keepdims=True)
            heads.append(_dot_nt(p, kv_ref[bi, MEM_WIDTH + hd * N:MEM_WIDTH + (hd + 1) * N, :]))
        attended.append(jnp.concatenate(heads, axis=1))
    o = x + _dot(jnp.concatenate(attended, axis=0), wo_ref[...])
    o_ref[...] = o.reshape(nb, tm, d)


def _out_mem(x, o_rw, o_ml, o_nsa, kv_t, w_out, g, w_q, q_g, w_o):
    B, T, d = x.shape
    tm = _row_tile(T, OUT_MEM_ROWS)
    w1 = w_out[:RW_WIDTH].astype(BF16)
    w2 = w_out[RW_WIDTH:RW_WIDTH + ML_WIDTH].astype(BF16)
    w3 = w_out[RW_WIDTH + ML_WIDTH:].astype(BF16)
    consts = [w1, w2, w3, g.reshape(1, d), w_q.astype(BF16), jnp.tile(q_g, MEM_HEADS).reshape(1, MEM_WIDTH),
              _seg_avg_matrix(MEM_WIDTH, 1.0 / HEAD_DIM), w_o.astype(BF16)]
    nb = _row_tile(B, max(1, min(8, OUT_MEM_ROWS // tm)))
    tile = lambda w: pl.BlockSpec((nb, tm, w), lambda b, i: (b, i, 0))
    return pl.pallas_call(
        _out_mem_body,
        grid=(B // nb, T // tm),
        in_specs=[tile(d), tile(RW_WIDTH), tile(ML_WIDTH), tile(NSA_WIDTH),
                  pl.BlockSpec((nb,) + kv_t.shape[1:], lambda b, i: (b, 0, 0))]
                 + [_const_spec(c.shape) for c in consts],
        out_specs=tile(d),
        out_shape=jax.ShapeDtypeStruct((B, T, d), F32),
        compiler_params=pltpu.CompilerParams(dimension_semantics=("parallel", "parallel"),
                                             vmem_limit_bytes=VMEM_LIMIT),
        name="out_mem",
    )(x, o_rw, o_ml, o_nsa, kv_t, *consts)


def _layer(x, W, st, page_table, mem):
    B, T, d = x.shape
    is_prompt = st is None
    g = W['norm_g']
    G = NSA_KV_HEADS
    x1 = _ffn(x.reshape(B * T, d), g[0], W['ffa_up'], W['ffa_down']).reshape(B, T, d)
    proj = _in_proj(x1, g[1], W['w_in_perm'], W['nsa_qk_g'], channel_major=is_prompt)
    p_rw, p_ml, gates, q = proj[:4]
    if is_prompt:
        zeros = lambda *s: jnp.zeros(s, F32)
        st = {'rw_shift': zeros(B, 1, RW_COLS), 'rw_S': zeros(B, RW_HEADS, HEAD_DIM, HEAD_DIM),
              'ml_conv': zeros(B, ML_CONV - 1, ML_WIDTH), 'ml_C': zeros(B, ML_HEADS, HEAD_DIM, HEAD_DIM),
              'ml_n': zeros(B, ML_HEADS, HEAD_DIM), 'ml_m': zeros(B, ML_HEADS)}
        mem_kv_t = _mem_kv(mem, g[3], W['mem_w_kv'], W['mem_qk_g'][1])
    else:
        mem_kv_t = _channel_major(st['mem_kv'])
    o_rw, rw_S = _rwkv(p_rw, st['rw_shift'], st['rw_S'], W)
    o_ml, ml_C, ml_n, ml_m = _mlstm(p_ml, gates, st['ml_conv'], st['ml_C'], st['ml_n'], st['ml_m'], W)
    if is_prompt:
        ks, kw, kvc_t, kvs_t, kvw_t = proj[4:]
        ck, cv_t = _split_cmp(_nsa_compress_dense(kvc_t, W))
        o_nsa = _nsa_prompt(q, gates, ck, cv_t, ks, kvs_t, kw, kvw_t, W)
        new_kv = {'nsa_cmp': _token_major(kvc_t, 2, G), 'nsa_sel': _token_major(kvs_t, 2, G),
                  'nsa_win': _token_major(kvw_t[:, :, T - min(NSA_WINDOW, T):], 2, G)}
    else:
        kv_c, kv_s, kv_w = proj[4:]
        win_prev_t = _channel_major(st['nsa_win'])
        wb = win_prev_t.shape[2]
        o_nsa = _nsa_sample_mixer(q, gates, kv_s, kv_w, page_table, st['nsa_cmp_t'], st['nsa_sel_t'], win_prev_t, W)
        win_t = jnp.concatenate([win_prev_t, jnp.swapaxes(kv_w, 1, 2)], axis=2)
        kv5 = lambda z: z.reshape(B, T, 2, G, HEAD_DIM)
        new_kv = {'nsa_cmp': kv5(kv_c), 'nsa_sel': kv5(kv_s),
                  'nsa_win': _token_major(win_t[:, :, wb + T - min(NSA_WINDOW, wb + T):], 2, G)}
    x2 = _out_mem(x1, o_rw, o_ml, o_nsa, mem_kv_t, W['w_out'], g[2], W['mem_w_q'], W['mem_qk_g'][0], W['mem_w_o'])
    x3 = _ffn(x2.reshape(B * T, d), g[4], W['ffb_up'], W['ffb_down']).reshape(B, T, d)
    qk_in = p_ml[:, :, :ML_WIDTH]
    conv_all = jnp.concatenate([st['ml_conv'], qk_in], axis=1) if T < ML_CONV - 1 else qk_in
    new = dict(new_kv)
    new.update({'rw_shift': p_rw[:, T - 1:], 'rw_S': rw_S, 'ml_conv': conv_all[:, conv_all.shape[1] - (ML_CONV - 1):],
                'ml_C': ml_C, 'ml_n': ml_n, 'ml_m': ml_m})
    if is_prompt:
        new['mem_kv'] = _token_major(mem_kv_t, 2, MEM_HEADS)
    return x3, new


def kernel(x_prompt, x_sample, cache_nsa_cmp, cache_nsa_sel, cache_nsa_win, cache_mem_kv, state_rwkv_shift, state_rwkv_S, state_mlstm_conv, state_mlstm_C, state_mlstm_n, state_mlstm_m, page_table, mem_prompt, norm_g, ffa_up, ffa_down, ffb_up, ffb_down, w_in, w_out, rw_mu, rw_w0, rw_w2, rw_a0, rw_a2, rw_g2, rw_kk, rw_ka, rw_rk, rw_gn_g, rw_gn_b, ml_conv_w, ml_conv_b, ml_wq, ml_wk, ml_gate_b, ml_norm_g, ml_skip, nsa_qk_g, nsa_wpos, nsa_wc, nsa_gate_b, mem_w_q, mem_w_kv, mem_qk_g, mem_w_o):
    params = dict(norm_g=norm_g, ffa_up=ffa_up, ffa_down=ffa_down, ffb_up=ffb_up, ffb_down=ffb_down, w_out=w_out,
                  rw_mu=rw_mu, rw_w0=rw_w0, rw_w2=rw_w2, rw_a0=rw_a0, rw_a2=rw_a2, rw_g2=rw_g2, rw_kk=rw_kk,
                  rw_ka=rw_ka, rw_rk=rw_rk, rw_gn_g=rw_gn_g, rw_gn_b=rw_gn_b, ml_conv_w=ml_conv_w,
                  ml_conv_b=ml_conv_b, ml_wq=ml_wq, ml_wk=ml_wk, ml_gate_b=ml_gate_b, ml_norm_g=ml_norm_g,
                  ml_skip=ml_skip, nsa_qk_g=nsa_qk_g, nsa_wpos=nsa_wpos, nsa_wc=nsa_wc, nsa_gate_b=nsa_gate_b,
                  mem_w_q=mem_w_q, mem_w_kv=mem_w_kv, mem_qk_g=mem_qk_g, mem_w_o=mem_w_o)
    depth = norm_g.shape[0]
    y_p, y_s = x_prompt, x_sample
    new_p, new_s = [], []
    n_phys = cache_nsa_cmp.shape[1]
    all_pages = lambda c: _channel_major(c.reshape((depth * n_phys,) + c.shape[2:]))
    pool_cmp_t, pool_sel_t = all_pages(cache_nsa_cmp), all_pages(cache_nsa_sel)
    for l in range(depth):
        W = {name: v[l] for name, v in params.items()}
        W['w_in_perm'] = _permute_w_in(w_in[l])
        st = {'nsa_cmp_t': pool_cmp_t, 'nsa_sel_t': pool_sel_t, 'nsa_win': cache_nsa_win[l],
              'mem_kv': cache_mem_kv[l], 'rw_shift': state_rwkv_shift[l], 'rw_S': state_rwkv_S[l],
              'ml_conv': state_mlstm_conv[l], 'ml_C': state_mlstm_C[l], 'ml_n': state_mlstm_n[l],
              'ml_m': state_mlstm_m[l]}
        y_p, sp = _layer(y_p, W, None, None, mem_prompt)
        y_s, ss = _layer(y_s, W, st, page_table + l * n_phys, None)
        new_p.append(sp)
        new_s.append(ss)
    P = lambda name: jnp.stack([d[name] for d in new_p])
    S = lambda name: jnp.stack([d[name] for d in new_s])
    return (y_p, y_s,
            P('nsa_cmp'), S('nsa_cmp'), P('nsa_sel'), S('nsa_sel'), P('nsa_win'), S('nsa_win'),
            P('mem_kv'),
            P('rw_shift'), S('rw_shift'), P('rw_S'), S('rw_S'),
            P('ml_conv'), S('ml_conv'), P('ml_C'), S('ml_C'), P('ml_n'), S('ml_n'), P('ml_m'), S('ml_m'))
```
